```python
import jax, jax.numpy as jnp
from jax import lax
import numpy as np

D_MODEL = 1024
BATCH = 4
SEQ = 4096
DEPTH = 4

GRID_W = 64
CTX_LEN = 256
N_EVEN = (DEPTH + 1) // 2
N_ODD = DEPTH // 2
N_MOD = 6
EPS = 1e-6
D_CONV = D_MODEL // 2
CONV_W = 3
D_POOL = D_MODEL // 2
POOL_WINDOWS = (2, 4, 8, 16)
N_POOL_GROUPS = len(POOL_WINDOWS)
POOL_GROUP = D_POOL // N_POOL_GROUPS
D_MIX_IN = 3 * D_CONV + D_POOL
NA_HEADS = 16
NA_HEAD_DIM = D_MODEL // NA_HEADS
NA_ROWS = 8
NA_COLS = 16
D_FF = 2816
N_EXPERTS = 8
TOP_K = 2
D_FF_EXPERT = 3584

kernel_name = 'hybrid_conv_pool_natten_moe_dit'


def rmsnorm(x, g):
    x32 = x.astype(jnp.float32)
    y = x32 * lax.rsqrt(jnp.mean(x32 * x32, axis=-1, keepdims=True) + EPS)
    return (y * g.astype(jnp.float32)).astype(x.dtype)


def modulate(h, shift, scale):
    return h * (1 + scale) + shift


def adaln(cond, w, b):
    m = jax.nn.silu(cond) @ w + b
    return m.reshape(m.shape[:-1] + (N_MOD, D_MODEL))


def short_conv(z, w):
    L = z.shape[1]
    pad = CONV_W // 2
    zp = jnp.pad(z, ((0, 0), (pad, CONV_W - 1 - pad), (0, 0)))
    out = zp[:, 0:L] * w[0]
    for k in range(1, CONV_W):
        out = out + zp[:, k:k + L] * w[k]
    return out


def multiscale_pool(p, w_grp, scale):
    B_, L, _ = p.shape
    p32 = p.astype(jnp.float32)
    csum = jnp.concatenate([jnp.zeros_like(p32[:, :1]), jnp.cumsum(p32, axis=1)], axis=1)
    t = np.arange(L)
    pooled = []
    for g, win in enumerate(POOL_WINDOWS):
        lo = np.maximum(t - win // 2, 0)
        hi = np.minimum(t + (win - 1 - win // 2), L - 1)
        cs = csum[..., g * POOL_GROUP:(g + 1) * POOL_GROUP]
        cnt = (hi - lo + 1).astype(np.float32)[None, :, None]
        pooled.append((cs[:, hi + 1] - cs[:, lo]) / cnt)
    pooled = jnp.stack(pooled, axis=2).astype(p.dtype)
    diff = pooled - p.reshape(B_, L, N_POOL_GROUPS, POOL_GROUP)
    y = jnp.einsum('blgc,gcd->blgd', diff, w_grp)
    return y.reshape(B_, L, D_POOL) * scale


def conv_pool_mixer(h, w_in, conv_w, pool_w, pool_scale, w_out):
    u = h @ w_in
    gate_b, gate_c, val, pin = jnp.split(u, [D_CONV, 2 * D_CONV, 3 * D_CONV], axis=-1)
    y_conv = gate_b * short_conv(gate_c * val, conv_w)
    y_pool = multiscale_pool(pin, pool_w, pool_scale)
    return jnp.concatenate([y_conv, y_pool], axis=-1) @ w_out


def _heads(t):
    return t.reshape(t.shape[0], t.shape[1], NA_HEADS, NA_HEAD_DIM)


def neighbourhood_attention(q, k, v, kc, vc, rpb):
    B_, L, H, dh = q.shape
    rows = L // GRID_W
    kr = min(NA_ROWS, rows)
    scale = dh ** -0.5
    qg = q.reshape(B_, rows, GRID_W, H, dh)
    kg = k.reshape(B_, rows, GRID_W, H, dh)
    vg = v.reshape(B_, rows, GRID_W, H, dh)
    jcol = np.arange(GRID_W)
    col_start = np.clip(jcol - NA_COLS // 2, 0, GRID_W - NA_COLS)
    col_idx = col_start[:, None] + np.arange(NA_COLS)[None, :]
    dc_idx = col_idx - jcol[:, None] + (NA_COLS - 1)
    rpb_c = rpb[:, :, dc_idx]
    n_loc = kr * NA_COLS

    def row_block(r):
        q_r = lax.dynamic_index_in_dim(qg, r, axis=1, keepdims=False)
        start = jnp.clip(r - NA_ROWS // 2, 0, rows - kr)
        k_band = lax.dynamic_slice_in_dim(kg, start, kr, axis=1)
        v_band = lax.dynamic_slice_in_dim(vg, start, kr, axis=1)
        k_win = k_band[:, :, col_idx]
        v_win = v_band[:, :, col_idx]
        dr_idx = start + jnp.arange(kr) - r + (NA_ROWS - 1)
        bias = jnp.take(rpb_c, dr_idx, axis=1).transpose(0, 2, 1, 3)
        s_loc = jnp.einsum('bjhd,brjchd->bhjrc', q_r, k_win).astype(jnp.float32) * scale + bias
        s_ctx = jnp.einsum('bjhd,bnhd->bhjn', q_r, kc).astype(jnp.float32) * scale
        logits = jnp.concatenate([s_loc.reshape(B_, H, GRID_W, n_loc), s_ctx], axis=-1)
        p = jax.nn.softmax(logits, axis=-1).astype(v.dtype)
        p_loc = p[..., :n_loc].reshape(B_, H, GRID_W, kr, NA_COLS)
        p_ctx = p[..., n_loc:]
        return (jnp.einsum('bhjrc,brjchd->bjhd', p_loc, v_win)
                + jnp.einsum('bhjn,bnhd->bjhd', p_ctx, vc))

    out = lax.map(row_block, jnp.arange(rows))
    return out.transpose(1, 0, 2, 3, 4).reshape(B_, L, H * dh)


def context_attention(qc, kc, vc):
    B_, N, H, dh = qc.shape
    s = jnp.einsum('bmhd,bnhd->bhmn', qc, kc).astype(jnp.float32) * (dh ** -0.5)
    p = jax.nn.softmax(s, axis=-1).astype(vc.dtype)
    return jnp.einsum('bhmn,bnhd->bmhd', p, vc).reshape(B_, N, H * dh)


def swiglu(h, w_gate, w_up, w_down):
    return (jax.nn.silu(h @ w_gate) * (h @ w_up)) @ w_down


def moe_swiglu(h, w_router, we_gate, we_up, we_down):
    shp = h.shape
    t = h.reshape(-1, D_MODEL)
    logits = (t @ w_router).astype(jnp.float32)
    top_v, top_i = lax.top_k(logits, TOP_K)
    gates = jax.nn.softmax(top_v, axis=-1)
    comb = jnp.sum(jax.nn.one_hot(top_i, N_EXPERTS, dtype=jnp.float32) * gates[..., None], axis=1).astype(h.dtype)
    out = jnp.zeros_like(t)
    for e in range(N_EXPERTS):
        out = out + comb[:, e:e + 1] * swiglu(t, we_gate[e], we_up[e], we_down[e])
    return out.reshape(shp)


def channel_mix(layer, h, w_ff_gate, w_ff_up, w_ff_down, w_router, we_gate, we_up, we_down):
    j = layer // 2
    if layer % 2 == 0:
        return swiglu(h, w_ff_gate[j], w_ff_up[j], w_ff_down[j])
    return moe_swiglu(h, w_router[j], we_gate[j], we_up[j], we_down[j])


def setup_inputs(seed: int = 0) -> dict:
    key = jax.random.key(seed)
    ks = iter(jax.random.split(key, 32))
    D = D_MODEL

    def nrm(shape, scale):
        return jax.random.normal(next(ks), shape, jnp.float32) * scale

    return {
        'x': nrm((BATCH, SEQ, D), 1.0),
        'c': nrm((BATCH, D), 1.0),
        'ctx': nrm((BATCH, CTX_LEN, D), 1.0),
        'c_ctx': nrm((D,), 1.0),
        'w_mod': nrm((DEPTH, D, N_MOD * D), 0.5 * D ** -0.5),
        'b_mod': nrm((DEPTH, N_MOD * D), 0.02),
        'g_mix': 1.0 + nrm((DEPTH, D), 0.05),
        'g_ffn': 1.0 + nrm((DEPTH, D), 0.05),
        'g_final': 1.0 + nrm((D,), 0.05),
        'w_in_ab': nrm((N_EVEN, D, D_MIX_IN), D ** -0.5),
        'conv_w': nrm((N_EVEN, CONV_W, D_CONV), CONV_W ** -0.5),
        'pool_w': nrm((N_EVEN, N_POOL_GROUPS, POOL_GROUP, POOL_GROUP), POOL_GROUP ** -0.5),
        'pool_scale': 1.0 + nrm((N_EVEN, D_POOL), 0.1),
        'w_out_ab': nrm((N_EVEN, D_CONV + D_POOL, D), (D_CONV + D_POOL) ** -0.5),
        'w_ff_gate': nrm((N_EVEN, D, D_FF), D ** -0.5),
        'w_ff_up': nrm((N_EVEN, D, D_FF), D ** -0.5),
        'w_ff_down': nrm((N_EVEN, D_FF, D), D_FF ** -0.5),
        'w_qkv': nrm((N_ODD, D, 3 * D), D ** -0.5),
        'rpb': nrm((N_ODD, NA_HEADS, 2 * NA_ROWS - 1, 2 * NA_COLS - 1), 0.1),
        'w_out_na': nrm((N_ODD, D, D), D ** -0.5),
        'w_router': nrm((N_ODD, D, N_EXPERTS), D ** -0.5),
        'we_gate': nrm((N_ODD, N_EXPERTS, D, D_FF_EXPERT), D ** -0.5),
        'we_up': nrm((N_ODD, N_EXPERTS, D, D_FF_EXPERT), D ** -0.5),
        'we_down': nrm((N_ODD, N_EXPERTS, D_FF_EXPERT, D), D_FF_EXPERT ** -0.5),
    }


def reference(x, c, ctx, c_ctx, w_mod, b_mod, g_mix, g_ffn, g_final, w_in_ab, conv_w, pool_w, pool_scale,
              w_out_ab, w_ff_gate, w_ff_up, w_ff_down, w_qkv, rpb, w_out_na, w_router, we_gate, we_up, we_down):
    h_lat, h_ctx = x, ctx
    for i in range(DEPTH):
        last = i == DEPTH - 1
        j = i // 2
        m_lat = adaln(c, w_mod[i], b_mod[i])[:, :, None, :]
        m_ctx = adaln(c_ctx, w_mod[i], b_mod[i])[None, :, None, :]
        a_lat = modulate(rmsnorm(h_lat, g_mix[i]), m_lat[:, 0], m_lat[:, 1])
        if i % 2 == 0:
            y_lat = conv_pool_mixer(a_lat, w_in_ab[j], conv_w[j], pool_w[j], pool_scale[j], w_out_ab[j])
        else:
            a_ctx = modulate(rmsnorm(h_ctx, g_mix[i]), m_ctx[:, 0], m_ctx[:, 1])
            q, k, v = (_heads(t) for t in jnp.split(a_lat @ w_qkv[j], 3, axis=-1))
            kc, vc = (_heads(t) for t in jnp.split(a_ctx @ w_qkv[j][:, D_MODEL:], 2, axis=-1))
            y_lat = neighbourhood_attention(q, k, v, kc, vc, rpb[j]) @ w_out_na[j]
        h_lat = h_lat + m_lat[:, 2] * y_lat
        f_lat = modulate(rmsnorm(h_lat, g_ffn[i]), m_lat[:, 3], m_lat[:, 4])
        h_lat = h_lat + m_lat[:, 5] * channel_mix(i, f_lat, w_ff_gate, w_ff_up, w_ff_down,
                                                  w_router, we_gate, we_up, we_down)
        if not last:
            if i % 2 == 0:
                a_ctx = modulate(rmsnorm(h_ctx, g_mix[i]), m_ctx[:, 0], m_ctx[:, 1])
                y_ctx = conv_pool_mixer(a_ctx, w_in_ab[j], conv_w[j], pool_w[j], pool_scale[j], w_out_ab[j])
            else:
                qc = _heads(a_ctx @ w_qkv[j][:, :D_MODEL])
                y_ctx = context_attention(qc, kc, vc) @ w_out_na[j]
            h_ctx = h_ctx + m_ctx[:, 2] * y_ctx
            f_ctx = modulate(rmsnorm(h_ctx, g_ffn[i]), m_ctx[:, 3], m_ctx[:, 4])
            h_ctx = h_ctx + m_ctx[:, 5] * channel_mix(i, f_ctx, w_ff_gate, w_ff_up, w_ff_down,
                                                      w_router, we_gate, we_up, we_down)
    return rmsnorm(h_lat, g_final)
```

```python
import functools

import numpy as np
import jax
import jax.numpy as jnp
from jax import lax
from jax.experimental import pallas as pl
from jax.experimental.pallas import tpu as pltpu

F32 = jnp.float32
BF16 = jnp.bfloat16
I32 = jnp.int32

GRID_W = 64
POOL_WINDOWS = (2, 4, 8, 16)
N_MOD = 6
TOP_K = 2
EPS = 1e-6
NEG = -1e30

LANE = 128
SUBLANE = 8
TB = 256
HALO = SUBLANE
TM = 512
FC = 512
VMEM_LIMIT = 56 * 1024 * 1024


def _params(n_axes):
    return pltpu.CompilerParams(dimension_semantics=("arbitrary",) * n_axes,
                                vmem_limit_bytes=VMEM_LIMIT)


def _rms_mod(x, g, shift, scale):
    y = x * lax.rsqrt(jnp.mean(x * x, axis=-1, keepdims=True) + EPS)
    return (y * g) * (1.0 + scale) + shift


def _silu(x):
    return x * jax.nn.sigmoid(x)


def _dot(a, b):
    return jnp.dot(a, b, preferred_element_type=F32)


def _dot_nt(a, b):
    return lax.dot_general(a, b, (((1,), (1,)), ((), ())), preferred_element_type=F32)


def _adaln_kernel(cond_ref, w_ref, b_ref, o_ref):
    s = _silu(cond_ref[...]).astype(BF16)
    o_ref[...] = _dot(s, w_ref[...].astype(BF16)) + b_ref[...]


def _adaln(cond, w_mod, b_mod):
    depth, d, nd = w_mod.shape
    tn = 1536
    assert nd % tn == 0
    return pl.pallas_call(
        _adaln_kernel,
        grid=(depth, nd // tn),
        in_specs=[pl.BlockSpec((SUBLANE, d), lambda l, n: (0, 0)),
                  pl.BlockSpec((None, d, tn), lambda l, n: (l, 0, n)),
                  pl.BlockSpec((None, 1, tn), lambda l, n: (l, 0, n))],
        out_specs=pl.BlockSpec((None, SUBLANE, tn), lambda l, n: (l, 0, n)),
        out_shape=jax.ShapeDtypeStruct((depth, SUBLANE, nd), F32),
        compiler_params=_params(2),
        name="adaln",
    )(cond, w_mod, b_mod.reshape(depth, 1, nd))


def _mixer_kernel(hp_ref, hc_ref, hn_ref, mod_ref, g_ref, win_ref, cw_ref, pw_ref, ps_ref, wout_ref,
                  o_ref, xs_ref, z_ref, p_ref, *, nlat, tpl, tpc, seq_l, seq_c, d_conv, pool_group):
    i = pl.program_id(0)
    is_ctx = i >= nlat
    pos = jnp.where(is_ctx, lax.rem(i - nlat, tpc), lax.rem(i, tpl)) * TB
    seq_len = jnp.where(is_ctx, seq_c, seq_l)
    ext = TB + 2 * HALO

    xs_ref[0:HALO, :] = hp_ref[...]
    xs_ref[HALO:HALO + TB, :] = hc_ref[...]
    xs_ref[HALO + TB:ext, :] = hn_ref[...]
    mod = mod_ref[...]
    a = _rms_mod(xs_ref[...], g_ref[...], mod[0:1], mod[1:2]).astype(BF16)
    u = _dot(a, win_ref[...])
    srow = lax.broadcasted_iota(I32, (ext, 1), 0) + (pos - HALO)
    u = jnp.where((srow >= 0) & (srow < seq_len), u, 0.0)

    z_ref[...] = u[:, d_conv:2 * d_conv] * u[:, 2 * d_conv:3 * d_conv]
    p_ref[...] = u[:, 3 * d_conv:]
    cw = cw_ref[...]
    conv = (z_ref[HALO - 1:HALO - 1 + TB, :] * cw[0:1] + z_ref[HALO:HALO + TB, :] * cw[1:2]
            + z_ref[HALO + 1:HALO + 1 + TB, :] * cw[2:3])
    pieces = [u[HALO:HALO + TB, 0:d_conv] * conv]

    spos = srow[HALO:HALO + TB]
    ps = ps_ref[...]
    for g, win in enumerate(POOL_WINDOWS):
        lo, hi = win // 2, win - 1 - win // 2
        cols = slice(g * pool_group, (g + 1) * pool_group)
        acc = p_ref[HALO - lo:HALO - lo + TB, cols]
        for dlt in range(-lo + 1, hi + 1):
            acc = acc + p_ref[HALO + dlt:HALO + dlt + TB, cols]
        cnt = jnp.minimum(spos + hi, seq_len - 1) - jnp.maximum(spos - lo, 0) + 1
        diff = acc / cnt.astype(F32) - p_ref[HALO:HALO + TB, cols]
        pieces.append(_dot(diff.astype(BF16), pw_ref[g]) * ps[:, cols])
    cat = jnp.concatenate(pieces, axis=-1).astype(BF16)
    o_ref[...] = hc_ref[...] + mod[2:3] * _dot(cat, wout_ref[...])


def _mixer(h, mod, g, w_in, conv_w, pool_w, pool_scale, w_out, *, n_tiles, nlat, nb, seq_l, seq_c):
    t, d = h.shape
    d_conv = conv_w.shape[-1]
    d_pool = pool_scale.shape[-1]
    pool_group = pool_w.shape[-1]
    assert pool_group % LANE == 0 and d_conv % LANE == 0 and len(POOL_WINDOWS) == pool_w.shape[0]
    tpl, tpc = seq_l // TB, seq_c // TB
    ext = TB + 2 * HALO
    hb = TB // HALO
    last_hb = t // HALO - 1
    kern = functools.partial(_mixer_kernel, nlat=nlat, tpl=tpl, tpc=tpc, seq_l=seq_l, seq_c=seq_c,
                             d_conv=d_conv, pool_group=pool_group)
    const = lambda i: (0, 0)
    return pl.pallas_call(
        kern,
        grid=(n_tiles,),
        in_specs=[pl.BlockSpec((HALO, d), lambda i: (jnp.maximum(i * hb - 1, 0), 0)),
                  pl.BlockSpec((TB, d), lambda i: (i, 0)),
                  pl.BlockSpec((HALO, d), lambda i: (jnp.minimum((i + 1) * hb, last_hb), 0)),
                  pl.BlockSpec((None, N_MOD, d), lambda i: (jnp.minimum(i // tpl, nb), 0, 0)),
                  pl.BlockSpec((1, d), const),
                  pl.BlockSpec(w_in.shape, const),
                  pl.BlockSpec(conv_w.shape, const),
                  pl.BlockSpec(pool_w.shape, lambda i: (0, 0, 0)),
                  pl.BlockSpec((1, d_pool), const),
                  pl.BlockSpec(w_out.shape, const)],
        out_specs=pl.BlockSpec((TB, d), lambda i: (i, 0)),
        out_shape=jax.ShapeDtypeStruct((n_tiles * TB, d), F32),
        scratch_shapes=[pltpu.VMEM((ext, d), F32), pltpu.VMEM((ext, d_conv), F32),
                        pltpu.VMEM((ext, d_pool), F32)],
        compiler_params=_params(1),
        name="conv_pool_mixer",
    )(h, h, h, mod, g.reshape(1, d), w_in, conv_w, pool_w, pool_scale.reshape(1, d_pool), w_out)


def _ffn_kernel(h_ref, mod_ref, g_ref, wg_ref, wu_ref, wd_ref, o_ref, act_ref, *, chunks):
    mod = mod_ref[...]
    h = h_ref[...]
    f = _rms_mod(h, g_ref[...], mod[3:4], mod[4:5]).astype(BF16)
    for c0, c1 in chunks:
        gate = _dot(f, wg_ref[:, c0:c1])
        up = _dot(f, wu_ref[:, c0:c1])
        act_ref[:, c0:c1] = (_silu(gate) * up).astype(BF16)
    o_ref[...] = h + mod[5:6] * _dot(act_ref[...], wd_ref[...])


def _ffn(h, mod, g, wg, wu, wd, *, n_tiles, tpl, nb):
    t, d = h.shape
    dff = wg.shape[-1]
    step = 768
    chunks = tuple((c, min(c + step, dff)) for c in range(0, dff, step))
    const = lambda i: (0, 0)
    return pl.pallas_call(
        functools.partial(_ffn_kernel, chunks=chunks),
        grid=(n_tiles,),
        in_specs=[pl.BlockSpec((TB, d), lambda i: (i, 0)),
                  pl.BlockSpec((None, N_MOD, d), lambda i: (jnp.minimum(i // tpl, nb), 0, 0)),
                  pl.BlockSpec((1, d), const),
                  pl.BlockSpec(wg.shape, const),
                  pl.BlockSpec(wu.shape, const),
                  pl.BlockSpec(wd.shape, const)],
        out_specs=pl.BlockSpec((TB, d), lambda i: (i, 0)),
        out_shape=jax.ShapeDtypeStruct((n_tiles * TB, d), F32),
        scratch_shapes=[pltpu.VMEM((TB, dff), BF16)],
        compiler_params=_params(1),
        name="dense_swiglu",
    )(h, mod, g.reshape(1, d), wg, wu, wd)


def _qkv_kernel(h_ref, mod_ref, g_ref, w_ref, q_ref, k_ref, v_ref, *, d, qscale):
    mod = mod_ref[...]
    a = _rms_mod(h_ref[...], g_ref[...], mod[0:1], mod[1:2]).astype(BF16)
    q_ref[...] = (_dot(a, w_ref[:, 0:d]) * qscale).astype(BF16)
    k_ref[...] = _dot(a, w_ref[:, d:2 * d]).astype(BF16)
    v_ref[...] = _dot(a, w_ref[:, 2 * d:3 * d]).astype(BF16)


def _qkv(h, mod, g, w, *, n_tiles, tpl, nb, qscale):
    t, d = h.shape
    const = lambda i: (0, 0)
    out = jax.ShapeDtypeStruct((n_tiles * TB, d), BF16)
    return pl.pallas_call(
        functools.partial(_qkv_kernel, d=d, qscale=qscale),
        grid=(n_tiles,),
        in_specs=[pl.BlockSpec((TB, d), lambda i: (i, 0)),
                  pl.BlockSpec((None, N_MOD, d), lambda i: (jnp.minimum(i // tpl, nb), 0, 0)),
                  pl.BlockSpec((1, d), const),
                  pl.BlockSpec(w.shape, const)],
        out_specs=[pl.BlockSpec((TB, d), lambda i: (i, 0))] * 3,
        out_shape=[out, out, out],
        compiler_params=_params(1),
        name="qkv_proj",
    )(h, mod, g.reshape(1, d), w)


def _softmax_pv(s_parts, v_parts):
    m = s_parts[0].max(axis=-1, keepdims=True)
    for s in s_parts[1:]:
        m = jnp.maximum(m, s.max(axis=-1, keepdims=True))
    den = 0.0
    out = 0.0
    for s, v in zip(s_parts, v_parts):
        p = jnp.exp(s - m)
        den = den + p.sum(axis=-1, keepdims=True)
        out = out + _dot(p.astype(BF16), v)
    return out / den


def _natten_kernel(q_ref, k_ref, v_ref, kc_ref, vc_ref, qc_ref, bias_ref, o_ref, oc_ref,
                   *, rows, na_rows, head_dim):
    w = GRID_W
    band = na_rows * w
    lane = lax.broadcasted_iota(I32, (1, LANE), 1)
    head0 = lane < head_dim
    zero = jnp.zeros((), BF16)
    kc = kc_ref[...]
    vc = vc_ref[...]

    def stack(q):
        return jnp.concatenate([jnp.where(head0, q, zero), jnp.where(head0, zero, q)], axis=0)

    def unstack(o, n):
        return jnp.where(head0, o[0:n], o[n:2 * n])

    def row_body(r, carry):
        start = jnp.clip(r - na_rows // 2, 0, rows - na_rows)
        q0 = pl.multiple_of(r * w, w)
        k0 = pl.multiple_of(start * w, w)
        qq = stack(q_ref[pl.ds(q0, w), :])
        kb = k_ref[pl.ds(k0, band), :]
        vb = v_ref[pl.ds(k0, band), :]
        d0 = start - r + (na_rows - 1)
        s_loc = _dot_nt(qq, kb)
        bias = jnp.concatenate([bias_ref[d0 + 2 * p] for p in range(band // LANE)], axis=-1)
        s_ctx = _dot_nt(qq, kc)
        o = _softmax_pv([s_loc + bias, s_ctx], [vb, vc])
        o_ref[pl.ds(q0, w), :] = unstack(o, w).astype(o_ref.dtype)
        return carry

    lax.fori_loop(0, rows, row_body, 0)

    qc = qc_ref[...]
    nc = qc.shape[0]
    oc = _softmax_pv([_dot_nt(stack(qc), kc)], [vc])
    oc_ref[...] = unstack(oc, nc).astype(oc_ref.dtype)


def _bias_table(rpb, na_cols):
    heads, nr, _ = rpb.shape
    w = GRID_W
    jcol = np.arange(w)
    cstart = np.clip(jcol - na_cols // 2, 0, w - na_cols)
    kcol = np.arange(w)
    inside = (kcol[None, :] >= cstart[:, None]) & (kcol[None, :] < cstart[:, None] + na_cols)
    dc = np.clip(kcol[None, :] - jcol[:, None] + (na_cols - 1), 0, 2 * na_cols - 2)
    t2 = jnp.where(inside[None, None], rpb[:, :, dc], NEG)
    t3 = jnp.concatenate([t2[:, :-1], t2[:, 1:]], axis=-1)
    t3 = t3.reshape(heads // 2, 2, nr - 1, w, 2 * w).transpose(0, 2, 1, 3, 4)
    return t3.reshape(heads // 2, nr - 1, 2 * w, 2 * w).astype(F32)


def _natten(q, k, v, bias, *, nb, seq_l, seq_c, heads, na_rows):
    t, d = q.shape
    head_dim = d // heads
    assert 2 * head_dim == LANE and 2 * GRID_W == LANE and seq_l % GRID_W == 0
    rows = seq_l // GRID_W
    assert (nb * seq_l) % seq_c == 0
    cblk = nb * seq_l // seq_c
    lat = lambda b, hp: (b, hp)
    ctx = lambda b, hp: (cblk + b, hp)
    return pl.pallas_call(
        functools.partial(_natten_kernel, rows=rows, na_rows=na_rows, head_dim=head_dim),
        grid=(nb, heads // 2),
        in_specs=[pl.BlockSpec((seq_l, LANE), lat),
                  pl.BlockSpec((seq_l, LANE), lat),
                  pl.BlockSpec((seq_l, LANE), lat),
                  pl.BlockSpec((seq_c, LANE), ctx),
                  pl.BlockSpec((seq_c, LANE), ctx),
                  pl.BlockSpec((seq_c, LANE), ctx),
                  pl.BlockSpec((None,) + bias.shape[1:], lambda b, hp: (hp, 0, 0, 0))],
        out_specs=[pl.BlockSpec((seq_l, LANE), lat),
                   pl.BlockSpec((seq_c, LANE), lambda b, hp: (b, hp))],
        out_shape=[jax.ShapeDtypeStruct((nb * seq_l, d), BF16),
                   jax.ShapeDtypeStruct((nb * seq_c, d), BF16)],
        compiler_params=_params(2),
        name="neighbourhood_attention",
    )(q, k, v, k, v, q, bias)


def _split_bf16(x):
    hi = x.astype(BF16)
    return hi, (x - hi.astype(F32)).astype(BF16)


def _route_kernel(attn_ref, h_ref, mod_ref, g_ref, wo_ref, wr_ref, h_out_ref, f_ref, info_ref, cnt_ref,
                  *, n_experts):
    i = pl.program_id(0)

    @pl.when(i == 0)
    def _():
        cnt_ref[...] = jnp.zeros_like(cnt_ref)

    mod = mod_ref[...]
    h = h_ref[...] + mod[2:3] * _dot(attn_ref[...], wo_ref[...])
    h_out_ref[...] = h
    f = _rms_mod(h, g_ref[...], mod[3:4], mod[4:5])
    f_ref[...] = f

    f_hi, f_lo = _split_bf16(f)
    w_hi, w_lo = _split_bf16(wr_ref[...])
    logits = _dot(f_hi, w_hi) + (_dot(f_lo, w_hi) + _dot(f_hi, w_lo))
    lane_i = lax.broadcasted_iota(I32, logits.shape, 1)
    lane = lane_i.astype(F32)
    logits = jnp.where(lane_i < n_experts, logits, -jnp.inf)
    v1 = logits.max(axis=-1, keepdims=True)
    i1 = jnp.where(logits == v1, lane, float(LANE)).min(axis=-1, keepdims=True)
    rest = jnp.where(lane == i1, -jnp.inf, logits)
    v2 = rest.max(axis=-1, keepdims=True)
    i2 = jnp.where(rest == v2, lane, float(LANE)).min(axis=-1, keepdims=True)
    e2 = jnp.exp(v2 - v1)
    g1 = 1.0 / (1.0 + e2)
    g2 = e2 * g1

    sel1 = lane == i1
    sel2 = lane == i2
    onehot = jnp.where(sel1 | sel2, 1.0, 0.0)
    tr = lax.broadcasted_iota(I32, (TB, TB), 0)
    tc = lax.broadcasted_iota(I32, (TB, TB), 1)
    before = _dot(jnp.where(tc < tr, 1.0, 0.0).astype(BF16), onehot.astype(BF16)) + cnt_ref[0:1, :]
    r1 = jnp.where(sel1, before, 0.0).sum(axis=-1, keepdims=True)
    r2 = jnp.where(sel2, before, 0.0).sum(axis=-1, keepdims=True)
    cnt_ref[...] = cnt_ref[...] + onehot.sum(axis=0, keepdims=True)

    info = jnp.where(lane_i == 0, i1, 0.0)
    info = jnp.where(lane_i == 1, i2, info)
    info = jnp.where(lane_i == 2, r1, info)
    info = jnp.where(lane_i == 3, r2, info)
    info = jnp.where(lane_i == 4, g1, info)
    info = jnp.where(lane_i == 5, g2, info)
    info_ref[...] = info


def _route(attn, h, mod, g, w_out, w_router, *, n_tiles, tpl, nb):
    t, d = h.shape
    n_experts = w_router.shape[-1]
    wr = jnp.zeros((d, LANE), F32).at[:, :n_experts].set(w_router)
    const = lambda i: (0, 0)
    row = lambda i: (i, 0)
    n = n_tiles * TB
    return pl.pallas_call(
        functools.partial(_route_kernel, n_experts=n_experts),
        grid=(n_tiles,),
        in_specs=[pl.BlockSpec((TB, d), row),
                  pl.BlockSpec((TB, d), row),
                  pl.BlockSpec((None, N_MOD, d), lambda i: (jnp.minimum(i // tpl, nb), 0, 0)),
                  pl.BlockSpec((1, d), const),
                  pl.BlockSpec(w_out.shape, const),
                  pl.BlockSpec(wr.shape, const)],
        out_specs=[pl.BlockSpec((TB, d), row), pl.BlockSpec((TB, d), row),
                   pl.BlockSpec((TB, LANE), row), pl.BlockSpec((SUBLANE, LANE), const)],
        out_shape=[jax.ShapeDtypeStruct((n, d), F32), jax.ShapeDtypeStruct((n, d), F32),
                   jax.ShapeDtypeStruct((n, LANE), F32), jax.ShapeDtypeStruct((SUBLANE, LANE), F32)],
        compiler_params=_params(1),
        name="attn_out_router",
    )(attn, h, mod, g.reshape(1, d), w_out, wr)


def _row_copies(src_ref, dst_ref, idx_ref, sem, scatter):
    def issue(j, wait):
        for k in range(TOP_K):
            row = idx_ref[0, TOP_K * j + k]
            if scatter:
                cp = pltpu.make_async_copy(src_ref.at[pl.ds(j, 1)], dst_ref.at[pl.ds(row, 1)], sem)
            else:
                cp = pltpu.make_async_copy(src_ref.at[pl.ds(row, 1)], dst_ref.at[k, pl.ds(j, 1)], sem)
            if wait:
                cp.wait()
            else:
                cp.start()

    def start_body(j, c):
        issue(j, False)
        return c

    def wait_body(j, c):
        issue(j, True)
        return c

    lax.fori_loop(0, TB, start_body, 0)
    lax.fori_loop(0, TB, wait_body, 0)


def _dispatch_kernel(dest_ref, f_ref, xs_in_ref, xs_ref, sem):
    del xs_in_ref
    _row_copies(f_ref, xs_ref, dest_ref, sem, scatter=True)


def _dispatch(f, dest, xs_init, *, n_tiles):
    n, d = f.shape
    return pl.pallas_call(
        _dispatch_kernel,
        grid=(n_tiles,),
        in_specs=[pl.BlockSpec((None, 1, TOP_K * TB), lambda i: (i, 0, 0), memory_space=pltpu.SMEM),
                  pl.BlockSpec((TB, d), lambda i: (i, 0)),
                  pl.BlockSpec(memory_space=pl.ANY)],
        out_specs=pl.BlockSpec(memory_space=pl.ANY),
        out_shape=jax.ShapeDtypeStruct(xs_init.shape, xs_init.dtype),
        scratch_shapes=[pltpu.SemaphoreType.DMA(())],
        input_output_aliases={2: 0},
        compiler_params=_params(1),
        name="moe_dispatch",
    )(dest, f, xs_init)


def _experts_kernel(te_ref, nu_ref, x_ref, wg_ref, wu_ref, wd_ref, o_ref, xb_ref, acc_ref, *, n_chunks):
    i = pl.program_id(0)
    j = pl.program_id(1)

    @pl.when(i < nu_ref[0])
    def _():
        @pl.when(j == 0)
        def _():
            xb_ref[...] = x_ref[...].astype(BF16)

        xb = xb_ref[...]
        gate = _dot(xb, wg_ref[...].astype(BF16))
        up = _dot(xb, wu_ref[...].astype(BF16))
        part = _dot((_silu(gate) * up).astype(BF16), wd_ref[...].astype(BF16))

        @pl.when(j == 0)
        def _():
            acc_ref[...] = part

        @pl.when(j > 0)
        def _():
            acc_ref[...] = acc_ref[...] + part

        @pl.when(j == n_chunks - 1)
        def _():
            o_ref[...] = acc_ref[...]

    @pl.when((i >= nu_ref[0]) & (j == n_chunks - 1))
    def _():
        o_ref[...] = jnp.zeros_like(o_ref)


def _experts(xs, tile_expert, n_used, we_gate, we_up, we_down):
    p, d = xs.shape
    n_experts, _, dfe = we_gate.shape
    assert dfe % FC == 0 and p % TM == 0
    n_chunks = dfe // FC
    n_tiles = p // TM

    def tile(i, nu):
        return jnp.minimum(i, nu[0] - 1)

    def chunk(i, j, nu):
        return jnp.where(i < nu[0], j, n_chunks - 1)

    grid_spec = pltpu.PrefetchScalarGridSpec(
        num_scalar_prefetch=2,
        grid=(n_tiles, n_chunks),
        in_specs=[pl.BlockSpec((TM, d), lambda i, j, te, nu: (tile(i, nu), 0)),
                  pl.BlockSpec((None, d, FC), lambda i, j, te, nu: (te[tile(i, nu)], 0, chunk(i, j, nu))),
                  pl.BlockSpec((None, d, FC), lambda i, j, te, nu: (te[tile(i, nu)], 0, chunk(i, j, nu))),
                  pl.BlockSpec((None, FC, d), lambda i, j, te, nu: (te[tile(i, nu)], chunk(i, j, nu), 0))],
        out_specs=pl.BlockSpec((TM, d), lambda i, j, te, nu: (i, 0)),
        scratch_shapes=[pltpu.VMEM((TM, d), BF16), pltpu.VMEM((TM, d), F32)],
    )
    return pl.pallas_call(
        functools.partial(_experts_kernel, n_chunks=n_chunks),
        grid_spec=grid_spec,
        out_shape=jax.ShapeDtypeStruct((p, d), F32),
        compiler_params=_params(2),
        name="expert_swiglu",
    )(tile_expert, n_used, xs, we_gate, we_up, we_down)


def _combine_kernel(dest_ref, ys_ref, info_ref, h_ref, mod_ref, gf_ref, o_ref, ybuf, sem, *, final_norm):
    _row_copies(ys_ref, ybuf, dest_ref, sem, scatter=False)
    info = info_ref[...]
    mix = info[:, 4:5] * ybuf[0] + info[:, 5:6] * ybuf[1]
    h = h_ref[...] + mod_ref[...][5:6] * mix
    if final_norm:
        h = (h * lax.rsqrt(jnp.mean(h * h, axis=-1, keepdims=True) + EPS)) * gf_ref[...]
    o_ref[...] = h


def _combine(ys, dest, info, h, mod, g_final, *, n_tiles, tpl, nb, final_norm):
    n, d = h.shape
    row = lambda i: (i, 0)
    return pl.pallas_call(
        functools.partial(_combine_kernel, final_norm=final_norm),
        grid=(n_tiles,),
        in_specs=[pl.BlockSpec((None, 1, TOP_K * TB), lambda i: (i, 0, 0), memory_space=pltpu.SMEM),
                  pl.BlockSpec(memory_space=pl.ANY),
                  pl.BlockSpec((TB, LANE), row),
                  pl.BlockSpec((TB, d), row),
                  pl.BlockSpec((None, N_MOD, d), lambda i: (jnp.minimum(i // tpl, nb), 0, 0)),
                  pl.BlockSpec((1, d), lambda i: (0, 0))],
        out_specs=pl.BlockSpec((TB, d), row),
        out_shape=jax.ShapeDtypeStruct((n_tiles * TB, d), F32),
        scratch_shapes=[pltpu.VMEM((TOP_K, TB, d), F32), pltpu.SemaphoreType.DMA(())],
        compiler_params=_params(1),
        name="moe_combine",
    )(dest, ys, info, h, mod, g_final.reshape(1, d))


def _moe(f, info, counts, h, mod, g_final, we_gate, we_up, we_down, *, n_tiles, tpl, nb, final_norm):
    n, d = f.shape
    n_experts = we_gate.shape[0]
    n_xtiles = TOP_K * n // TM + n_experts

    cnt = counts[0, :n_experts].astype(I32)
    tiles_e = (cnt + TM - 1) // TM
    tile_end = jnp.cumsum(tiles_e)
    starts = (tile_end - tiles_e) * TM
    n_used = tile_end[-1:]
    tile_expert = jnp.minimum(
        jnp.sum(jnp.arange(n_xtiles, dtype=I32)[:, None] >= tile_end[None, :], axis=1), n_experts - 1).astype(I32)
    choice = info[:, 0:TOP_K].astype(I32)
    rank = info[:, TOP_K:2 * TOP_K].astype(I32)
    dest = (starts[choice] + rank).reshape(n_tiles, 1, TOP_K * TB)

    xs = _dispatch(f, dest, jnp.zeros((n_xtiles * TM, d), F32), n_tiles=n_tiles)
    ys = _experts(xs, tile_expert, n_used, we_gate, we_up, we_down)
    return _combine(ys, dest, info, h, mod, g_final, n_tiles=n_tiles, tpl=tpl, nb=nb, final_norm=final_norm)


def kernel(x, c, ctx, c_ctx, w_mod, b_mod, g_mix, g_ffn, g_final, w_in_ab, conv_w, pool_w, pool_scale,
           w_out_ab, w_ff_gate, w_ff_up, w_ff_down, w_qkv, rpb, w_out_na, w_router, we_gate, we_up, we_down):
    nb, seq_l, d = x.shape
    seq_c = ctx.shape[1]
    depth = w_mod.shape[0]
    heads = rpb.shape[1]
    na_rows = (rpb.shape[2] + 1) // 2
    na_cols = (rpb.shape[3] + 1) // 2
    assert seq_l % TB == 0 and seq_c % TB == 0 and nb + 1 <= SUBLANE and depth % 2 == 0
    tpl = seq_l // TB
    nlat = nb * tpl
    nall = nlat + nb * seq_c // TB

    cond = jnp.zeros((SUBLANE, d), F32).at[:nb].set(c).at[nb].set(c_ctx)
    mods = _adaln(cond, w_mod, b_mod).reshape(depth, SUBLANE, N_MOD, d)

    h = jnp.concatenate([x.reshape(nb * seq_l, d), ctx.reshape(nb * seq_c, d)], axis=0)
    bf = lambda a: a.astype(BF16)
    out = None
    for i in range(depth):
        j = i // 2
        last = i == depth - 1
        mod = mods[i]
        if i % 2 == 0:
            h = _mixer(h, mod, g_mix[i], bf(w_in_ab[j]), conv_w[j], bf(pool_w[j]), pool_scale[j], bf(w_out_ab[j]),
                       n_tiles=nall, nlat=nlat, nb=nb, seq_l=seq_l, seq_c=seq_c)
            h = _ffn(h, mod, g_ffn[i], bf(w_ff_gate[j]), bf(w_ff_up[j]), bf(w_ff_down[j]),
                     n_tiles=nall, tpl=tpl, nb=nb)
        else:
            n_tiles = nlat if last else nall
            q, k, v = _qkv(h, mod, g_mix[i], bf(w_qkv[j]), n_tiles=nall, tpl=tpl, nb=nb,
                           qscale=(d // heads) ** -0.5)
            o_lat, o_ctx = _natten(q, k, v, _bias_table(rpb[j], na_cols), nb=nb, seq_l=seq_l, seq_c=seq_c,
                                   heads=heads, na_rows=na_rows)
            attn = o_lat if last else jnp.concatenate([o_lat, o_ctx], axis=0)
            h, f, info, counts = _route(attn, h, mod, g_ffn[i], bf(w_out_na[j]), w_router[j],
                                        n_tiles=n_tiles, tpl=tpl, nb=nb)
            h = _moe(f, info, counts, h, mod, g_final, we_gate[j], we_up[j], we_down[j],
                     n_tiles=n_tiles, tpl=tpl, nb=nb, final_norm=last)
            if last:
                out = h
    return out.reshape(nb, seq_l, d)
```

```python
import functools

import numpy as np
import jax
import jax.numpy as jnp
from jax import lax
from jax.experimental import pallas as pl
from jax.experimental.pallas import tpu as pltpu

F32 = jnp.float32
BF16 = jnp.bfloat16
I32 = jnp.int32

GRID_W = 64
POOL_WINDOWS = (2, 4, 8, 16)
N_MOD = 6
TOP_K = 2
EPS = 1e-6
NEG = -1e30

LANE = 128
SUBLANE = 8
TB = 256
HALO = SUBLANE
TM = 1024
SUB = 512
FC = 512
NA_GROUP = 4
VMEM_LIMIT = 56 * 1024 * 1024


def _params(n_axes):
    return pltpu.CompilerParams(dimension_semantics=("arbitrary",) * n_axes,
                                vmem_limit_bytes=VMEM_LIMIT)


def _rms_mod(x, g, shift, scale):
    y = x * lax.rsqrt(jnp.mean(x * x, axis=-1, keepdims=True) + EPS)
    return (y * g) * (1.0 + scale) + shift


def _silu(x):
    return x * jax.nn.sigmoid(x)


def _dot(a, b):
    return jnp.dot(a, b, preferred_element_type=F32)


def _dot_nt(a, b):
    return lax.dot_general(a, b, (((1,), (1,)), ((), ())), preferred_element_type=F32)


def _adaln_kernel(cond_ref, w_ref, b_ref, o_ref):
    s = _silu(cond_ref[...]).astype(BF16)
    o_ref[...] = _dot(s, w_ref[...].astype(BF16)) + b_ref[...]


def _adaln(cond, w_mod, b_mod):
    depth, d, nd = w_mod.shape
    tn = 1536
    assert nd % tn == 0
    return pl.pallas_call(
        _adaln_kernel,
        grid=(depth, nd // tn),
        in_specs=[pl.BlockSpec((SUBLANE, d), lambda l, n: (0, 0)),
                  pl.BlockSpec((None, d, tn), lambda l, n: (l, 0, n)),
                  pl.BlockSpec((None, 1, tn), lambda l, n: (l, 0, n))],
        out_specs=pl.BlockSpec((None, SUBLANE, tn), lambda l, n: (l, 0, n)),
        out_shape=jax.ShapeDtypeStruct((depth, SUBLANE, nd), F32),
        compiler_params=_params(2),
        name="adaln",
    )(cond, w_mod, b_mod.reshape(depth, 1, nd))


def _mixer_kernel(hp_ref, hc_ref, hn_ref, mod_ref, g_ref, win_ref, cw_ref, pw_ref, ps_ref, wout_ref,
                  o_ref, xs_ref, z_ref, p_ref, *, nlat, tpl, tpc, seq_l, seq_c, d_conv, pool_group):
    i = pl.program_id(0)
    is_ctx = i >= nlat
    pos = jnp.where(is_ctx, lax.rem(i - nlat, tpc), lax.rem(i, tpl)) * TB
    seq_len = jnp.where(is_ctx, seq_c, seq_l)
    ext = TB + 2 * HALO

    xs_ref[0:HALO, :] = hp_ref[...]
    xs_ref[HALO:HALO + TB, :] = hc_ref[...]
    xs_ref[HALO + TB:ext, :] = hn_ref[...]
    mod = mod_ref[...]
    a = _rms_mod(xs_ref[...], g_ref[...], mod[0:1], mod[1:2]).astype(BF16)
    u = _dot(a, win_ref[...])
    srow = lax.broadcasted_iota(I32, (ext, 1), 0) + (pos - HALO)
    u = jnp.where((srow >= 0) & (srow < seq_len), u, 0.0)

    z_ref[...] = u[:, d_conv:2 * d_conv] * u[:, 2 * d_conv:3 * d_conv]
    p_ref[...] = u[:, 3 * d_conv:]
    cw = cw_ref[...]
    conv = (z_ref[HALO - 1:HALO - 1 + TB, :] * cw[0:1] + z_ref[HALO:HALO + TB, :] * cw[1:2]
            + z_ref[HALO + 1:HALO + 1 + TB, :] * cw[2:3])
    pieces = [u[HALO:HALO + TB, 0:d_conv] * conv]

    spos = srow[HALO:HALO + TB]
    ps = ps_ref[...]
    for g, win in enumerate(POOL_WINDOWS):
        lo, hi = win // 2, win - 1 - win // 2
        cols = slice(g * pool_group, (g + 1) * pool_group)
        acc = p_ref[HALO - lo:HALO - lo + TB, cols]
        for dlt in range(-lo + 1, hi + 1):
            acc = acc + p_ref[HALO + dlt:HALO + dlt + TB, cols]
        cnt = jnp.minimum(spos + hi, seq_len - 1) - jnp.maximum(spos - lo, 0) + 1
        diff = acc / cnt.astype(F32) - p_ref[HALO:HALO + TB, cols]
        pieces.append(_dot(diff.astype(BF16), pw_ref[g]) * ps[:, cols])
    cat = jnp.concatenate(pieces, axis=-1).astype(BF16)
    o_ref[...] = hc_ref[...] + mod[2:3] * _dot(cat, wout_ref[...])


def _mixer(h, mod, g, w_in, conv_w, pool_w, pool_scale, w_out, *, n_tiles, nlat, nb, seq_l, seq_c):
    t, d = h.shape
    d_conv = conv_w.shape[-1]
    d_pool = pool_scale.shape[-1]
    pool_group = pool_w.shape[-1]
    assert pool_group % LANE == 0 and d_conv % LANE == 0 and len(POOL_WINDOWS) == pool_w.shape[0]
    tpl, tpc = seq_l // TB, seq_c // TB
    ext = TB + 2 * HALO
    hb = TB // HALO
    last_hb = t // HALO - 1
    kern = functools.partial(_mixer_kernel, nlat=nlat, tpl=tpl, tpc=tpc, seq_l=seq_l, seq_c=seq_c,
                             d_conv=d_conv, pool_group=pool_group)
    const = lambda i: (0, 0)
    return pl.pallas_call(
        kern,
        grid=(n_tiles,),
        in_specs=[pl.BlockSpec((HALO, d), lambda i: (jnp.maximum(i * hb - 1, 0), 0)),
                  pl.BlockSpec((TB, d), lambda i: (i, 0)),
                  pl.BlockSpec((HALO, d), lambda i: (jnp.minimum((i + 1) * hb, last_hb), 0)),
                  pl.BlockSpec((None, N_MOD, d), lambda i: (jnp.minimum(i // tpl, nb), 0, 0)),
                  pl.BlockSpec((1, d), const),
                  pl.BlockSpec(w_in.shape, const),
                  pl.BlockSpec(conv_w.shape, const),
                  pl.BlockSpec(pool_w.shape, lambda i: (0, 0, 0)),
                  pl.BlockSpec((1, d_pool), const),
                  pl.BlockSpec(w_out.shape, const)],
        out_specs=pl.BlockSpec((TB, d), lambda i: (i, 0)),
        out_shape=jax.ShapeDtypeStruct((n_tiles * TB, d), F32),
        scratch_shapes=[pltpu.VMEM((ext, d), F32), pltpu.VMEM((ext, d_conv), F32),
                        pltpu.VMEM((ext, d_pool), F32)],
        compiler_params=_params(1),
        name="conv_pool_mixer",
    )(h, h, h, mod, g.reshape(1, d), w_in, conv_w, pool_w, pool_scale.reshape(1, d_pool), w_out)


def _ffn_kernel(h_ref, mod_ref, g_ref, wg_ref, wu_ref, wd_ref, o_ref, act_ref, *, chunks):
    mod = mod_ref[...]
    h = h_ref[...]
    f = _rms_mod(h, g_ref[...], mod[3:4], mod[4:5]).astype(BF16)
    for c0, c1 in chunks:
        gate = _dot(f, wg_ref[:, c0:c1])
        up = _dot(f, wu_ref[:, c0:c1])
        act_ref[:, c0:c1] = (_silu(gate) * up).astype(BF16)
    o_ref[...] = h + mod[5:6] * _dot(act_ref[...], wd_ref[...])


def _ffn(h, mod, g, wg, wu, wd, *, n_tiles, tpl, nb):
    t, d = h.shape
    dff = wg.shape[-1]
    step = 768
    chunks = tuple((c, min(c + step, dff)) for c in range(0, dff, step))
    const = lambda i: (0, 0)
    return pl.pallas_call(
        functools.partial(_ffn_kernel, chunks=chunks),
        grid=(n_tiles,),
        in_specs=[pl.BlockSpec((TB, d), lambda i: (i, 0)),
                  pl.BlockSpec((None, N_MOD, d), lambda i: (jnp.minimum(i // tpl, nb), 0, 0)),
                  pl.BlockSpec((1, d), const),
                  pl.BlockSpec(wg.shape, const),
                  pl.BlockSpec(wu.shape, const),
                  pl.BlockSpec(wd.shape, const)],
        out_specs=pl.BlockSpec((TB, d), lambda i: (i, 0)),
        out_shape=jax.ShapeDtypeStruct((n_tiles * TB, d), F32),
        scratch_shapes=[pltpu.VMEM((TB, dff), BF16)],
        compiler_params=_params(1),
        name="dense_swiglu",
    )(h, mod, g.reshape(1, d), wg, wu, wd)


def _qkv_kernel(h_ref, mod_ref, g_ref, w_ref, q_ref, k_ref, v_ref, *, d, qscale):
    mod = mod_ref[...]
    a = _rms_mod(h_ref[...], g_ref[...], mod[0:1], mod[1:2]).astype(BF16)
    q_ref[...] = (_dot(a, w_ref[:, 0:d]) * qscale).astype(BF16)
    k_ref[...] = _dot(a, w_ref[:, d:2 * d]).astype(BF16)
    v_ref[...] = _dot(a, w_ref[:, 2 * d:3 * d]).astype(BF16)


def _qkv(h, mod, g, w, *, n_tiles, tpl, nb, qscale):
    t, d = h.shape
    const = lambda i: (0, 0)
    out = jax.ShapeDtypeStruct((n_tiles * TB, d), BF16)
    return pl.pallas_call(
        functools.partial(_qkv_kernel, d=d, qscale=qscale),
        grid=(n_tiles,),
        in_specs=[pl.BlockSpec((TB, d), lambda i: (i, 0)),
                  pl.BlockSpec((None, N_MOD, d), lambda i: (jnp.minimum(i // tpl, nb), 0, 0)),
                  pl.BlockSpec((1, d), const),
                  pl.BlockSpec(w.shape, const)],
        out_specs=[pl.BlockSpec((TB, d), lambda i: (i, 0))] * 3,
        out_shape=[out, out, out],
        compiler_params=_params(1),
        name="qkv_proj",
    )(h, mod, g.reshape(1, d), w)


def _softmax_pv(s_parts, v_parts):
    m = s_parts[0].max(axis=-1, keepdims=True)
    for s in s_parts[1:]:
        m = jnp.maximum(m, s.max(axis=-1, keepdims=True))
    den = 0.0
    out = 0.0
    for s, v in zip(s_parts, v_parts):
        p = jnp.exp(s - m)
        den = den + p.sum(axis=-1, keepdims=True)
        out = out + _dot(p.astype(BF16), v)
    return out / den


def _natten_kernel(q_ref, k_ref, v_ref, kc_ref, vc_ref, qc_ref, bias_ref, o_ref, oc_ref,
                   s_scr, p_scr, den_scr, *, rows, na_rows, head_dim):
    w = GRID_W
    band = na_rows * w
    lane = lax.broadcasted_iota(I32, (1, LANE), 1)
    head0 = lane < head_dim
    zero = jnp.zeros((), BF16)
    kc = kc_ref[...]
    vc = vc_ref[...]

    def stack(q):
        return jnp.concatenate([jnp.where(head0, q, zero), jnp.where(head0, zero, q)], axis=0)

    def unstack(o, n):
        return jnp.where(head0, o[0:n], o[n:2 * n])

    def offsets(r):
        start = jnp.clip(r - na_rows // 2, 0, rows - na_rows)
        return pl.multiple_of(r * w, w), pl.multiple_of(start * w, w), start - r + (na_rows - 1)

    def scores(r, slot):
        q0, k0, d0 = offsets(r)
        qq = stack(q_ref[pl.ds(q0, w), :])
        bias = jnp.concatenate([bias_ref[d0 + 2 * p] for p in range(band // LANE)], axis=-1)
        s_scr[slot, :, 0:band] = _dot_nt(qq, k_ref[pl.ds(k0, band), :]) + bias
        s_scr[slot, :, band:] = _dot_nt(qq, kc)

    def softmax(slot):
        s = s_scr[slot]
        p = jnp.exp(s - s.max(axis=-1, keepdims=True))
        den_scr[slot] = p.sum(axis=-1, keepdims=True)
        p_scr[slot] = p.astype(BF16)

    def values(r, slot):
        q0, k0, _ = offsets(r)
        o = _dot(p_scr[slot, :, 0:band], v_ref[pl.ds(k0, band), :]) + _dot(p_scr[slot, :, band:], vc)
        o_ref[pl.ds(q0, w), :] = unstack(o / den_scr[slot], w).astype(o_ref.dtype)

    n_groups = rows // NA_GROUP
    assert rows % NA_GROUP == 0 and n_groups % 2 == 0 and n_groups >= 4

    def step(u, parity, do_scores, do_softmax, do_values):
        for g in range(NA_GROUP):
            if do_scores:
                scores(u * NA_GROUP + g, parity * NA_GROUP + g)
            if do_softmax:
                softmax((1 - parity) * NA_GROUP + g)
            if do_values:
                values((u - 2) * NA_GROUP + g, parity * NA_GROUP + g)

    step(0, 0, True, False, False)
    step(1, 1, True, True, False)

    def group_pair_body(u2, carry):
        step(2 * u2, 0, True, True, True)
        step(2 * u2 + 1, 1, True, True, True)
        return carry

    lax.fori_loop(1, n_groups // 2, group_pair_body, 0)
    step(n_groups, 0, False, True, True)
    step(n_groups + 1, 1, False, False, True)

    qc = qc_ref[...]
    nc = qc.shape[0]
    oc = _softmax_pv([_dot_nt(stack(qc), kc)], [vc])
    oc_ref[...] = unstack(oc, nc).astype(oc_ref.dtype)


def _bias_table(rpb, na_cols):
    heads, nr, _ = rpb.shape
    w = GRID_W
    jcol = np.arange(w)
    cstart = np.clip(jcol - na_cols // 2, 0, w - na_cols)
    kcol = np.arange(w)
    inside = (kcol[None, :] >= cstart[:, None]) & (kcol[None, :] < cstart[:, None] + na_cols)
    dc = np.clip(kcol[None, :] - jcol[:, None] + (na_cols - 1), 0, 2 * na_cols - 2)
    t2 = jnp.where(inside[None, None], rpb[:, :, dc], NEG)
    t3 = jnp.concatenate([t2[:, :-1], t2[:, 1:]], axis=-1)
    t3 = t3.reshape(heads // 2, 2, nr - 1, w, 2 * w).transpose(0, 2, 1, 3, 4)
    return t3.reshape(heads // 2, nr - 1, 2 * w, 2 * w).astype(F32)


def _natten(q, k, v, bias, *, nb, seq_l, seq_c, heads, na_rows):
    t, d = q.shape
    head_dim = d // heads
    assert 2 * head_dim == LANE and 2 * GRID_W == LANE and seq_l % GRID_W == 0
    rows = seq_l // GRID_W
    assert (nb * seq_l) % seq_c == 0
    cblk = nb * seq_l // seq_c
    lat = lambda b, hp: (b, hp)
    ctx = lambda b, hp: (cblk + b, hp)
    return pl.pallas_call(
        functools.partial(_natten_kernel, rows=rows, na_rows=na_rows, head_dim=head_dim),
        grid=(nb, heads // 2),
        in_specs=[pl.BlockSpec((seq_l, LANE), lat),
                  pl.BlockSpec((seq_l, LANE), lat),
                  pl.BlockSpec((seq_l, LANE), lat),
                  pl.BlockSpec((seq_c, LANE), ctx),
                  pl.BlockSpec((seq_c, LANE), ctx),
                  pl.BlockSpec((seq_c, LANE), ctx),
                  pl.BlockSpec((None,) + bias.shape[1:], lambda b, hp: (hp, 0, 0, 0))],
        out_specs=[pl.BlockSpec((seq_l, LANE), lat),
                   pl.BlockSpec((seq_c, LANE), lambda b, hp: (b, hp))],
        out_shape=[jax.ShapeDtypeStruct((nb * seq_l, d), BF16),
                   jax.ShapeDtypeStruct((nb * seq_c, d), BF16)],
        scratch_shapes=[pltpu.VMEM((2 * NA_GROUP, 2 * GRID_W, na_rows * GRID_W + seq_c), F32),
                        pltpu.VMEM((2 * NA_GROUP, 2 * GRID_W, na_rows * GRID_W + seq_c), BF16),
                        pltpu.VMEM((2 * NA_GROUP, 2 * GRID_W, 1), F32)],
        compiler_params=_params(2),
        name="neighbourhood_attention",
    )(q, k, v, k, v, q, bias)


def _split_bf16(x):
    hi = x.astype(BF16)
    return hi, (x - hi.astype(F32)).astype(BF16)


def _route_kernel(attn_ref, h_ref, mod_ref, g_ref, wo_ref, wr_ref, h_out_ref, f_ref, info_ref, cnt_ref,
                  *, n_experts):
    i = pl.program_id(0)

    @pl.when(i == 0)
    def _():
        cnt_ref[...] = jnp.zeros_like(cnt_ref)

    mod = mod_ref[...]
    h = h_ref[...] + mod[2:3] * _dot(attn_ref[...], wo_ref[...])
    h_out_ref[...] = h
    f = _rms_mod(h, g_ref[...], mod[3:4], mod[4:5])
    f_ref[...] = f

    f_hi, f_lo = _split_bf16(f)
    w_hi, w_lo = _split_bf16(wr_ref[...])
    logits = _dot(f_hi, w_hi) + (_dot(f_lo, w_hi) + _dot(f_hi, w_lo))
    lane_i = lax.broadcasted_iota(I32, logits.shape, 1)
    lane = lane_i.astype(F32)
    logits = jnp.where(lane_i < n_experts, logits, -jnp.inf)
    v1 = logits.max(axis=-1, keepdims=True)
    i1 = jnp.where(logits == v1, lane, float(LANE)).min(axis=-1, keepdims=True)
    rest = jnp.where(lane == i1, -jnp.inf, logits)
    v2 = rest.max(axis=-1, keepdims=True)
    i2 = jnp.where(rest == v2, lane, float(LANE)).min(axis=-1, keepdims=True)
    e2 = jnp.exp(v2 - v1)
    g1 = 1.0 / (1.0 + e2)
    g2 = e2 * g1

    sel1 = lane == i1
    sel2 = lane == i2
    onehot = jnp.where(sel1 | sel2, 1.0, 0.0)
    tr = lax.broadcasted_iota(I32, (TB, TB), 0)
    tc = lax.broadcasted_iota(I32, (TB, TB), 1)
    before = _dot(jnp.where(tc < tr, 1.0, 0.0).astype(BF16), onehot.astype(BF16)) + cnt_ref[0:1, :]
    r1 = jnp.where(sel1, before, 0.0).sum(axis=-1, keepdims=True)
    r2 = jnp.where(sel2, before, 0.0).sum(axis=-1, keepdims=True)
    cnt_ref[...] = cnt_ref[...] + onehot.sum(axis=0, keepdims=True)

    info = jnp.where(lane_i == 0, i1, 0.0)
    info = jnp.where(lane_i == 1, i2, info)
    info = jnp.where(lane_i == 2, r1, info)
    info = jnp.where(lane_i == 3, r2, info)
    info = jnp.where(lane_i == 4, g1, info)
    info = jnp.where(lane_i == 5, g2, info)
    info_ref[...] = info


def _route(attn, h, mod, g, w_out, w_router, *, n_tiles, tpl, nb):
    t, d = h.shape
    n_experts = w_router.shape[-1]
    wr = jnp.zeros((d, LANE), F32).at[:, :n_experts].set(w_router)
    const = lambda i: (0, 0)
    row = lambda i: (i, 0)
    n = n_tiles * TB
    return pl.pallas_call(
        functools.partial(_route_kernel, n_experts=n_experts),
        grid=(n_tiles,),
        in_specs=[pl.BlockSpec((TB, d), row),
                  pl.BlockSpec((TB, d), row),
                  pl.BlockSpec((None, N_MOD, d), lambda i: (jnp.minimum(i // tpl, nb), 0, 0)),
                  pl.BlockSpec((1, d), const),
                  pl.BlockSpec(w_out.shape, const),
                  pl.BlockSpec(wr.shape, const)],
        out_specs=[pl.BlockSpec((TB, d), row), pl.BlockSpec((TB, d), row),
                   pl.BlockSpec((TB, LANE), row), pl.BlockSpec((SUBLANE, LANE), const)],
        out_shape=[jax.ShapeDtypeStruct((n, d), F32), jax.ShapeDtypeStruct((n, d), F32),
                   jax.ShapeDtypeStruct((n, LANE), F32), jax.ShapeDtypeStruct((SUBLANE, LANE), F32)],
        compiler_params=_params(1),
        name="attn_out_router",
    )(attn, h, mod, g.reshape(1, d), w_out, wr)


def _row_copies(src_ref, dst_ref, idx_ref, sem, scatter):
    def issue(j, wait):
        for k in range(TOP_K):
            row = idx_ref[0, TOP_K * j + k]
            if scatter:
                cp = pltpu.make_async_copy(src_ref.at[pl.ds(j, 1)], dst_ref.at[pl.ds(row, 1)], sem)
            else:
                cp = pltpu.make_async_copy(src_ref.at[pl.ds(row, 1)], dst_ref.at[k, pl.ds(j, 1)], sem)
            if wait:
                cp.wait()
            else:
                cp.start()

    def start_body(j, c):
        issue(j, False)
        return c

    def wait_body(j, c):
        issue(j, True)
        return c

    lax.fori_loop(0, TB, start_body, 0, unroll=8)
    lax.fori_loop(0, TB, wait_body, 0, unroll=8)


def _dispatch_kernel(dest_ref, f_ref, xs_in_ref, xs_ref, sem):
    del xs_in_ref
    _row_copies(f_ref, xs_ref, dest_ref, sem, scatter=True)


def _dispatch(f, dest, xs_init, *, n_tiles):
    n, d = f.shape
    return pl.pallas_call(
        _dispatch_kernel,
        grid=(n_tiles,),
        in_specs=[pl.BlockSpec((None, 1, TOP_K * TB), lambda i: (i, 0, 0), memory_space=pltpu.SMEM),
                  pl.BlockSpec((TB, d), lambda i: (i, 0)),
                  pl.BlockSpec(memory_space=pl.ANY)],
        out_specs=pl.BlockSpec(memory_space=pl.ANY),
        out_shape=jax.ShapeDtypeStruct(xs_init.shape, xs_init.dtype),
        scratch_shapes=[pltpu.SemaphoreType.DMA(())],
        input_output_aliases={2: 0},
        compiler_params=_params(1),
        name="moe_dispatch",
    )(dest, f, xs_init)


def _experts_kernel(te_ref, tv_ref, nu_ref, x_ref, wg_ref, wu_ref, wd_ref, o_ref, xb_ref):
    i = pl.program_id(0)
    j = pl.program_id(1)
    used = i < nu_ref[0]
    valid = tv_ref[i]

    @pl.when(j == 0)
    def _():
        xb_ref[...] = x_ref[...].astype(BF16)
        o_ref[...] = jnp.zeros_like(o_ref)

    def compute(n_rows):
        xb = xb_ref[0:n_rows, :]
        act = _silu(_dot(xb, wg_ref[...].astype(BF16))) * _dot(xb, wu_ref[...].astype(BF16))
        o_ref[0:n_rows, :] += _dot(act.astype(BF16), wd_ref[...].astype(BF16))

    for n_rows in range(SUB, TM + 1, SUB):
        @pl.when(used & (valid > n_rows - SUB) & (valid <= n_rows))
        def _():
            compute(n_rows)


def _experts(xs, tile_expert, tile_valid, n_used, layer, we_gate, we_up, we_down):
    p, d = xs.shape
    dfe = we_gate.shape[-1]
    assert dfe % FC == 0 and p % TM == 0 and TM % SUB == 0
    n_chunks = dfe // FC
    n_tiles = p // TM

    def tile(i, nu):
        return jnp.minimum(i, nu[0] - 1)

    def chunk(i, j, nu):
        return jnp.where(i < nu[0], j, n_chunks - 1)

    grid_spec = pltpu.PrefetchScalarGridSpec(
        num_scalar_prefetch=3,
        grid=(n_tiles, n_chunks),
        in_specs=[pl.BlockSpec((TM, d), lambda i, j, te, tv, nu: (tile(i, nu), 0)),
                  pl.BlockSpec((None, None, d, FC),
                               lambda i, j, te, tv, nu: (layer, te[tile(i, nu)], 0, chunk(i, j, nu))),
                  pl.BlockSpec((None, None, d, FC),
                               lambda i, j, te, tv, nu: (layer, te[tile(i, nu)], 0, chunk(i, j, nu))),
                  pl.BlockSpec((None, None, FC, d),
                               lambda i, j, te, tv, nu: (layer, te[tile(i, nu)], chunk(i, j, nu), 0))],
        out_specs=pl.BlockSpec((TM, d), lambda i, j, te, tv, nu: (i, 0)),
        scratch_shapes=[pltpu.VMEM((TM, d), BF16)],
    )
    return pl.pallas_call(
        _experts_kernel,
        grid_spec=grid_spec,
        out_shape=jax.ShapeDtypeStruct((p, d), F32),
        compiler_params=_params(2),
        name="expert_swiglu",
    )(tile_expert, tile_valid, n_used, xs, we_gate, we_up, we_down)


def _combine_kernel(dest_ref, ys_ref, info_ref, h_ref, mod_ref, gf_ref, o_ref, ybuf, sem, *, final_norm):
    _row_copies(ys_ref, ybuf, dest_ref, sem, scatter=False)
    info = info_ref[...]
    mix = info[:, 4:5] * ybuf[0] + info[:, 5:6] * ybuf[1]
    h = h_ref[...] + mod_ref[...][5:6] * mix
    if final_norm:
        h = (h * lax.rsqrt(jnp.mean(h * h, axis=-1, keepdims=True) + EPS)) * gf_ref[...]
    o_ref[...] = h


def _combine(ys, dest, info, h, mod, g_final, *, n_tiles, tpl, nb, final_norm):
    n, d = h.shape
    row = lambda i: (i, 0)
    return pl.pallas_call(
        functools.partial(_combine_kernel, final_norm=final_norm),
        grid=(n_tiles,),
        in_specs=[pl.BlockSpec((None, 1, TOP_K * TB), lambda i: (i, 0, 0), memory_space=pltpu.SMEM),
                  pl.BlockSpec(memory_space=pl.ANY),
                  pl.BlockSpec((TB, LANE), row),
                  pl.BlockSpec((TB, d), row),
                  pl.BlockSpec((None, N_MOD, d), lambda i: (jnp.minimum(i // tpl, nb), 0, 0)),
                  pl.BlockSpec((1, d), lambda i: (0, 0))],
        out_specs=pl.BlockSpec((TB, d), row),
        out_shape=jax.ShapeDtypeStruct((n_tiles * TB, d), F32),
        scratch_shapes=[pltpu.VMEM((TOP_K, TB, d), F32), pltpu.SemaphoreType.DMA(())],
        compiler_params=_params(1),
        name="moe_combine",
    )(dest, ys, info, h, mod, g_final.reshape(1, d))


def _moe(f, info, counts, h, mod, g_final, layer, we_gate, we_up, we_down, *, n_tiles, tpl, nb, final_norm):
    n, d = f.shape
    n_experts = we_gate.shape[1]
    n_xtiles = -(-TOP_K * n // TM) + n_experts

    cnt = counts[0, :n_experts].astype(I32)
    tiles_e = (cnt + TM - 1) // TM
    tile_end = jnp.cumsum(tiles_e)
    starts = (tile_end - tiles_e) * TM
    n_used = tile_end[-1:]
    tile_ids = jnp.arange(n_xtiles, dtype=I32)
    tile_expert = jnp.minimum(jnp.sum(tile_ids[:, None] >= tile_end[None, :], axis=1), n_experts - 1).astype(I32)
    tile_valid = jnp.clip(cnt[tile_expert] - (tile_ids * TM - starts[tile_expert]), 0, TM).astype(I32)
    choice = info[:, 0:TOP_K].astype(I32)
    rank = info[:, TOP_K:2 * TOP_K].astype(I32)
    dest = (starts[choice] + rank).reshape(n_tiles, 1, TOP_K * TB)

    xs = _dispatch(f, dest, jnp.zeros((n_xtiles * TM, d), F32), n_tiles=n_tiles)
    ys = _experts(xs, tile_expert, tile_valid, n_used, layer, we_gate, we_up, we_down)
    return _combine(ys, dest, info, h, mod, g_final, n_tiles=n_tiles, tpl=tpl, nb=nb, final_norm=final_norm)


def kernel(x, c, ctx, c_ctx, w_mod, b_mod, g_mix, g_ffn, g_final, w_in_ab, conv_w, pool_w, pool_scale,
           w_out_ab, w_ff_gate, w_ff_up, w_ff_down, w_qkv, rpb, w_out_na, w_router, we_gate, we_up, we_down):
    nb, seq_l, d = x.shape
    seq_c = ctx.shape[1]
    depth = w_mod.shape[0]
    heads = rpb.shape[1]
    na_rows = (rpb.shape[2] + 1) // 2
    na_cols = (rpb.shape[3] + 1) // 2
    assert seq_l % TB == 0 and seq_c % TB == 0 and nb + 1 <= SUBLANE and depth % 2 == 0
    tpl = seq_l // TB
    nlat = nb * tpl
    nall = nlat + nb * seq_c // TB

    cond = jnp.zeros((SUBLANE, d), F32).at[:nb].set(c).at[nb].set(c_ctx)
    mods = _adaln(cond, w_mod, b_mod).reshape(depth, SUBLANE, N_MOD, d)

    h = jnp.concatenate([x.reshape(nb * seq_l, d), ctx.reshape(nb * seq_c, d)], axis=0)
    bf = lambda a: a.astype(BF16)
    out = None
    for i in range(depth):
        j = i // 2
        last = i == depth - 1
        mod = mods[i]
        if i % 2 == 0:
            h = _mixer(h, mod, g_mix[i], bf(w_in_ab[j]), conv_w[j], bf(pool_w[j]), pool_scale[j], bf(w_out_ab[j]),
                       n_tiles=nall, nlat=nlat, nb=nb, seq_l=seq_l, seq_c=seq_c)
            h = _ffn(h, mod, g_ffn[i], bf(w_ff_gate[j]), bf(w_ff_up[j]), bf(w_ff_down[j]),
                     n_tiles=nall, tpl=tpl, nb=nb)
        else:
            n_tiles = nlat if last else nall
            q, k, v = _qkv(h, mod, g_mix[i], bf(w_qkv[j]), n_tiles=nall, tpl=tpl, nb=nb,
                           qscale=(d // heads) ** -0.5)
            o_lat, o_ctx = _natten(q, k, v, _bias_table(rpb[j], na_cols), nb=nb, seq_l=seq_l, seq_c=seq_c,
                                   heads=heads, na_rows=na_rows)
            attn = o_lat if last else jnp.concatenate([o_lat, o_ctx], axis=0)
            h, f, info, counts = _route(attn, h, mod, g_ffn[i], bf(w_out_na[j]), w_router[j],
                                        n_tiles=n_tiles, tpl=tpl, nb=nb)
            h = _moe(f, info, counts, h, mod, g_final, j, we_gate, we_up, we_down,
                     n_tiles=n_tiles, tpl=tpl, nb=nb, final_norm=last)
            if last:
                out = h
    return out.reshape(nb, seq_l, d)
```

```python
import functools

import numpy as np
import jax
import jax.numpy as jnp
from jax import lax
from jax.experimental import pallas as pl
from jax.experimental.pallas import tpu as pltpu

F32 = jnp.float32
BF16 = jnp.bfloat16
I32 = jnp.int32

GRID_W = 64
POOL_WINDOWS = (2, 4, 8, 16)
N_MOD = 6
TOP_K = 2
EPS = 1e-6
NEG = -1e30

LANE = 128
SUBLANE = 8
TB = 256
HALO = SUBLANE
TM = 1024
SUB = 256
FC = 512
NA_GROUP = 4
VMEM_LIMIT = 56 * 1024 * 1024


def _params(n_axes):
    return pltpu.CompilerParams(dimension_semantics=("arbitrary",) * n_axes,
                                vmem_limit_bytes=VMEM_LIMIT)


def _rms_mod(x, g, shift, scale):
    y = x * lax.rsqrt(jnp.mean(x * x, axis=-1, keepdims=True) + EPS)
    return (y * g) * (1.0 + scale) + shift


def _silu(x):
    return x * jax.nn.sigmoid(x)


def _dot(a, b):
    return jnp.dot(a, b, preferred_element_type=F32)


def _dot_nt(a, b):
    return lax.dot_general(a, b, (((1,), (1,)), ((), ())), preferred_element_type=F32)


def _adaln_kernel(cond_ref, w_ref, b_ref, o_ref):
    s = _silu(cond_ref[...]).astype(BF16)
    o_ref[...] = _dot(s, w_ref[...].astype(BF16)) + b_ref[...]


def _adaln(cond, w_mod, b_mod):
    depth, d, nd = w_mod.shape
    tn = 1536
    assert nd % tn == 0
    return pl.pallas_call(
        _adaln_kernel,
        grid=(depth, nd // tn),
        in_specs=[pl.BlockSpec((SUBLANE, d), lambda l, n: (0, 0)),
                  pl.BlockSpec((None, d, tn), lambda l, n: (l, 0, n)),
                  pl.BlockSpec((None, 1, tn), lambda l, n: (l, 0, n))],
        out_specs=pl.BlockSpec((None, SUBLANE, tn), lambda l, n: (l, 0, n)),
        out_shape=jax.ShapeDtypeStruct((depth, SUBLANE, nd), F32),
        compiler_params=_params(2),
        name="adaln",
    )(cond, w_mod, b_mod.reshape(depth, 1, nd))


def _mixer_kernel(hp_ref, hc_ref, hn_ref, mod_ref, g_ref, win_ref, cw_ref, pw_ref, ps_ref, wout_ref,
                  o_ref, xs_ref, z_ref, p_ref, *, nlat, tpl, tpc, seq_l, seq_c, d_conv, pool_group):
    i = pl.program_id(0)
    is_ctx = i >= nlat
    pos = jnp.where(is_ctx, lax.rem(i - nlat, tpc), lax.rem(i, tpl)) * TB
    seq_len = jnp.where(is_ctx, seq_c, seq_l)
    ext = TB + 2 * HALO

    xs_ref[0:HALO, :] = hp_ref[...]
    xs_ref[HALO:HALO + TB, :] = hc_ref[...]
    xs_ref[HALO + TB:ext, :] = hn_ref[...]
    mod = mod_ref[...]
    a = _rms_mod(xs_ref[...], g_ref[...], mod[0:1], mod[1:2]).astype(BF16)
    u = _dot(a, win_ref[...])
    srow = lax.broadcasted_iota(I32, (ext, 1), 0) + (pos - HALO)
    u = jnp.where((srow >= 0) & (srow < seq_len), u, 0.0)

    z_ref[...] = u[:, d_conv:2 * d_conv] * u[:, 2 * d_conv:3 * d_conv]
    p_ref[...] = u[:, 3 * d_conv:]
    cw = cw_ref[...]
    conv = (z_ref[HALO - 1:HALO - 1 + TB, :] * cw[0:1] + z_ref[HALO:HALO + TB, :] * cw[1:2]
            + z_ref[HALO + 1:HALO + 1 + TB, :] * cw[2:3])
    pieces = [u[HALO:HALO + TB, 0:d_conv] * conv]

    spos = srow[HALO:HALO + TB]
    ps = ps_ref[...]
    for g, win in enumerate(POOL_WINDOWS):
        lo, hi = win // 2, win - 1 - win // 2
        cols = slice(g * pool_group, (g + 1) * pool_group)
        acc = p_ref[HALO - lo:HALO - lo + TB, cols]
        for dlt in range(-lo + 1, hi + 1):
            acc = acc + p_ref[HALO + dlt:HALO + dlt + TB, cols]
        cnt = jnp.minimum(spos + hi, seq_len - 1) - jnp.maximum(spos - lo, 0) + 1
        diff = acc / cnt.astype(F32) - p_ref[HALO:HALO + TB, cols]
        pieces.append(_dot(diff.astype(BF16), pw_ref[g]) * ps[:, cols])
    cat = jnp.concatenate(pieces, axis=-1).astype(BF16)
    o_ref[...] = hc_ref[...] + mod[2:3] * _dot(cat, wout_ref[...])


def _mixer(h, mod, g, w_in, conv_w, pool_w, pool_scale, w_out, *, n_tiles, nlat, nb, seq_l, seq_c):
    t, d = h.shape
    d_conv = conv_w.shape[-1]
    d_pool = pool_scale.shape[-1]
    pool_group = pool_w.shape[-1]
    assert pool_group % LANE == 0 and d_conv % LANE == 0 and len(POOL_WINDOWS) == pool_w.shape[0]
    tpl, tpc = seq_l // TB, seq_c // TB
    ext = TB + 2 * HALO
    hb = TB // HALO
    last_hb = t // HALO - 1
    kern = functools.partial(_mixer_kernel, nlat=nlat, tpl=tpl, tpc=tpc, seq_l=seq_l, seq_c=seq_c,
                             d_conv=d_conv, pool_group=pool_group)
    const = lambda i: (0, 0)
    return pl.pallas_call(
        kern,
        grid=(n_tiles,),
        in_specs=[pl.BlockSpec((HALO, d), lambda i: (jnp.maximum(i * hb - 1, 0), 0)),
                  pl.BlockSpec((TB, d), lambda i: (i, 0)),
                  pl.BlockSpec((HALO, d), lambda i: (jnp.minimum((i + 1) * hb, last_hb), 0)),
                  pl.BlockSpec((None, N_MOD, d), lambda i: (jnp.minimum(i // tpl, nb), 0, 0)),
                  pl.BlockSpec((1, d), const),
                  pl.BlockSpec(w_in.shape, const),
                  pl.BlockSpec(conv_w.shape, const),
                  pl.BlockSpec(pool_w.shape, lambda i: (0, 0, 0)),
                  pl.BlockSpec((1, d_pool), const),
                  pl.BlockSpec(w_out.shape, const)],
        out_specs=pl.BlockSpec((TB, d), lambda i: (i, 0)),
        out_shape=jax.ShapeDtypeStruct((n_tiles * TB, d), F32),
        scratch_shapes=[pltpu.VMEM((ext, d), F32), pltpu.VMEM((ext, d_conv), F32),
                        pltpu.VMEM((ext, d_pool), F32)],
        compiler_params=_params(1),
        name="conv_pool_mixer",
    )(h, h, h, mod, g.reshape(1, d), w_in, conv_w, pool_w, pool_scale.reshape(1, d_pool), w_out)


def _ffn_kernel(h_ref, mod_ref, g_ref, wg_ref, wu_ref, wd_ref, o_ref, act_ref, *, chunks):
    mod = mod_ref[...]
    h = h_ref[...]
    f = _rms_mod(h, g_ref[...], mod[3:4], mod[4:5]).astype(BF16)
    for c0, c1 in chunks:
        gate = _dot(f, wg_ref[:, c0:c1])
        up = _dot(f, wu_ref[:, c0:c1])
        act_ref[:, c0:c1] = (_silu(gate) * up).astype(BF16)
    o_ref[...] = h + mod[5:6] * _dot(act_ref[...], wd_ref[...])


def _ffn(h, mod, g, wg, wu, wd, *, n_tiles, tpl, nb):
    t, d = h.shape
    dff = wg.shape[-1]
    step = 768
    chunks = tuple((c, min(c + step, dff)) for c in range(0, dff, step))
    const = lambda i: (0, 0)
    return pl.pallas_call(
        functools.partial(_ffn_kernel, chunks=chunks),
        grid=(n_tiles,),
        in_specs=[pl.BlockSpec((TB, d), lambda i: (i, 0)),
                  pl.BlockSpec((None, N_MOD, d), lambda i: (jnp.minimum(i // tpl, nb), 0, 0)),
                  pl.BlockSpec((1, d), const),
                  pl.BlockSpec(wg.shape, const),
                  pl.BlockSpec(wu.shape, const),
                  pl.BlockSpec(wd.shape, const)],
        out_specs=pl.BlockSpec((TB, d), lambda i: (i, 0)),
        out_shape=jax.ShapeDtypeStruct((n_tiles * TB, d), F32),
        scratch_shapes=[pltpu.VMEM((TB, dff), BF16)],
        compiler_params=_params(1),
        name="dense_swiglu",
    )(h, mod, g.reshape(1, d), wg, wu, wd)


def _qkv_kernel(h_ref, mod_ref, g_ref, w_ref, q_ref, k_ref, v_ref, *, d, qscale):
    mod = mod_ref[...]
    a = _rms_mod(h_ref[...], g_ref[...], mod[0:1], mod[1:2]).astype(BF16)
    q_ref[...] = (_dot(a, w_ref[:, 0:d]) * qscale).astype(BF16)
    k_ref[...] = _dot(a, w_ref[:, d:2 * d]).astype(BF16)
    v_ref[...] = _dot(a, w_ref[:, 2 * d:3 * d]).astype(BF16)


def _qkv(h, mod, g, w, *, n_tiles, tpl, nb, qscale):
    t, d = h.shape
    const = lambda i: (0, 0)
    out = jax.ShapeDtypeStruct((n_tiles * TB, d), BF16)
    return pl.pallas_call(
        functools.partial(_qkv_kernel, d=d, qscale=qscale),
        grid=(n_tiles,),
        in_specs=[pl.BlockSpec((TB, d), lambda i: (i, 0)),
                  pl.BlockSpec((None, N_MOD, d), lambda i: (jnp.minimum(i // tpl, nb), 0, 0)),
                  pl.BlockSpec((1, d), const),
                  pl.BlockSpec(w.shape, const)],
        out_specs=[pl.BlockSpec((TB, d), lambda i: (i, 0))] * 3,
        out_shape=[out, out, out],
        compiler_params=_params(1),
        name="qkv_proj",
    )(h, mod, g.reshape(1, d), w)


def _softmax_pv(s_parts, v_parts):
    m = s_parts[0].max(axis=-1, keepdims=True)
    for s in s_parts[1:]:
        m = jnp.maximum(m, s.max(axis=-1, keepdims=True))
    den = 0.0
    out = 0.0
    for s, v in zip(s_parts, v_parts):
        p = jnp.exp(s - m)
        den = den + p.sum(axis=-1, keepdims=True)
        out = out + _dot(p.astype(BF16), v)
    return out / den


def _natten_kernel(q_ref, k_ref, v_ref, kc_ref, vc_ref, qc_ref, bias_ref, o_ref, oc_ref,
                   s_scr, p_scr, den_scr, *, rows, na_rows, head_dim):
    w = GRID_W
    band = na_rows * w
    lane = lax.broadcasted_iota(I32, (1, LANE), 1)
    head0 = lane < head_dim
    zero = jnp.zeros((), BF16)
    kc = kc_ref[...]
    vc = vc_ref[...]

    def stack(q):
        return jnp.concatenate([jnp.where(head0, q, zero), jnp.where(head0, zero, q)], axis=0)

    def unstack(o, n):
        return jnp.where(head0, o[0:n], o[n:2 * n])

    def offsets(r):
        start = jnp.clip(r - na_rows // 2, 0, rows - na_rows)
        return pl.multiple_of(r * w, w), pl.multiple_of(start * w, w), start - r + (na_rows - 1)

    def scores(r, slot):
        q0, k0, d0 = offsets(r)
        qq = stack(q_ref[pl.ds(q0, w), :])
        bias = jnp.concatenate([bias_ref[d0 + 2 * p] for p in range(band // LANE)], axis=-1)
        s_scr[slot, :, 0:band] = _dot_nt(qq, k_ref[pl.ds(k0, band), :]) + bias
        s_scr[slot, :, band:] = _dot_nt(qq, kc)

    def softmax(slot):
        s = s_scr[slot]
        p = jnp.exp(s - s.max(axis=-1, keepdims=True))
        den_scr[slot] = p.sum(axis=-1, keepdims=True)
        p_scr[slot] = p.astype(BF16)

    def values(r, slot):
        q0, k0, _ = offsets(r)
        o = _dot(p_scr[slot, :, 0:band], v_ref[pl.ds(k0, band), :]) + _dot(p_scr[slot, :, band:], vc)
        o_ref[pl.ds(q0, w), :] = unstack(o / den_scr[slot], w).astype(o_ref.dtype)

    n_groups = rows // NA_GROUP
    assert rows % NA_GROUP == 0 and n_groups % 2 == 0 and n_groups >= 4

    def step(u, parity, do_scores, do_softmax, do_values):
        for g in range(NA_GROUP):
            if do_scores:
                scores(u * NA_GROUP + g, parity * NA_GROUP + g)
            if do_softmax:
                softmax((1 - parity) * NA_GROUP + g)
            if do_values:
                values((u - 2) * NA_GROUP + g, parity * NA_GROUP + g)

    step(0, 0, True, False, False)
    step(1, 1, True, True, False)

    def group_pair_body(u2, carry):
        step(2 * u2, 0, True, True, True)
        step(2 * u2 + 1, 1, True, True, True)
        return carry

    lax.fori_loop(1, n_groups // 2, group_pair_body, 0)
    step(n_groups, 0, False, True, True)
    step(n_groups + 1, 1, False, False, True)

    qc = qc_ref[...]
    nc = qc.shape[0]
    oc = _softmax_pv([_dot_nt(stack(qc), kc)], [vc])
    oc_ref[...] = unstack(oc, nc).astype(oc_ref.dtype)


def _bias_table(rpb, na_cols):
    heads, nr, nc = rpb.shape
    w = GRID_W
    jcol = np.arange(w)
    cstart = np.clip(jcol - na_cols // 2, 0, w - na_cols)
    kcol = np.arange(w)
    inside = (kcol[None, :] >= cstart[:, None]) & (kcol[None, :] < cstart[:, None] + na_cols)
    dc = kcol[None, :] - jcol[:, None] + (na_cols - 1)
    pick = jnp.asarray((dc[None] == np.arange(nc)[:, None, None]) & inside[None], F32)
    t2 = jnp.einsum("hdm,mqk->hdqk", rpb, pick, precision=lax.Precision.HIGHEST)
    t2 = jnp.where(inside[None, None], t2, NEG)
    t3 = jnp.concatenate([t2[:, :-1], t2[:, 1:]], axis=-1)
    t3 = t3.reshape(heads // 2, 2, nr - 1, w, 2 * w).transpose(0, 2, 1, 3, 4)
    return t3.reshape(heads // 2, nr - 1, 2 * w, 2 * w).astype(F32)


def _natten(q, k, v, bias, *, nb, seq_l, seq_c, heads, na_rows):
    t, d = q.shape
    head_dim = d // heads
    assert 2 * head_dim == LANE and 2 * GRID_W == LANE and seq_l % GRID_W == 0
    rows = seq_l // GRID_W
    assert (nb * seq_l) % seq_c == 0
    cblk = nb * seq_l // seq_c
    lat = lambda b, hp: (b, hp)
    ctx = lambda b, hp: (cblk + b, hp)
    return pl.pallas_call(
        functools.partial(_natten_kernel, rows=rows, na_rows=na_rows, head_dim=head_dim),
        grid=(nb, heads // 2),
        in_specs=[pl.BlockSpec((seq_l, LANE), lat),
                  pl.BlockSpec((seq_l, LANE), lat),
                  pl.BlockSpec((seq_l, LANE), lat),
                  pl.BlockSpec((seq_c, LANE), ctx),
                  pl.BlockSpec((seq_c, LANE), ctx),
                  pl.BlockSpec((seq_c, LANE), ctx),
                  pl.BlockSpec((None,) + bias.shape[1:], lambda b, hp: (hp, 0, 0, 0))],
        out_specs=[pl.BlockSpec((seq_l, LANE), lat),
                   pl.BlockSpec((seq_c, LANE), lambda b, hp: (b, hp))],
        out_shape=[jax.ShapeDtypeStruct((nb * seq_l, d), BF16),
                   jax.ShapeDtypeStruct((nb * seq_c, d), BF16)],
        scratch_shapes=[pltpu.VMEM((2 * NA_GROUP, 2 * GRID_W, na_rows * GRID_W + seq_c), F32),
                        pltpu.VMEM((2 * NA_GROUP, 2 * GRID_W, na_rows * GRID_W + seq_c), BF16),
                        pltpu.VMEM((2 * NA_GROUP, 2 * GRID_W, 1), F32)],
        compiler_params=_params(2),
        name="neighbourhood_attention",
    )(q, k, v, k, v, q, bias)


def _split_bf16(x):
    hi = x.astype(BF16)
    return hi, (x - hi.astype(F32)).astype(BF16)


def _route_kernel(al_ref, ac_ref, h_ref, mod_ref, g_ref, wo_ref, wr_ref, h_out_ref, f_ref, info_ref, rt_ref,
                  cnt_ref, *, n_experts, nlat):
    i = pl.program_id(0)

    @pl.when(i == 0)
    def _():
        cnt_ref[...] = jnp.zeros_like(cnt_ref)

    mod = mod_ref[...]
    attn = jnp.where(i < nlat, al_ref[...], ac_ref[...])
    h = h_ref[...] + mod[2:3] * _dot(attn, wo_ref[...])
    h_out_ref[...] = h
    f = _rms_mod(h, g_ref[...], mod[3:4], mod[4:5])
    f_ref[...] = f

    f_hi, f_lo = _split_bf16(f)
    w_hi, w_lo = _split_bf16(wr_ref[...])
    logits = _dot(f_hi, w_hi) + (_dot(f_lo, w_hi) + _dot(f_hi, w_lo))
    lane_i = lax.broadcasted_iota(I32, logits.shape, 1)
    lane = lane_i.astype(F32)
    logits = jnp.where(lane_i < n_experts, logits, -jnp.inf)
    v1 = logits.max(axis=-1, keepdims=True)
    i1 = jnp.where(logits == v1, lane, float(LANE)).min(axis=-1, keepdims=True)
    rest = jnp.where(lane == i1, -jnp.inf, logits)
    v2 = rest.max(axis=-1, keepdims=True)
    i2 = jnp.where(rest == v2, lane, float(LANE)).min(axis=-1, keepdims=True)
    e2 = jnp.exp(v2 - v1)
    g1 = 1.0 / (1.0 + e2)
    g2 = e2 * g1

    sel1 = lane == i1
    sel2 = lane == i2
    onehot = jnp.where(sel1 | sel2, 1.0, 0.0)
    tr = lax.broadcasted_iota(I32, (TB, TB), 0)
    tc = lax.broadcasted_iota(I32, (TB, TB), 1)
    before = _dot(jnp.where(tc < tr, 1.0, 0.0).astype(BF16), onehot.astype(BF16)) + cnt_ref[0:1, :]
    r1 = jnp.where(sel1, before, 0.0).sum(axis=-1, keepdims=True)
    r2 = jnp.where(sel2, before, 0.0).sum(axis=-1, keepdims=True)
    cnt_ref[...] = cnt_ref[...] + onehot.sum(axis=0, keepdims=True)

    info = jnp.where(lane_i == 0, i1, 0.0)
    info = jnp.where(lane_i == 1, i2, info)
    info = jnp.where(lane_i == 2, r1, info)
    info = jnp.where(lane_i == 3, r2, info)
    info = jnp.where(lane_i == 4, g1, info)
    info = jnp.where(lane_i == 5, g2, info)
    info_ref[...] = info
    rt_ref[...] = jnp.transpose(info)[0:SUBLANE, :].astype(I32)


def _route(a_lat, a_ctx, h, mod, g, w_out, w_router, *, n_tiles, nlat, tpl, nb):
    t, d = h.shape
    assert d == SUBLANE * LANE
    n_experts = w_router.shape[-1]
    wr = jnp.zeros((d, LANE), F32).at[:, :n_experts].set(w_router)
    const = lambda i: (0, 0)
    row = lambda i: (i, 0)
    n = n_tiles * TB
    return pl.pallas_call(
        functools.partial(_route_kernel, n_experts=n_experts, nlat=nlat),
        grid=(n_tiles,),
        in_specs=[pl.BlockSpec((TB, d), lambda i: (jnp.minimum(i, nlat - 1), 0)),
                  pl.BlockSpec((TB, d), lambda i: (jnp.maximum(i - nlat, 0), 0)),
                  pl.BlockSpec((TB, d), row),
                  pl.BlockSpec((None, N_MOD, d), lambda i: (jnp.minimum(i // tpl, nb), 0, 0)),
                  pl.BlockSpec((1, d), const),
                  pl.BlockSpec(w_out.shape, const),
                  pl.BlockSpec(wr.shape, const)],
        out_specs=[pl.BlockSpec((TB, d), row),
                   pl.BlockSpec((TB, d), row),
                   pl.BlockSpec((TB, LANE), row),
                   pl.BlockSpec((None, SUBLANE, TB), lambda i: (i, 0, 0)),
                   pl.BlockSpec((SUBLANE, LANE), const)],
        out_shape=[jax.ShapeDtypeStruct((n, d), F32),
                   jax.ShapeDtypeStruct((n, d), F32),
                   jax.ShapeDtypeStruct((n, LANE), F32),
                   jax.ShapeDtypeStruct((n_tiles, SUBLANE, TB), I32),
                   jax.ShapeDtypeStruct((SUBLANE, LANE), F32)],
        compiler_params=_params(1),
        name="attn_out_router",
    )(a_lat, a_ctx, h, mod, g.reshape(1, d), w_out, wr)


def _row_copies(src_ref, dst_ref, idx_ref, sem, scatter):
    def issue(jj, wait):
        for u in range(SUBLANE):
            for k in range(TOP_K):
                row = idx_ref[0, k * TB + jj * SUBLANE + u]
                if scatter:
                    cp = pltpu.make_async_copy(src_ref.at[jj, pl.ds(u, 1)], dst_ref.at[pl.ds(row, 1)], sem)
                else:
                    cp = pltpu.make_async_copy(src_ref.at[pl.ds(row, 1)], dst_ref.at[k, jj, pl.ds(u, 1)], sem)
                if wait:
                    cp.wait()
                else:
                    cp.start()

    def start_body(jj, c):
        issue(jj, False)
        return c

    def wait_body(jj, c):
        issue(jj, True)
        return c

    lax.fori_loop(0, TB // SUBLANE, start_body, 0)
    lax.fori_loop(0, TB // SUBLANE, wait_body, 0)


def _dispatch_kernel(pad_ref, dest_ref, f_ref, xs_ref, zbuf, sem, zsem, *, n_fills):
    @pl.when(pl.program_id(0) == 0)
    def _():
        zbuf[...] = jnp.zeros_like(zbuf)

        def fill(e):
            return pltpu.make_async_copy(zbuf, xs_ref.at[pl.ds(pl.multiple_of(pad_ref[e] * TM, TM), TM)], zsem)

        for e in range(n_fills):
            @pl.when(pad_ref[e] >= 0)
            def _():
                fill(e).start()
        for e in range(n_fills):
            @pl.when(pad_ref[e] >= 0)
            def _():
                fill(e).wait()

    _row_copies(f_ref, xs_ref, dest_ref, sem, scatter=True)


def _dispatch(f, dest, pad_at, *, n_tiles, n_rows):
    n_fills = pad_at.shape[0]
    n, d = f.shape
    grid_spec = pltpu.PrefetchScalarGridSpec(
        num_scalar_prefetch=1,
        grid=(n_tiles,),
        in_specs=[pl.BlockSpec((None, 1, TOP_K * TB), lambda i, pad: (i, 0, 0), memory_space=pltpu.SMEM),
                  pl.BlockSpec((TB // SUBLANE, SUBLANE, d), lambda i, pad: (i, 0, 0))],
        out_specs=pl.BlockSpec(memory_space=pl.ANY),
        scratch_shapes=[pltpu.VMEM((TM, d), F32), pltpu.SemaphoreType.DMA(()), pltpu.SemaphoreType.DMA(())],
    )
    return pl.pallas_call(
        functools.partial(_dispatch_kernel, n_fills=n_fills),
        grid_spec=grid_spec,
        out_shape=jax.ShapeDtypeStruct((n_rows, d), F32),
        compiler_params=_params(1),
        name="moe_dispatch",
    )(pad_at, dest, f.reshape(n // SUBLANE, SUBLANE, d))


def _experts_kernel(te_ref, tv_ref, nu_ref, x_ref, wg_ref, wu_ref, wd_ref, o_ref, xb_ref):
    i = pl.program_id(0)
    j = pl.program_id(1)
    used = i < nu_ref[0]
    valid = tv_ref[i]

    @pl.when(j == 0)
    def _():
        o_ref[...] = jnp.zeros_like(o_ref)

    def compute(n_rows):
        @pl.when(j == 0)
        def _():
            xb_ref[0:n_rows, :] = x_ref[0:n_rows, :].astype(BF16)

        xb = xb_ref[0:n_rows, :]
        act = _silu(_dot(xb, wg_ref[...].astype(BF16))) * _dot(xb, wu_ref[...].astype(BF16))
        o_ref[0:n_rows, :] += _dot(act.astype(BF16), wd_ref[...].astype(BF16))

    for n_rows in range(SUB, TM + 1, SUB):
        @pl.when(used & (valid > n_rows - SUB) & (valid <= n_rows))
        def _():
            compute(n_rows)


def _experts(xs, tile_expert, tile_valid, n_used, layer, we_gate, we_up, we_down):
    p, d = xs.shape
    dfe = we_gate.shape[-1]
    assert dfe % FC == 0 and p % TM == 0 and TM % SUB == 0
    n_chunks = dfe // FC
    n_tiles = p // TM

    def tile(i, nu):
        return jnp.minimum(i, nu[0] - 1)

    def chunk(i, j, nu):
        return jnp.where(i < nu[0], j, n_chunks - 1)

    grid_spec = pltpu.PrefetchScalarGridSpec(
        num_scalar_prefetch=3,
        grid=(n_tiles, n_chunks),
        in_specs=[pl.BlockSpec((TM, d), lambda i, j, te, tv, nu: (tile(i, nu), 0)),
                  pl.BlockSpec((None, None, d, FC),
                               lambda i, j, te, tv, nu: (layer, te[tile(i, nu)], 0, chunk(i, j, nu))),
                  pl.BlockSpec((None, None, d, FC),
                               lambda i, j, te, tv, nu: (layer, te[tile(i, nu)], 0, chunk(i, j, nu))),
                  pl.BlockSpec((None, None, FC, d),
                               lambda i, j, te, tv, nu: (layer, te[tile(i, nu)], chunk(i, j, nu), 0))],
        out_specs=pl.BlockSpec((TM, d), lambda i, j, te, tv, nu: (i, 0)),
        scratch_shapes=[pltpu.VMEM((TM, d), BF16)],
    )
    return pl.pallas_call(
        _experts_kernel,
        grid_spec=grid_spec,
        out_shape=jax.ShapeDtypeStruct((p, d), F32),
        compiler_params=_params(2),
        name="expert_swiglu",
    )(tile_expert, tile_valid, n_used, xs, we_gate, we_up, we_down)


def _combine_kernel(dest_ref, ys_ref, info_ref, h_ref, mod_ref, gf_ref, o_ref, ybuf, sem, *, final_norm):
    _row_copies(ys_ref, ybuf, dest_ref, sem, scatter=False)
    info = info_ref[...]
    d = h_ref.shape[1]
    mix = info[:, 4:5] * ybuf[0].reshape(TB, d) + info[:, 5:6] * ybuf[1].reshape(TB, d)
    h = h_ref[...] + mod_ref[...][5:6] * mix
    if final_norm:
        h = (h * lax.rsqrt(jnp.mean(h * h, axis=-1, keepdims=True) + EPS)) * gf_ref[...]
    o_ref[...] = h


def _combine(ys, dest, info, h, mod, g_final, *, n_tiles, tpl, nb, final_norm):
    n, d = h.shape
    row = lambda i: (i, 0)
    return pl.pallas_call(
        functools.partial(_combine_kernel, final_norm=final_norm),
        grid=(n_tiles,),
        in_specs=[pl.BlockSpec((None, 1, TOP_K * TB), lambda i: (i, 0, 0), memory_space=pltpu.SMEM),
                  pl.BlockSpec(memory_space=pl.ANY),
                  pl.BlockSpec((TB, LANE), row),
                  pl.BlockSpec((TB, d), row),
                  pl.BlockSpec((None, N_MOD, d), lambda i: (jnp.minimum(i // tpl, nb), 0, 0)),
                  pl.BlockSpec((1, d), lambda i: (0, 0))],
        out_specs=pl.BlockSpec((TB, d), row),
        out_shape=jax.ShapeDtypeStruct((n_tiles * TB, d), F32),
        scratch_shapes=[pltpu.VMEM((TOP_K, TB // SUBLANE, SUBLANE, d), F32), pltpu.SemaphoreType.DMA(())],
        compiler_params=_params(1),
        name="moe_combine",
    )(dest, ys, info, h, mod, g_final.reshape(1, d))


def _moe(f, info, rt, counts, h, mod, g_final, layer, we_gate, we_up, we_down, *, n_tiles, tpl, nb, final_norm):
    n = f.shape[0]
    n_experts = we_gate.shape[1]
    n_xtiles = -(-TOP_K * n // TM) + n_experts
    n_rows = n_xtiles * TM

    cnt = counts[0, :n_experts].astype(I32)
    tiles_e = (cnt + TM - 1) // TM
    tile_end = jnp.cumsum(tiles_e)
    starts = (tile_end - tiles_e) * TM
    n_used = tile_end[-1:]
    tile_ids = jnp.arange(n_xtiles, dtype=I32)
    tile_expert = jnp.minimum(jnp.sum(tile_ids[:, None] >= tile_end[None, :], axis=1), n_experts - 1).astype(I32)
    tile_valid = jnp.clip(cnt[tile_expert] - (tile_ids * TM - starts[tile_expert]), 0, TM).astype(I32)
    choice = rt[:, 0:TOP_K, :]
    rank = rt[:, TOP_K:2 * TOP_K, :]
    start_of = sum(jnp.where(choice == e, starts[e], 0) for e in range(n_experts))
    dest = (start_of + rank).reshape(n_tiles, 1, TOP_K * TB)
    tail = n_used + jnp.arange(n_experts, dtype=I32)
    pad_at = jnp.concatenate([jnp.where(tiles_e > 0, tile_end - 1, -1),
                              jnp.where(tail < n_xtiles, tail, -1)]).astype(I32)

    xs = _dispatch(f, dest, pad_at, n_tiles=n_tiles, n_rows=n_rows)
    ys = _experts(xs, tile_expert, tile_valid, n_used, layer, we_gate, we_up, we_down)
    return _combine(ys, dest, info, h, mod, g_final, n_tiles=n_tiles, tpl=tpl, nb=nb, final_norm=final_norm)


def kernel(x, c, ctx, c_ctx, w_mod, b_mod, g_mix, g_ffn, g_final, w_in_ab, conv_w, pool_w, pool_scale,
           w_out_ab, w_ff_gate, w_ff_up, w_ff_down, w_qkv, rpb, w_out_na, w_router, we_gate, we_up, we_down):
    nb, seq_l, d = x.shape
    seq_c = ctx.shape[1]
    depth = w_mod.shape[0]
    heads = rpb.shape[1]
    na_rows = (rpb.shape[2] + 1) // 2
    na_cols = (rpb.shape[3] + 1) // 2
    assert seq_l % TB == 0 and seq_c % TB == 0 and nb + 1 <= SUBLANE and depth % 2 == 0
    tpl = seq_l // TB
    nlat = nb * tpl
    nall = nlat + nb * seq_c // TB

    cond = jnp.zeros((SUBLANE, d), F32).at[:nb].set(c).at[nb].set(c_ctx)
    mods = _adaln(cond, w_mod, b_mod).reshape(depth, SUBLANE, N_MOD, d)

    h = jnp.concatenate([x.reshape(nb * seq_l, d), ctx.reshape(nb * seq_c, d)], axis=0)
    bf = lambda a: a.astype(BF16)
    out = None
    for i in range(depth):
        j = i // 2
        last = i == depth - 1
        mod = mods[i]
        if i % 2 == 0:
            h = _mixer(h, mod, g_mix[i], bf(w_in_ab[j]), conv_w[j], bf(pool_w[j]), pool_scale[j], bf(w_out_ab[j]),
                       n_tiles=nall, nlat=nlat, nb=nb, seq_l=seq_l, seq_c=seq_c)
            h = _ffn(h, mod, g_ffn[i], bf(w_ff_gate[j]), bf(w_ff_up[j]), bf(w_ff_down[j]),
                     n_tiles=nall, tpl=tpl, nb=nb)
        else:
            n_tiles = nlat if last else nall
            q, k, v = _qkv(h, mod, g_mix[i], bf(w_qkv[j]), n_tiles=nall, tpl=tpl, nb=nb,
                           qscale=(d // heads) ** -0.5)
            o_lat, o_ctx = _natten(q, k, v, _bias_table(rpb[j], na_cols), nb=nb, seq_l=seq_l, seq_c=seq_c,
                                   heads=heads, na_rows=na_rows)
            h, f, info, rt, counts = _route(o_lat, o_ctx, h, mod, g_ffn[i], bf(w_out_na[j]), w_router[j],
                                            n_tiles=n_tiles, nlat=nlat, tpl=tpl, nb=nb)
            h = _moe(f, info, rt, counts, h, mod, g_final, j, we_gate, we_up, we_down,
                     n_tiles=n_tiles, tpl=tpl, nb=nb, final_norm=last)
            if last:
                out = h
    return out.reshape(nb, seq_l, d)
```

```python
import functools

import numpy as np
import jax
import jax.numpy as jnp
from jax import lax
from jax.experimental import pallas as pl
from jax.experimental.pallas import tpu as pltpu

F32 = jnp.float32
BF16 = jnp.bfloat16
I32 = jnp.int32

GRID_W = 64
POOL_WINDOWS = (2, 4, 8, 16)
N_MOD = 6
TOP_K = 2
EPS = 1e-6
NEG = -1e30

LANE = 128
SUBLANE = 8
TB = 256
HALO = SUBLANE
TM = 1024
SUB = 256
FC = 512
NA_GROUP = 4
VMEM_LIMIT = 56 * 1024 * 1024


def _params(n_axes):
    return pltpu.CompilerParams(dimension_semantics=("arbitrary",) * n_axes,
                                vmem_limit_bytes=VMEM_LIMIT)


def _rms_mod(x, g, shift, scale):
    y = x * lax.rsqrt(jnp.mean(x * x, axis=-1, keepdims=True) + EPS)
    return (y * g) * (1.0 + scale) + shift


def _silu(x):
    return x * jax.nn.sigmoid(x)


def _dot(a, b):
    return jnp.dot(a, b, preferred_element_type=F32)


def _dot_nt(a, b):
    return lax.dot_general(a, b, (((1,), (1,)), ((), ())), preferred_element_type=F32)


def _adaln_kernel(cond_ref, w_ref, b_ref, o_ref):
    s = _silu(cond_ref[...]).astype(BF16)
    o_ref[...] = _dot(s, w_ref[...].astype(BF16)) + b_ref[...]


def _adaln(cond, w_mod, b_mod):
    depth, d, nd = w_mod.shape
    tn = 1536
    assert nd % tn == 0
    return pl.pallas_call(
        _adaln_kernel,
        grid=(depth, nd // tn),
        in_specs=[pl.BlockSpec((SUBLANE, d), lambda l, n: (0, 0)),
                  pl.BlockSpec((None, d, tn), lambda l, n: (l, 0, n)),
                  pl.BlockSpec((None, 1, tn), lambda l, n: (l, 0, n))],
        out_specs=pl.BlockSpec((None, SUBLANE, tn), lambda l, n: (l, 0, n)),
        out_shape=jax.ShapeDtypeStruct((depth, SUBLANE, nd), F32),
        compiler_params=_params(2),
        name="adaln",
    )(cond, w_mod, b_mod.reshape(depth, 1, nd))


def _mixer_kernel(hp_ref, hc_ref, hn_ref, mod_ref, g_ref, win_ref, cw_ref, pw_ref, ps_ref, wout_ref,
                  o_ref, xs_ref, z_ref, p_ref, *, nlat, tpl, tpc, seq_l, seq_c, d_conv, pool_group):
    i = pl.program_id(0)
    is_ctx = i >= nlat
    pos = jnp.where(is_ctx, lax.rem(i - nlat, tpc), lax.rem(i, tpl)) * TB
    seq_len = jnp.where(is_ctx, seq_c, seq_l)
    ext = TB + 2 * HALO

    xs_ref[0:HALO, :] = hp_ref[...]
    xs_ref[HALO:HALO + TB, :] = hc_ref[...]
    xs_ref[HALO + TB:ext, :] = hn_ref[...]
    mod = mod_ref[...]
    a = _rms_mod(xs_ref[...], g_ref[...], mod[0:1], mod[1:2]).astype(BF16)
    u = _dot(a, win_ref[...])
    srow = lax.broadcasted_iota(I32, (ext, 1), 0) + (pos - HALO)
    u = jnp.where((srow >= 0) & (srow < seq_len), u, 0.0)

    z_ref[...] = u[:, d_conv:2 * d_conv] * u[:, 2 * d_conv:3 * d_conv]
    p_ref[...] = u[:, 3 * d_conv:]
    cw = cw_ref[...]
    conv = (z_ref[HALO - 1:HALO - 1 + TB, :] * cw[0:1] + z_ref[HALO:HALO + TB, :] * cw[1:2]
            + z_ref[HALO + 1:HALO + 1 + TB, :] * cw[2:3])
    pieces = [u[HALO:HALO + TB, 0:d_conv] * conv]

    spos = srow[HALO:HALO + TB]
    ps = ps_ref[...]
    for g, win in enumerate(POOL_WINDOWS):
        lo, hi = win // 2, win - 1 - win // 2
        cols = slice(g * pool_group, (g + 1) * pool_group)
        acc = p_ref[HALO - lo:HALO - lo + TB, cols]
        for dlt in range(-lo + 1, hi + 1):
            acc = acc + p_ref[HALO + dlt:HALO + dlt + TB, cols]
        cnt = jnp.minimum(spos + hi, seq_len - 1) - jnp.maximum(spos - lo, 0) + 1
        diff = acc / cnt.astype(F32) - p_ref[HALO:HALO + TB, cols]
        pieces.append(_dot(diff.astype(BF16), pw_ref[g]) * ps[:, cols])
    cat = jnp.concatenate(pieces, axis=-1).astype(BF16)
    o_ref[...] = hc_ref[...] + mod[2:3] * _dot(cat, wout_ref[...])


def _mixer(h, mod, g, w_in, conv_w, pool_w, pool_scale, w_out, *, n_tiles, nlat, nb, seq_l, seq_c):
    t, d = h.shape
    d_conv = conv_w.shape[-1]
    d_pool = pool_scale.shape[-1]
    pool_group = pool_w.shape[-1]
    assert pool_group % LANE == 0 and d_conv % LANE == 0 and len(POOL_WINDOWS) == pool_w.shape[0]
    tpl, tpc = seq_l // TB, seq_c // TB
    ext = TB + 2 * HALO
    hb = TB // HALO
    last_hb = t // HALO - 1
    kern = functools.partial(_mixer_kernel, nlat=nlat, tpl=tpl, tpc=tpc, seq_l=seq_l, seq_c=seq_c,
                             d_conv=d_conv, pool_group=pool_group)
    const = lambda i: (0, 0)
    return pl.pallas_call(
        kern,
        grid=(n_tiles,),
        in_specs=[pl.BlockSpec((HALO, d), lambda i: (jnp.maximum(i * hb - 1, 0), 0)),
                  pl.BlockSpec((TB, d), lambda i: (i, 0)),
                  pl.BlockSpec((HALO, d), lambda i: (jnp.minimum((i + 1) * hb, last_hb), 0)),
                  pl.BlockSpec((None, N_MOD, d), lambda i: (jnp.minimum(i // tpl, nb), 0, 0)),
                  pl.BlockSpec((1, d), const),
                  pl.BlockSpec(w_in.shape, const),
                  pl.BlockSpec(conv_w.shape, const),
                  pl.BlockSpec(pool_w.shape, lambda i: (0, 0, 0)),
                  pl.BlockSpec((1, d_pool), const),
                  pl.BlockSpec(w_out.shape, const)],
        out_specs=pl.BlockSpec((TB, d), lambda i: (i, 0)),
        out_shape=jax.ShapeDtypeStruct((n_tiles * TB, d), F32),
        scratch_shapes=[pltpu.VMEM((ext, d), F32), pltpu.VMEM((ext, d_conv), F32),
                        pltpu.VMEM((ext, d_pool), F32)],
        compiler_params=_params(1),
        name="conv_pool_mixer",
    )(h, h, h, mod, g.reshape(1, d), w_in, conv_w, pool_w, pool_scale.reshape(1, d_pool), w_out)


def _ffn_kernel(h_ref, mod_ref, g_ref, wg_ref, wu_ref, wd_ref, o_ref, act_ref, *, chunks):
    mod = mod_ref[...]
    h = h_ref[...]
    f = _rms_mod(h, g_ref[...], mod[3:4], mod[4:5]).astype(BF16)
    for c0, c1 in chunks:
        gate = _dot(f, wg_ref[:, c0:c1])
        up = _dot(f, wu_ref[:, c0:c1])
        act_ref[:, c0:c1] = (_silu(gate) * up).astype(BF16)
    o_ref[...] = h + mod[5:6] * _dot(act_ref[...], wd_ref[...])


def _ffn(h, mod, g, wg, wu, wd, *, n_tiles, tpl, nb):
    t, d = h.shape
    dff = wg.shape[-1]
    step = 768
    chunks = tuple((c, min(c + step, dff)) for c in range(0, dff, step))
    const = lambda i: (0, 0)
    return pl.pallas_call(
        functools.partial(_ffn_kernel, chunks=chunks),
        grid=(n_tiles,),
        in_specs=[pl.BlockSpec((TB, d), lambda i: (i, 0)),
                  pl.BlockSpec((None, N_MOD, d), lambda i: (jnp.minimum(i // tpl, nb), 0, 0)),
                  pl.BlockSpec((1, d), const),
                  pl.BlockSpec(wg.shape, const),
                  pl.BlockSpec(wu.shape, const),
                  pl.BlockSpec(wd.shape, const)],
        out_specs=pl.BlockSpec((TB, d), lambda i: (i, 0)),
        out_shape=jax.ShapeDtypeStruct((n_tiles * TB, d), F32),
        scratch_shapes=[pltpu.VMEM((TB, dff), BF16)],
        compiler_params=_params(1),
        name="dense_swiglu",
    )(h, mod, g.reshape(1, d), wg, wu, wd)


def _qkv_kernel(h_ref, mod_ref, g_ref, w_ref, q_ref, k_ref, v_ref, *, d, qscale):
    mod = mod_ref[...]
    a = _rms_mod(h_ref[...], g_ref[...], mod[0:1], mod[1:2]).astype(BF16)
    q_ref[...] = (_dot(a, w_ref[:, 0:d]) * qscale).astype(BF16)
    k_ref[...] = _dot(a, w_ref[:, d:2 * d]).astype(BF16)
    v_ref[...] = _dot(a, w_ref[:, 2 * d:3 * d]).astype(BF16)


def _qkv(h, mod, g, w, *, n_tiles, tpl, nb, qscale):
    t, d = h.shape
    const = lambda i: (0, 0)
    out = jax.ShapeDtypeStruct((n_tiles * TB, d), BF16)
    return pl.pallas_call(
        functools.partial(_qkv_kernel, d=d, qscale=qscale),
        grid=(n_tiles,),
        in_specs=[pl.BlockSpec((TB, d), lambda i: (i, 0)),
                  pl.BlockSpec((None, N_MOD, d), lambda i: (jnp.minimum(i // tpl, nb), 0, 0)),
                  pl.BlockSpec((1, d), const),
                  pl.BlockSpec(w.shape, const)],
        out_specs=[pl.BlockSpec((TB, d), lambda i: (i, 0))] * 3,
        out_shape=[out, out, out],
        compiler_params=_params(1),
        name="qkv_proj",
    )(h, mod, g.reshape(1, d), w)


def _softmax_pv(s_parts, v_parts):
    m = s_parts[0].max(axis=-1, keepdims=True)
    for s in s_parts[1:]:
        m = jnp.maximum(m, s.max(axis=-1, keepdims=True))
    den = 0.0
    out = 0.0
    for s, v in zip(s_parts, v_parts):
        p = jnp.exp(s - m)
        den = den + p.sum(axis=-1, keepdims=True)
        out = out + _dot(p.astype(BF16), v)
    return out / den


def _natten_kernel(q_ref, k_ref, v_ref, kc_ref, vc_ref, qc_ref, bias_ref, o_ref, oc_ref,
                   s_scr, p_scr, den_scr, *, rows, na_rows, head_dim):
    w = GRID_W
    band = na_rows * w
    lane = lax.broadcasted_iota(I32, (1, LANE), 1)
    head0 = lane < head_dim
    zero = jnp.zeros((), BF16)
    kc = kc_ref[...]
    vc = vc_ref[...]

    def stack(q):
        return jnp.concatenate([jnp.where(head0, q, zero), jnp.where(head0, zero, q)], axis=0)

    def unstack(o, n):
        return jnp.where(head0, o[0:n], o[n:2 * n])

    def offsets(r):
        start = jnp.clip(r - na_rows // 2, 0, rows - na_rows)
        return pl.multiple_of(r * w, w), pl.multiple_of(start * w, w), start - r + (na_rows - 1)

    def scores(r, slot):
        q0, k0, d0 = offsets(r)
        qq = stack(q_ref[pl.ds(q0, w), :])
        bias = jnp.concatenate([bias_ref[d0 + 2 * p] for p in range(band // LANE)], axis=-1)
        s_scr[slot, :, 0:band] = _dot_nt(qq, k_ref[pl.ds(k0, band), :]) + bias
        s_scr[slot, :, band:] = _dot_nt(qq, kc)

    def softmax(slot):
        s = s_scr[slot]
        p = jnp.exp(s - s.max(axis=-1, keepdims=True))
        den_scr[slot] = p.sum(axis=-1, keepdims=True)
        p_scr[slot] = p.astype(BF16)

    def values(r, slot):
        q0, k0, _ = offsets(r)
        o = _dot(p_scr[slot, :, 0:band], v_ref[pl.ds(k0, band), :]) + _dot(p_scr[slot, :, band:], vc)
        o_ref[pl.ds(q0, w), :] = unstack(o / den_scr[slot], w).astype(o_ref.dtype)

    n_groups = rows // NA_GROUP
    assert rows % NA_GROUP == 0 and n_groups % 2 == 0 and n_groups >= 4

    def step(u, parity, do_scores, do_softmax, do_values):
        for g in range(NA_GROUP):
            if do_softmax:
                softmax((1 - parity) * NA_GROUP + g)
        for g in range(NA_GROUP):
            if do_scores:
                scores(u * NA_GROUP + g, parity * NA_GROUP + g)
            if do_values:
                values((u - 2) * NA_GROUP + g, parity * NA_GROUP + g)

    step(0, 0, True, False, False)
    step(1, 1, True, True, False)

    def group_pair_body(u2, carry):
        step(2 * u2, 0, True, True, True)
        step(2 * u2 + 1, 1, True, True, True)
        return carry

    lax.fori_loop(1, n_groups // 2, group_pair_body, 0)
    step(n_groups, 0, False, True, True)
    step(n_groups + 1, 1, False, False, True)

    qc = qc_ref[...]
    nc = qc.shape[0]
    oc = _softmax_pv([_dot_nt(stack(qc), kc)], [vc])
    oc_ref[...] = unstack(oc, nc).astype(oc_ref.dtype)


def _bias_table(rpb, na_cols):
    heads, nr, nc = rpb.shape
    w = GRID_W
    jcol = np.arange(w)
    cstart = np.clip(jcol - na_cols // 2, 0, w - na_cols)
    kcol = np.arange(w)
    inside = (kcol[None, :] >= cstart[:, None]) & (kcol[None, :] < cstart[:, None] + na_cols)
    dc = kcol[None, :] - jcol[:, None] + (na_cols - 1)
    pick = jnp.asarray((dc[None] == np.arange(nc)[:, None, None]) & inside[None], F32)
    t2 = jnp.einsum("hdm,mqk->hdqk", rpb, pick, precision=lax.Precision.HIGHEST)
    t2 = jnp.where(inside[None, None], t2, NEG)
    t3 = jnp.concatenate([t2[:, :-1], t2[:, 1:]], axis=-1)
    t3 = t3.reshape(heads // 2, 2, nr - 1, w, 2 * w).transpose(0, 2, 1, 3, 4)
    return t3.reshape(heads // 2, nr - 1, 2 * w, 2 * w).astype(F32)


def _natten(q, k, v, bias, *, nb, seq_l, seq_c, heads, na_rows):
    t, d = q.shape
    head_dim = d // heads
    assert 2 * head_dim == LANE and 2 * GRID_W == LANE and seq_l % GRID_W == 0
    rows = seq_l // GRID_W
    assert (nb * seq_l) % seq_c == 0
    cblk = nb * seq_l // seq_c
    lat = lambda b, hp: (b, hp)
    ctx = lambda b, hp: (cblk + b, hp)
    return pl.pallas_call(
        functools.partial(_natten_kernel, rows=rows, na_rows=na_rows, head_dim=head_dim),
        grid=(nb, heads // 2),
        in_specs=[pl.BlockSpec((seq_l, LANE), lat),
                  pl.BlockSpec((seq_l, LANE), lat),
                  pl.BlockSpec((seq_l, LANE), lat),
                  pl.BlockSpec((seq_c, LANE), ctx),
                  pl.BlockSpec((seq_c, LANE), ctx),
                  pl.BlockSpec((seq_c, LANE), ctx),
                  pl.BlockSpec((None,) + bias.shape[1:], lambda b, hp: (hp, 0, 0, 0))],
        out_specs=[pl.BlockSpec((seq_l, LANE), lat),
                   pl.BlockSpec((seq_c, LANE), lambda b, hp: (b, hp))],
        out_shape=[jax.ShapeDtypeStruct((nb * seq_l, d), BF16),
                   jax.ShapeDtypeStruct((nb * seq_c, d), BF16)],
        scratch_shapes=[pltpu.VMEM((2 * NA_GROUP, 2 * GRID_W, na_rows * GRID_W + seq_c), F32),
                        pltpu.VMEM((2 * NA_GROUP, 2 * GRID_W, na_rows * GRID_W + seq_c), BF16),
                        pltpu.VMEM((2 * NA_GROUP, 2 * GRID_W, 1), F32)],
        compiler_params=_params(2),
        name="neighbourhood_attention",
    )(q, k, v, k, v, q, bias)


def _split_bf16(x):
    hi = x.astype(BF16)
    return hi, (x - hi.astype(F32)).astype(BF16)


def _route_kernel(al_ref, ac_ref, h_ref, mod_ref, g_ref, wo_ref, wr_ref, h_out_ref, f_ref, info_ref, rt_ref,
                  cnt_ref, *, n_experts, nlat):
    i = pl.program_id(0)

    @pl.when(i == 0)
    def _():
        cnt_ref[...] = jnp.zeros_like(cnt_ref)

    mod = mod_ref[...]
    attn = jnp.where(i < nlat, al_ref[...], ac_ref[...])
    h = h_ref[...] + mod[2:3] * _dot(attn, wo_ref[...])
    h_out_ref[...] = h
    f = _rms_mod(h, g_ref[...], mod[3:4], mod[4:5])
    f_ref[...] = f

    f_hi, f_lo = _split_bf16(f)
    w_hi, w_lo = _split_bf16(wr_ref[...])
    logits = _dot(f_hi, w_hi) + (_dot(f_lo, w_hi) + _dot(f_hi, w_lo))
    lane_i = lax.broadcasted_iota(I32, logits.shape, 1)
    lane = lane_i.astype(F32)
    logits = jnp.where(lane_i < n_experts, logits, -jnp.inf)
    v1 = logits.max(axis=-1, keepdims=True)
    i1 = jnp.where(logits == v1, lane, float(LANE)).min(axis=-1, keepdims=True)
    rest = jnp.where(lane == i1, -jnp.inf, logits)
    v2 = rest.max(axis=-1, keepdims=True)
    i2 = jnp.where(rest == v2, lane, float(LANE)).min(axis=-1, keepdims=True)
    e2 = jnp.exp(v2 - v1)
    g1 = 1.0 / (1.0 + e2)
    g2 = e2 * g1

    sel1 = lane == i1
    sel2 = lane == i2
    onehot = jnp.where(sel1 | sel2, 1.0, 0.0)
    tr = lax.broadcasted_iota(I32, (TB, TB), 0)
    tc = lax.broadcasted_iota(I32, (TB, TB), 1)
    before = _dot(jnp.where(tc < tr, 1.0, 0.0).astype(BF16), onehot.astype(BF16)) + cnt_ref[0:1, :]
    r1 = jnp.where(sel1, before, 0.0).sum(axis=-1, keepdims=True)
    r2 = jnp.where(sel2, before, 0.0).sum(axis=-1, keepdims=True)
    cnt_ref[...] = cnt_ref[...] + onehot.sum(axis=0, keepdims=True)

    info = jnp.where(lane_i == 0, i1, 0.0)
    info = jnp.where(lane_i == 1, i2, info)
    info = jnp.where(lane_i == 2, r1, info)
    info = jnp.where(lane_i == 3, r2, info)
    info = jnp.where(lane_i == 4, g1, info)
    info = jnp.where(lane_i == 5, g2, info)
    info_ref[...] = info
    rt_ref[...] = jnp.transpose(info)[0:SUBLANE, :].astype(I32)


def _route(a_lat, a_ctx, h, mod, g, w_out, w_router, *, n_tiles, nlat, tpl, nb):
    t, d = h.shape
    assert d == SUBLANE * LANE
    n_experts = w_router.shape[-1]
    wr = jnp.zeros((d, LANE), F32).at[:, :n_experts].set(w_router)
    const = lambda i: (0, 0)
    row = lambda i: (i, 0)
    n = n_tiles * TB
    return pl.pallas_call(
        functools.partial(_route_kernel, n_experts=n_experts, nlat=nlat),
        grid=(n_tiles,),
        in_specs=[pl.BlockSpec((TB, d), lambda i: (jnp.minimum(i, nlat - 1), 0)),
                  pl.BlockSpec((TB, d), lambda i: (jnp.maximum(i - nlat, 0), 0)),
                  pl.BlockSpec((TB, d), row),
                  pl.BlockSpec((None, N_MOD, d), lambda i: (jnp.minimum(i // tpl, nb), 0, 0)),
                  pl.BlockSpec((1, d), const),
                  pl.BlockSpec(w_out.shape, const),
                  pl.BlockSpec(wr.shape, const)],
        out_specs=[pl.BlockSpec((TB, d), row),
                   pl.BlockSpec((TB, d), row),
                   pl.BlockSpec((TB, LANE), row),
                   pl.BlockSpec((None, SUBLANE, TB), lambda i: (i, 0, 0)),
                   pl.BlockSpec((SUBLANE, LANE), const)],
        out_shape=[jax.ShapeDtypeStruct((n, d), F32),
                   jax.ShapeDtypeStruct((n, d), F32),
                   jax.ShapeDtypeStruct((n, LANE), F32),
                   jax.ShapeDtypeStruct((n_tiles, SUBLANE, TB), I32),
                   jax.ShapeDtypeStruct((SUBLANE, LANE), F32)],
        compiler_params=_params(1),
        name="attn_out_router",
    )(a_lat, a_ctx, h, mod, g.reshape(1, d), w_out, wr)


def _row_copies(src_ref, dst_ref, idx_ref, sem, scatter):
    def issue(jj, wait):
        for u in range(SUBLANE):
            for k in range(TOP_K):
                row = idx_ref[0, k * TB + jj * SUBLANE + u]
                if scatter:
                    cp = pltpu.make_async_copy(src_ref.at[jj, pl.ds(u, 1)], dst_ref.at[pl.ds(row, 1)], sem)
                else:
                    cp = pltpu.make_async_copy(src_ref.at[pl.ds(row, 1)], dst_ref.at[k, jj, pl.ds(u, 1)], sem)
                if wait:
                    cp.wait()
                else:
                    cp.start(priority=k % 2)

    def start_body(jj, c):
        issue(jj, False)
        return c

    def wait_body(jj, c):
        issue(jj, True)
        return c

    lax.fori_loop(0, TB // SUBLANE, start_body, 0)
    lax.fori_loop(0, TB // SUBLANE, wait_body, 0)


def _dispatch_kernel(pad_ref, dest_ref, f_ref, xs_ref, zbuf, sem, zsem, *, n_fills):
    @pl.when(pl.program_id(0) == 0)
    def _():
        zbuf[...] = jnp.zeros_like(zbuf)

        def fill(e):
            return pltpu.make_async_copy(zbuf, xs_ref.at[pl.ds(pl.multiple_of(pad_ref[e] * TM, TM), TM)], zsem)

        for e in range(n_fills):
            @pl.when(pad_ref[e] >= 0)
            def _():
                fill(e).start()
        for e in range(n_fills):
            @pl.when(pad_ref[e] >= 0)
            def _():
                fill(e).wait()

    _row_copies(f_ref, xs_ref, dest_ref, sem, scatter=True)


def _dispatch(f, dest, pad_at, *, n_tiles, n_rows):
    n_fills = pad_at.shape[0]
    n, d = f.shape
    grid_spec = pltpu.PrefetchScalarGridSpec(
        num_scalar_prefetch=1,
        grid=(n_tiles,),
        in_specs=[pl.BlockSpec((None, 1, TOP_K * TB), lambda i, pad: (i, 0, 0), memory_space=pltpu.SMEM),
                  pl.BlockSpec((TB // SUBLANE, SUBLANE, d), lambda i, pad: (i, 0, 0))],
        out_specs=pl.BlockSpec(memory_space=pl.ANY),
        scratch_shapes=[pltpu.VMEM((TM, d), F32), pltpu.SemaphoreType.DMA(()), pltpu.SemaphoreType.DMA(())],
    )
    return pl.pallas_call(
        functools.partial(_dispatch_kernel, n_fills=n_fills),
        grid_spec=grid_spec,
        out_shape=jax.ShapeDtypeStruct((n_rows, d), F32),
        compiler_params=_params(1),
        name="moe_dispatch",
    )(pad_at, dest, f.reshape(n // SUBLANE, SUBLANE, d))


def _experts_kernel(te_ref, tv_ref, nu_ref, x_ref, wg_ref, wu_ref, wd_ref, o_ref, xb_ref):
    i = pl.program_id(0)
    j = pl.program_id(1)
    used = i < nu_ref[0]
    valid = tv_ref[i]

    @pl.when(j == 0)
    def _():
        o_ref[...] = jnp.zeros_like(o_ref)

    def compute(n_rows):
        @pl.when(j == 0)
        def _():
            xb_ref[0:n_rows, :] = x_ref[0:n_rows, :].astype(BF16)

        xb = xb_ref[0:n_rows, :]
        act = _silu(_dot(xb, wg_ref[...].astype(BF16))) * _dot(xb, wu_ref[...].astype(BF16))
        o_ref[0:n_rows, :] += _dot(act.astype(BF16), wd_ref[...].astype(BF16))

    for n_rows in range(SUB, TM + 1, SUB):
        @pl.when(used & (valid > n_rows - SUB) & (valid <= n_rows))
        def _():
            compute(n_rows)


def _experts(xs, tile_expert, tile_valid, n_used, layer, we_gate, we_up, we_down):
    p, d = xs.shape
    dfe = we_gate.shape[-1]
    assert dfe % FC == 0 and p % TM == 0 and TM % SUB == 0
    n_chunks = dfe // FC
    n_tiles = p // TM

    def tile(i, nu):
        return jnp.minimum(i, nu[0] - 1)

    def chunk(i, j, nu):
        return jnp.where(i < nu[0], j, n_chunks - 1)

    grid_spec = pltpu.PrefetchScalarGridSpec(
        num_scalar_prefetch=3,
        grid=(n_tiles, n_chunks),
        in_specs=[pl.BlockSpec((TM, d), lambda i, j, te, tv, nu: (tile(i, nu), 0)),
                  pl.BlockSpec((None, None, d, FC),
                               lambda i, j, te, tv, nu: (layer, te[tile(i, nu)], 0, chunk(i, j, nu))),
                  pl.BlockSpec((None, None, d, FC),
                               lambda i, j, te, tv, nu: (layer, te[tile(i, nu)], 0, chunk(i, j, nu))),
                  pl.BlockSpec((None, None, FC, d),
                               lambda i, j, te, tv, nu: (layer, te[tile(i, nu)], chunk(i, j, nu), 0))],
        out_specs=pl.BlockSpec((TM, d), lambda i, j, te, tv, nu: (i, 0)),
        scratch_shapes=[pltpu.VMEM((TM, d), BF16)],
    )
    return pl.pallas_call(
        _experts_kernel,
        grid_spec=grid_spec,
        out_shape=jax.ShapeDtypeStruct((p, d), F32),
        compiler_params=_params(2),
        name="expert_swiglu",
    )(tile_expert, tile_valid, n_used, xs, we_gate, we_up, we_down)


def _combine_kernel(dest_ref, ys_ref, info_ref, h_ref, mod_ref, gf_ref, o_ref, ybuf, sem, *, final_norm):
    _row_copies(ys_ref, ybuf, dest_ref, sem, scatter=False)
    info = info_ref[...]
    d = h_ref.shape[1]
    mix = info[:, 4:5] * ybuf[0].reshape(TB, d) + info[:, 5:6] * ybuf[1].reshape(TB, d)
    h = h_ref[...] + mod_ref[...][5:6] * mix
    if final_norm:
        h = (h * lax.rsqrt(jnp.mean(h * h, axis=-1, keepdims=True) + EPS)) * gf_ref[...]
    o_ref[...] = h


def _combine(ys, dest, info, h, mod, g_final, *, n_tiles, tpl, nb, final_norm):
    n, d = h.shape
    row = lambda i: (i, 0)
    return pl.pallas_call(
        functools.partial(_combine_kernel, final_norm=final_norm),
        grid=(n_tiles,),
        in_specs=[pl.BlockSpec((None, 1, TOP_K * TB), lambda i: (i, 0, 0), memory_space=pltpu.SMEM),
                  pl.BlockSpec(memory_space=pl.ANY),
                  pl.BlockSpec((TB, LANE), row),
                  pl.BlockSpec((TB, d), row),
                  pl.BlockSpec((None, N_MOD, d), lambda i: (jnp.minimum(i // tpl, nb), 0, 0)),
                  pl.BlockSpec((1, d), lambda i: (0, 0))],
        out_specs=pl.BlockSpec((TB, d), row),
        out_shape=jax.ShapeDtypeStruct((n_tiles * TB, d), F32),
        scratch_shapes=[pltpu.VMEM((TOP_K, TB // SUBLANE, SUBLANE, d), F32), pltpu.SemaphoreType.DMA(())],
        compiler_params=_params(1),
        name="moe_combine",
    )(dest, ys, info, h, mod, g_final.reshape(1, d))


def _moe(f, info, rt, counts, h, mod, g_final, layer, we_gate, we_up, we_down, *, n_tiles, tpl, nb, final_norm):
    n = f.shape[0]
    n_experts = we_gate.shape[1]
    n_xtiles = -(-TOP_K * n // TM) + n_experts
    n_rows = n_xtiles * TM

    cnt = counts[0, :n_experts].astype(I32)
    tiles_e = (cnt + TM - 1) // TM
    tile_end = jnp.cumsum(tiles_e)
    starts = (tile_end - tiles_e) * TM
    n_used = tile_end[-1:]
    tile_ids = jnp.arange(n_xtiles, dtype=I32)
    tile_expert = jnp.minimum(jnp.sum(tile_ids[:, None] >= tile_end[None, :], axis=1), n_experts - 1).astype(I32)
    tile_valid = jnp.clip(cnt[tile_expert] - (tile_ids * TM - starts[tile_expert]), 0, TM).astype(I32)
    choice = rt[:, 0:TOP_K, :]
    rank = rt[:, TOP_K:2 * TOP_K, :]
    start_of = sum(jnp.where(choice == e, starts[e], 0) for e in range(n_experts))
    dest = (start_of + rank).reshape(n_tiles, 1, TOP_K * TB)
    tail = n_used + jnp.arange(n_experts, dtype=I32)
    pad_at = jnp.concatenate([jnp.where(tiles_e > 0, tile_end - 1, -1),
                              jnp.where(tail < n_xtiles, tail, -1)]).astype(I32)

    xs = _dispatch(f, dest, pad_at, n_tiles=n_tiles, n_rows=n_rows)
    ys = _experts(xs, tile_expert, tile_valid, n_used, layer, we_gate, we_up, we_down)
    return _combine(ys, dest, info, h, mod, g_final, n_tiles=n_tiles, tpl=tpl, nb=nb, final_norm=final_norm)


def kernel(x, c, ctx, c_ctx, w_mod, b_mod, g_mix, g_ffn, g_final, w_in_ab, conv_w, pool_w, pool_scale,
           w_out_ab, w_ff_gate, w_ff_up, w_ff_down, w_qkv, rpb, w_out_na, w_router, we_gate, we_up, we_down):
    nb, seq_l, d = x.shape
    seq_c = ctx.shape[1]
    depth = w_mod.shape[0]
    heads = rpb.shape[1]
    na_rows = (rpb.shape[2] + 1) // 2
    na_cols = (rpb.shape[3] + 1) // 2
    assert seq_l % TB == 0 and seq_c % TB == 0 and nb + 1 <= SUBLANE and depth % 2 == 0
    tpl = seq_l // TB
    nlat = nb * tpl
    nall = nlat + nb * seq_c // TB

    cond = jnp.zeros((SUBLANE, d), F32).at[:nb].set(c).at[nb].set(c_ctx)
    mods = _adaln(cond, w_mod, b_mod).reshape(depth, SUBLANE, N_MOD, d)

    h = jnp.concatenate([x.reshape(nb * seq_l, d), ctx.reshape(nb * seq_c, d)], axis=0)
    bf = lambda a: a.astype(BF16)
    out = None
    for i in range(depth):
        j = i // 2
        last = i == depth - 1
        mod = mods[i]
        if i % 2 == 0:
            h = _mixer(h, mod, g_mix[i], bf(w_in_ab[j]), conv_w[j], bf(pool_w[j]), pool_scale[j], bf(w_out_ab[j]),
                       n_tiles=nall, nlat=nlat, nb=nb, seq_l=seq_l, seq_c=seq_c)
            h = _ffn(h, mod, g_ffn[i], bf(w_ff_gate[j]), bf(w_ff_up[j]), bf(w_ff_down[j]),
                     n_tiles=nall, tpl=tpl, nb=nb)
        else:
            n_tiles = nlat if last else nall
            q, k, v = _qkv(h, mod, g_mix[i], bf(w_qkv[j]), n_tiles=nall, tpl=tpl, nb=nb,
                           qscale=(d // heads) ** -0.5)
            o_lat, o_ctx = _natten(q, k, v, _bias_table(rpb[j], na_cols), nb=nb, seq_l=seq_l, seq_c=seq_c,
                                   heads=heads, na_rows=na_rows)
            h, f, info, rt, counts = _route(o_lat, o_ctx, h, mod, g_ffn[i], bf(w_out_na[j]), w_router[j],
                                            n_tiles=n_tiles, nlat=nlat, tpl=tpl, nb=nb)
            h = _moe(f, info, rt, counts, h, mod, g_final, j, we_gate, we_up, we_down,
                     n_tiles=n_tiles, tpl=tpl, nb=nb, final_norm=last)
            if last:
                out = h
    return out.reshape(nb, seq_l, d)
```

```python
import functools

import numpy as np
import jax
import jax.numpy as jnp
from jax import lax
from jax.experimental import pallas as pl
from jax.experimental.pallas import tpu as pltpu

F32 = jnp.float32
BF16 = jnp.bfloat16
I32 = jnp.int32

GRID_W = 64
POOL_WINDOWS = (2, 4, 8, 16)
N_MOD = 6
TOP_K = 2
EPS = 1e-6
NEG = -1e30

LANE = 128
SUBLANE = 8
TB = 256
TBD = 512
HALO = SUBLANE
TM = 1024
SUB = 256
FC = 512
NA_GROUP = 4
VMEM_LIMIT = 56 * 1024 * 1024


def _params(n_axes):
    return pltpu.CompilerParams(dimension_semantics=("arbitrary",) * n_axes,
                                vmem_limit_bytes=VMEM_LIMIT)


def _rms_mod(x, g, shift, scale):
    y = x * lax.rsqrt(jnp.mean(x * x, axis=-1, keepdims=True) + EPS)
    return (y * g) * (1.0 + scale) + shift


def _silu(x):
    return x * jax.nn.sigmoid(x)


def _dot(a, b):
    return jnp.dot(a, b, preferred_element_type=F32)


def _dot_nt(a, b):
    return lax.dot_general(a, b, (((1,), (1,)), ((), ())), preferred_element_type=F32)


def _adaln_kernel(cond_ref, w_ref, b_ref, o_ref):
    s = _silu(cond_ref[...]).astype(BF16)
    o_ref[...] = _dot(s, w_ref[...].astype(BF16)) + b_ref[...]


def _adaln(cond, w_mod, b_mod):
    depth, d, nd = w_mod.shape
    tn = 1536
    assert nd % tn == 0
    return pl.pallas_call(
        _adaln_kernel,
        grid=(depth, nd // tn),
        in_specs=[pl.BlockSpec((SUBLANE, d), lambda l, n: (0, 0)),
                  pl.BlockSpec((None, d, tn), lambda l, n: (l, 0, n)),
                  pl.BlockSpec((None, 1, tn), lambda l, n: (l, 0, n))],
        out_specs=pl.BlockSpec((None, SUBLANE, tn), lambda l, n: (l, 0, n)),
        out_shape=jax.ShapeDtypeStruct((depth, SUBLANE, nd), F32),
        compiler_params=_params(2),
        name="adaln",
    )(cond, w_mod, b_mod.reshape(depth, 1, nd))


def _mixer_kernel(hp_ref, hc_ref, hn_ref, mod_ref, g_ref, win_ref, cw_ref, pw_ref, ps_ref, wout_ref,
                  o_ref, xs_ref, z_ref, p_ref, *, nlat, tpl, tpc, seq_l, seq_c, d_conv, pool_group):
    i = pl.program_id(0)
    is_ctx = i >= nlat
    pos = jnp.where(is_ctx, lax.rem(i - nlat, tpc), lax.rem(i, tpl)) * TB
    seq_len = jnp.where(is_ctx, seq_c, seq_l)
    ext = TB + 2 * HALO

    xs_ref[0:HALO, :] = hp_ref[...]
    xs_ref[HALO:HALO + TB, :] = hc_ref[...]
    xs_ref[HALO + TB:ext, :] = hn_ref[...]
    mod = mod_ref[...]
    a = _rms_mod(xs_ref[...], g_ref[...], mod[0:1], mod[1:2]).astype(BF16)
    u = _dot(a, win_ref[...])
    srow = lax.broadcasted_iota(I32, (ext, 1), 0) + (pos - HALO)
    u = jnp.where((srow >= 0) & (srow < seq_len), u, 0.0)

    z_ref[...] = u[:, d_conv:2 * d_conv] * u[:, 2 * d_conv:3 * d_conv]
    p_ref[...] = u[:, 3 * d_conv:]
    cw = cw_ref[...]
    conv = (z_ref[HALO - 1:HALO - 1 + TB, :] * cw[0:1] + z_ref[HALO:HALO + TB, :] * cw[1:2]
            + z_ref[HALO + 1:HALO + 1 + TB, :] * cw[2:3])
    pieces = [u[HALO:HALO + TB, 0:d_conv] * conv]

    spos = srow[HALO:HALO + TB]
    ps = ps_ref[...]
    for g, win in enumerate(POOL_WINDOWS):
        lo, hi = win // 2, win - 1 - win // 2
        cols = slice(g * pool_group, (g + 1) * pool_group)
        acc = p_ref[HALO - lo:HALO - lo + TB, cols]
        for dlt in range(-lo + 1, hi + 1):
            acc = acc + p_ref[HALO + dlt:HALO + dlt + TB, cols]
        cnt = jnp.minimum(spos + hi, seq_len - 1) - jnp.maximum(spos - lo, 0) + 1
        diff = acc / cnt.astype(F32) - p_ref[HALO:HALO + TB, cols]
        pieces.append(_dot(diff.astype(BF16), pw_ref[g]) * ps[:, cols])
    cat = jnp.concatenate(pieces, axis=-1).astype(BF16)
    o_ref[...] = hc_ref[...] + mod[2:3] * _dot(cat, wout_ref[...])


def _mixer(h, mod, g, w_in, conv_w, pool_w, pool_scale, w_out, *, n_tiles, nlat, nb, seq_l, seq_c):
    t, d = h.shape
    d_conv = conv_w.shape[-1]
    d_pool = pool_scale.shape[-1]
    pool_group = pool_w.shape[-1]
    assert pool_group % LANE == 0 and d_conv % LANE == 0 and len(POOL_WINDOWS) == pool_w.shape[0]
    tpl, tpc = seq_l // TB, seq_c // TB
    ext = TB + 2 * HALO
    hb = TB // HALO
    last_hb = t // HALO - 1
    kern = functools.partial(_mixer_kernel, nlat=nlat, tpl=tpl, tpc=tpc, seq_l=seq_l, seq_c=seq_c,
                             d_conv=d_conv, pool_group=pool_group)
    const = lambda i: (0, 0)
    return pl.pallas_call(
        kern,
        grid=(n_tiles,),
        in_specs=[pl.BlockSpec((HALO, d), lambda i: (jnp.maximum(i * hb - 1, 0), 0)),
                  pl.BlockSpec((TB, d), lambda i: (i, 0)),
                  pl.BlockSpec((HALO, d), lambda i: (jnp.minimum((i + 1) * hb, last_hb), 0)),
                  pl.BlockSpec((None, N_MOD, d), lambda i: (jnp.minimum(i // tpl, nb), 0, 0)),
                  pl.BlockSpec((1, d), const),
                  pl.BlockSpec(w_in.shape, const),
                  pl.BlockSpec(conv_w.shape, const),
                  pl.BlockSpec(pool_w.shape, lambda i: (0, 0, 0)),
                  pl.BlockSpec((1, d_pool), const),
                  pl.BlockSpec(w_out.shape, const)],
        out_specs=pl.BlockSpec((TB, d), lambda i: (i, 0)),
        out_shape=jax.ShapeDtypeStruct((n_tiles * TB, d), F32),
        scratch_shapes=[pltpu.VMEM((ext, d), F32), pltpu.VMEM((ext, d_conv), F32),
                        pltpu.VMEM((ext, d_pool), F32)],
        compiler_params=_params(1),
        name="conv_pool_mixer",
    )(h, h, h, mod, g.reshape(1, d), w_in, conv_w, pool_w, pool_scale.reshape(1, d_pool), w_out)


def _ffn_kernel(h_ref, mod_ref, g_ref, wg_ref, wu_ref, wd_ref, o_ref, act_ref, *, chunks):
    mod = mod_ref[...]
    h = h_ref[...]
    f = _rms_mod(h, g_ref[...], mod[3:4], mod[4:5]).astype(BF16)
    for c0, c1 in chunks:
        gate = _dot(f, wg_ref[:, c0:c1])
        up = _dot(f, wu_ref[:, c0:c1])
        act_ref[:, c0:c1] = (_silu(gate) * up).astype(BF16)
    o_ref[...] = h + mod[5:6] * _dot(act_ref[...], wd_ref[...])


def _ffn(h, mod, g, wg, wu, wd, *, n_rows, seq_l, nb):
    t, d = h.shape
    dff = wg.shape[-1]
    step = 768
    chunks = tuple((c, min(c + step, dff)) for c in range(0, dff, step))
    const = lambda i: (0, 0)
    once = pl.Buffered(1)
    assert n_rows % TBD == 0 and seq_l % TBD == 0
    tpl = seq_l // TBD
    return pl.pallas_call(
        functools.partial(_ffn_kernel, chunks=chunks),
        grid=(n_rows // TBD,),
        in_specs=[pl.BlockSpec((TBD, d), lambda i: (i, 0)),
                  pl.BlockSpec((None, N_MOD, d), lambda i: (jnp.minimum(i // tpl, nb), 0, 0)),
                  pl.BlockSpec((1, d), const),
                  pl.BlockSpec(wg.shape, const, pipeline_mode=once),
                  pl.BlockSpec(wu.shape, const, pipeline_mode=once),
                  pl.BlockSpec(wd.shape, const, pipeline_mode=once)],
        out_specs=pl.BlockSpec((TBD, d), lambda i: (i, 0)),
        out_shape=jax.ShapeDtypeStruct((n_rows, d), F32),
        scratch_shapes=[pltpu.VMEM((TBD, dff), BF16)],
        compiler_params=_params(1),
        name="dense_swiglu",
    )(h, mod, g.reshape(1, d), wg, wu, wd)


def _qkv_kernel(h_ref, mod_ref, g_ref, w_ref, q_ref, k_ref, v_ref, *, d, qscale):
    mod = mod_ref[...]
    a = _rms_mod(h_ref[...], g_ref[...], mod[0:1], mod[1:2]).astype(BF16)
    q_ref[...] = (_dot(a, w_ref[:, 0:d]) * qscale).astype(BF16)
    k_ref[...] = _dot(a, w_ref[:, d:2 * d]).astype(BF16)
    v_ref[...] = _dot(a, w_ref[:, 2 * d:3 * d]).astype(BF16)


def _qkv(h, mod, g, w, *, n_rows, seq_l, nb, qscale):
    t, d = h.shape
    const = lambda i: (0, 0)
    out = jax.ShapeDtypeStruct((n_rows, d), BF16)
    assert n_rows % TBD == 0 and seq_l % TBD == 0
    tpl = seq_l // TBD
    return pl.pallas_call(
        functools.partial(_qkv_kernel, d=d, qscale=qscale),
        grid=(n_rows // TBD,),
        in_specs=[pl.BlockSpec((TBD, d), lambda i: (i, 0)),
                  pl.BlockSpec((None, N_MOD, d), lambda i: (jnp.minimum(i // tpl, nb), 0, 0)),
                  pl.BlockSpec((1, d), const),
                  pl.BlockSpec(w.shape, const)],
        out_specs=[pl.BlockSpec((TBD, d), lambda i: (i, 0))] * 3,
        out_shape=[out, out, out],
        compiler_params=_params(1),
        name="qkv_proj",
    )(h, mod, g.reshape(1, d), w)


def _softmax_pv(s_parts, v_parts):
    m = s_parts[0].max(axis=-1, keepdims=True)
    for s in s_parts[1:]:
        m = jnp.maximum(m, s.max(axis=-1, keepdims=True))
    den = 0.0
    out = 0.0
    for s, v in zip(s_parts, v_parts):
        p = jnp.exp(s - m)
        den = den + p.sum(axis=-1, keepdims=True)
        out = out + _dot(p.astype(BF16), v)
    return out / den


def _natten_kernel(q_ref, k_ref, v_ref, kc_ref, vc_ref, qc_ref, bias_ref, o_ref, oc_ref,
                   s_scr, p_scr, den_scr, *, rows, na_rows, head_dim):
    w = GRID_W
    band = na_rows * w
    lane = lax.broadcasted_iota(I32, (1, LANE), 1)
    head0 = lane < head_dim
    zero = jnp.zeros((), BF16)
    kc = kc_ref[...]
    vc = vc_ref[...]

    def stack(q):
        return jnp.concatenate([jnp.where(head0, q, zero), jnp.where(head0, zero, q)], axis=0)

    def unstack(o, n):
        return jnp.where(head0, o[0:n], o[n:2 * n])

    def offsets(r):
        start = jnp.clip(r - na_rows // 2, 0, rows - na_rows)
        return pl.multiple_of(r * w, w), pl.multiple_of(start * w, w), start - r + (na_rows - 1)

    def scores(r, slot):
        q0, k0, d0 = offsets(r)
        qq = stack(q_ref[pl.ds(q0, w), :])
        bias = jnp.concatenate([bias_ref[d0 + 2 * p] for p in range(band // LANE)], axis=-1)
        s_scr[slot, :, 0:band] = _dot_nt(qq, k_ref[pl.ds(k0, band), :]) + bias
        s_scr[slot, :, band:] = _dot_nt(qq, kc)

    def softmax(slot):
        s = s_scr[slot]
        p = jnp.exp(s - s.max(axis=-1, keepdims=True))
        den_scr[slot] = p.sum(axis=-1, keepdims=True)
        p_scr[slot] = p.astype(BF16)

    def values(r, slot):
        q0, k0, _ = offsets(r)
        o = _dot(p_scr[slot, :, 0:band], v_ref[pl.ds(k0, band), :]) + _dot(p_scr[slot, :, band:], vc)
        o_ref[pl.ds(q0, w), :] = unstack(o / den_scr[slot], w).astype(o_ref.dtype)

    n_groups = rows // NA_GROUP
    assert rows % NA_GROUP == 0 and n_groups % 2 == 0 and n_groups >= 4

    def step(u, parity, do_scores, do_softmax, do_values):
        for g in range(NA_GROUP):
            if do_softmax:
                softmax((1 - parity) * NA_GROUP + g)
        for g in range(NA_GROUP):
            if do_scores:
                scores(u * NA_GROUP + g, parity * NA_GROUP + g)
            if do_values:
                values((u - 2) * NA_GROUP + g, parity * NA_GROUP + g)

    step(0, 0, True, False, False)
    step(1, 1, True, True, False)

    def group_pair_body(u2, carry):
        step(2 * u2, 0, True, True, True)
        step(2 * u2 + 1, 1, True, True, True)
        return carry

    lax.fori_loop(1, n_groups // 2, group_pair_body, 0)
    step(n_groups, 0, False, True, True)
    step(n_groups + 1, 1, False, False, True)

    qc = qc_ref[...]
    nc = qc.shape[0]
    oc = _softmax_pv([_dot_nt(stack(qc), kc)], [vc])
    oc_ref[...] = unstack(oc, nc).astype(oc_ref.dtype)


def _bias_table(rpb, na_cols):
    heads, nr, nc = rpb.shape
    w = GRID_W
    jcol = np.arange(w)
    cstart = np.clip(jcol - na_cols // 2, 0, w - na_cols)
    kcol = np.arange(w)
    inside = (kcol[None, :] >= cstart[:, None]) & (kcol[None, :] < cstart[:, None] + na_cols)
    dc = kcol[None, :] - jcol[:, None] + (na_cols - 1)
    pick = jnp.asarray((dc[None] == np.arange(nc)[:, None, None]) & inside[None], F32)
    t2 = jnp.einsum("hdm,mqk->hdqk", rpb, pick, precision=lax.Precision.HIGHEST)
    t2 = jnp.where(inside[None, None], t2, NEG)
    t3 = jnp.concatenate([t2[:, :-1], t2[:, 1:]], axis=-1)
    t3 = t3.reshape(heads // 2, 2, nr - 1, w, 2 * w).transpose(0, 2, 1, 3, 4)
    return t3.reshape(heads // 2, nr - 1, 2 * w, 2 * w).astype(F32)


def _natten(q, k, v, bias, *, nb, seq_l, seq_c, heads, na_rows):
    t, d = q.shape
    head_dim = d // heads
    assert 2 * head_dim == LANE and 2 * GRID_W == LANE and seq_l % GRID_W == 0
    rows = seq_l // GRID_W
    assert (nb * seq_l) % seq_c == 0
    cblk = nb * seq_l // seq_c
    lat = lambda b, hp: (b, hp)
    ctx = lambda b, hp: (cblk + b, hp)
    return pl.pallas_call(
        functools.partial(_natten_kernel, rows=rows, na_rows=na_rows, head_dim=head_dim),
        grid=(nb, heads // 2),
        in_specs=[pl.BlockSpec((seq_l, LANE), lat),
                  pl.BlockSpec((seq_l, LANE), lat),
                  pl.BlockSpec((seq_l, LANE), lat),
                  pl.BlockSpec((seq_c, LANE), ctx),
                  pl.BlockSpec((seq_c, LANE), ctx),
                  pl.BlockSpec((seq_c, LANE), ctx),
                  pl.BlockSpec((None,) + bias.shape[1:], lambda b, hp: (hp, 0, 0, 0))],
        out_specs=[pl.BlockSpec((seq_l, LANE), lat),
                   pl.BlockSpec((seq_c, LANE), lambda b, hp: (b, hp))],
        out_shape=[jax.ShapeDtypeStruct((nb * seq_l, d), BF16),
                   jax.ShapeDtypeStruct((nb * seq_c, d), BF16)],
        scratch_shapes=[pltpu.VMEM((2 * NA_GROUP, 2 * GRID_W, na_rows * GRID_W + seq_c), F32),
                        pltpu.VMEM((2 * NA_GROUP, 2 * GRID_W, na_rows * GRID_W + seq_c), BF16),
                        pltpu.VMEM((2 * NA_GROUP, 2 * GRID_W, 1), F32)],
        compiler_params=_params(2),
        name="neighbourhood_attention",
    )(q, k, v, k, v, q, bias)


def _split_bf16(x):
    hi = x.astype(BF16)
    return hi, (x - hi.astype(F32)).astype(BF16)


def _route_kernel(al_ref, ac_ref, h_ref, mod_ref, g_ref, wo_ref, wr_ref, h_out_ref, f_ref, info_ref, rt_ref,
                  cnt_ref, *, n_experts, nlat):
    i = pl.program_id(0)

    @pl.when(i == 0)
    def _():
        cnt_ref[...] = jnp.zeros_like(cnt_ref)

    mod = mod_ref[...]
    attn = jnp.where(i < nlat, al_ref[...], ac_ref[...])
    h = h_ref[...] + mod[2:3] * _dot(attn, wo_ref[...])
    h_out_ref[...] = h
    f = _rms_mod(h, g_ref[...], mod[3:4], mod[4:5])
    f_ref[...] = f

    f_hi, f_lo = _split_bf16(f)
    z_hi = _dot(f_hi, wr_ref[...])
    z_lo = _dot(f_lo, wr_ref[...])
    logits = z_hi[:, 0:LANE] + (z_hi[:, LANE:] + z_lo[:, 0:LANE])
    lane_i = lax.broadcasted_iota(I32, logits.shape, 1)
    lane = lane_i.astype(F32)
    logits = jnp.where(lane_i < n_experts, logits, -jnp.inf)
    v1 = logits.max(axis=-1, keepdims=True)
    i1 = jnp.where(logits == v1, lane, float(LANE)).min(axis=-1, keepdims=True)
    rest = jnp.where(lane == i1, -jnp.inf, logits)
    v2 = rest.max(axis=-1, keepdims=True)
    i2 = jnp.where(rest == v2, lane, float(LANE)).min(axis=-1, keepdims=True)
    e2 = jnp.exp(v2 - v1)
    g1 = 1.0 / (1.0 + e2)
    g2 = e2 * g1

    sel1 = lane == i1
    sel2 = lane == i2
    onehot = jnp.where(sel1 | sel2, 1.0, 0.0)
    tr = lax.broadcasted_iota(I32, (TB, TB), 0)
    tc = lax.broadcasted_iota(I32, (TB, TB), 1)
    before = _dot(jnp.where(tc < tr, 1.0, 0.0).astype(BF16), onehot.astype(BF16)) + cnt_ref[0:1, :]
    r1 = jnp.where(sel1, before, 0.0).sum(axis=-1, keepdims=True)
    r2 = jnp.where(sel2, before, 0.0).sum(axis=-1, keepdims=True)
    cnt_ref[...] = cnt_ref[...] + onehot.sum(axis=0, keepdims=True)

    info = jnp.where(lane_i == 0, i1, 0.0)
    info = jnp.where(lane_i == 1, i2, info)
    info = jnp.where(lane_i == 2, r1, info)
    info = jnp.where(lane_i == 3, r2, info)
    info = jnp.where(lane_i == 4, g1, info)
    info = jnp.where(lane_i == 5, g2, info)
    info_ref[...] = info
    rt_ref[...] = jnp.transpose(info)[0:SUBLANE, :].astype(I32)


def _route(a_lat, a_ctx, h, mod, g, w_out, w_router, *, n_rows, seq_l, nb):
    t, d = h.shape
    n_experts = w_router.shape[-1]
    wr = jnp.concatenate(_split_bf16(jnp.zeros((d, LANE), F32).at[:, :n_experts].set(w_router)), axis=1)
    const = lambda i: (0, 0)
    row = lambda i: (i, 0)
    n = n_rows
    assert n_rows % TB == 0 and seq_l % TB == 0 and a_ctx.shape[0] % TB == 0
    n_tiles = n_rows // TB
    tpl = seq_l // TB
    nlat = nb * tpl
    return pl.pallas_call(
        functools.partial(_route_kernel, n_experts=n_experts, nlat=nlat),
        grid=(n_tiles,),
        in_specs=[pl.BlockSpec((TB, d), lambda i: (jnp.minimum(i, nlat - 1), 0)),
                  pl.BlockSpec((TB, d), lambda i: (jnp.maximum(i - nlat, 0), 0)),
                  pl.BlockSpec((TB, d), row),
                  pl.BlockSpec((None, N_MOD, d), lambda i: (jnp.minimum(i // tpl, nb), 0, 0)),
                  pl.BlockSpec((1, d), const),
                  pl.BlockSpec(w_out.shape, const),
                  pl.BlockSpec(wr.shape, const)],
        out_specs=[pl.BlockSpec((TB, d), row),
                   pl.BlockSpec((TB, d), row),
                   pl.BlockSpec((TB, LANE), row),
                   pl.BlockSpec((None, SUBLANE, TB), lambda i: (i, 0, 0)),
                   pl.BlockSpec((SUBLANE, LANE), const)],
        out_shape=[jax.ShapeDtypeStruct((n, d), F32),
                   jax.ShapeDtypeStruct((n, d), F32),
                   jax.ShapeDtypeStruct((n, LANE), F32),
                   jax.ShapeDtypeStruct((n_tiles, SUBLANE, TB), I32),
                   jax.ShapeDtypeStruct((SUBLANE, LANE), F32)],
        compiler_params=_params(1),
        name="attn_out_router",
    )(a_lat, a_ctx, h, mod, g.reshape(1, d), w_out, wr)


def _row_copies(src_ref, dst_ref, idx_ref, sem, scatter, wait_here=True):
    def issue(jj, wait):
        for u in range(SUBLANE):
            for k in range(TOP_K):
                row = idx_ref[0, k * TB + jj * SUBLANE + u]
                if scatter:
                    cp = pltpu.make_async_copy(src_ref.at[jj, pl.ds(u, 1)], dst_ref.at[pl.ds(row, 1)], sem)
                else:
                    cp = pltpu.make_async_copy(src_ref.at[pl.ds(row, 1)], dst_ref.at[k, jj, pl.ds(u, 1)], sem)
                if wait:
                    cp.wait()
                else:
                    cp.start(priority=k % 2)

    def start_body(jj, c):
        issue(jj, False)
        return c

    def wait_body(jj, c):
        issue(jj, True)
        return c

    lax.fori_loop(0, TB // SUBLANE, start_body, 0)
    if wait_here:
        lax.fori_loop(0, TB // SUBLANE, wait_body, 0)


def _dispatch_kernel(pad_ref, dest_ref, f_ref, xs_ref, zbuf, sem, zsem, *, n_fills):
    @pl.when(pl.program_id(0) == 0)
    def _():
        zbuf[...] = jnp.zeros_like(zbuf)

        def fill(e):
            return pltpu.make_async_copy(zbuf, xs_ref.at[pl.ds(pl.multiple_of(pad_ref[e] * TM, TM), TM)], zsem)

        for e in range(n_fills):
            @pl.when(pad_ref[e] >= 0)
            def _():
                fill(e).start()
        for e in range(n_fills):
            @pl.when(pad_ref[e] >= 0)
            def _():
                fill(e).wait()

    _row_copies(f_ref, xs_ref, dest_ref, sem, scatter=True)


def _dispatch(f, dest, pad_at, *, n_tiles, n_rows):
    n_fills = pad_at.shape[0]
    n, d = f.shape
    grid_spec = pltpu.PrefetchScalarGridSpec(
        num_scalar_prefetch=1,
        grid=(n_tiles,),
        in_specs=[pl.BlockSpec((None, 1, TOP_K * TB), lambda i, pad: (i, 0, 0), memory_space=pltpu.SMEM),
                  pl.BlockSpec((TB // SUBLANE, SUBLANE, d), lambda i, pad: (i, 0, 0))],
        out_specs=pl.BlockSpec(memory_space=pl.ANY),
        scratch_shapes=[pltpu.VMEM((TM, d), F32), pltpu.SemaphoreType.DMA(()), pltpu.SemaphoreType.DMA(())],
    )
    return pl.pallas_call(
        functools.partial(_dispatch_kernel, n_fills=n_fills),
        grid_spec=grid_spec,
        out_shape=jax.ShapeDtypeStruct((n_rows, d), F32),
        compiler_params=_params(1),
        name="moe_dispatch",
    )(pad_at, dest, f.reshape(n // SUBLANE, SUBLANE, d))


def _experts_kernel(te_ref, tv_ref, nu_ref, x_ref, wg_ref, wu_ref, wd_ref, o_ref, xb_ref):
    i = pl.program_id(0)
    j = pl.program_id(1)
    used = i < nu_ref[0]
    valid = tv_ref[i]

    @pl.when(j == 0)
    def _():
        o_ref[...] = jnp.zeros_like(o_ref)

    def compute(n_rows):
        @pl.when(j == 0)
        def _():
            xb_ref[0:n_rows, :] = x_ref[0:n_rows, :].astype(BF16)

        xb = xb_ref[0:n_rows, :]
        act = _silu(_dot(xb, wg_ref[...].astype(BF16))) * _dot(xb, wu_ref[...].astype(BF16))
        o_ref[0:n_rows, :] += _dot(act.astype(BF16), wd_ref[...].astype(BF16))

    for n_rows in range(SUB, TM + 1, SUB):
        @pl.when(used & (valid > n_rows - SUB) & (valid <= n_rows))
        def _():
            compute(n_rows)


def _experts(xs, tile_expert, tile_valid, n_used, layer, we_gate, we_up, we_down):
    p, d = xs.shape
    dfe = we_gate.shape[-1]
    assert dfe % FC == 0 and p % TM == 0 and TM % SUB == 0
    n_chunks = dfe // FC
    n_tiles = p // TM

    def tile(i, nu):
        return jnp.minimum(i, nu[0] - 1)

    def chunk(i, j, nu):
        return jnp.where(i < nu[0], j, n_chunks - 1)

    grid_spec = pltpu.PrefetchScalarGridSpec(
        num_scalar_prefetch=3,
        grid=(n_tiles, n_chunks),
        in_specs=[pl.BlockSpec((TM, d), lambda i, j, te, tv, nu: (tile(i, nu), 0)),
                  pl.BlockSpec((None, None, d, FC),
                               lambda i, j, te, tv, nu: (layer, te[tile(i, nu)], 0, chunk(i, j, nu))),
                  pl.BlockSpec((None, None, d, FC),
                               lambda i, j, te, tv, nu: (layer, te[tile(i, nu)], 0, chunk(i, j, nu))),
                  pl.BlockSpec((None, None, FC, d),
                               lambda i, j, te, tv, nu: (layer, te[tile(i, nu)], chunk(i, j, nu), 0))],
        out_specs=pl.BlockSpec((TM, d), lambda i, j, te, tv, nu: (i, 0)),
        scratch_shapes=[pltpu.VMEM((TM, d), BF16)],
    )
    return pl.pallas_call(
        _experts_kernel,
        grid_spec=grid_spec,
        out_shape=jax.ShapeDtypeStruct((p, d), F32),
        compiler_params=_params(2),
        name="expert_swiglu",
    )(tile_expert, tile_valid, n_used, xs, we_gate, we_up, we_down)


def _combine_kernel(dest_ref, ys_ref, info_ref, h_ref, mod_ref, gf_ref, o_ref, ybuf, sem, *, final_norm, n_tiles):
    i = pl.program_id(0)
    slot = lax.rem(i, 2)

    @pl.when(i < n_tiles)
    def _():
        _row_copies(ys_ref, ybuf.at[slot], dest_ref, sem.at[slot], scatter=False, wait_here=False)

    @pl.when(i > 0)
    def _():
        done = ybuf.at[1 - slot]
        pltpu.make_async_copy(done, done, sem.at[1 - slot]).wait()
        info = info_ref[...]
        d = h_ref.shape[1]
        mix = info[:, 4:5] * done[0].reshape(TB, d) + info[:, 5:6] * done[1].reshape(TB, d)
        h = h_ref[...] + mod_ref[...][5:6] * mix
        if final_norm:
            h = (h * lax.rsqrt(jnp.mean(h * h, axis=-1, keepdims=True) + EPS)) * gf_ref[...]
        o_ref[...] = h


def _combine(ys, dest, info, h, mod, g_final, *, n_tiles, tpl, nb, final_norm):
    n, d = h.shape
    prev = lambda i: (jnp.maximum(i - 1, 0), 0)
    return pl.pallas_call(
        functools.partial(_combine_kernel, final_norm=final_norm, n_tiles=n_tiles),
        grid=(n_tiles + 1,),
        in_specs=[pl.BlockSpec((None, 1, TOP_K * TB), lambda i: (jnp.minimum(i, n_tiles - 1), 0, 0),
                               memory_space=pltpu.SMEM),
                  pl.BlockSpec(memory_space=pl.ANY),
                  pl.BlockSpec((TB, LANE), prev),
                  pl.BlockSpec((TB, d), prev),
                  pl.BlockSpec((None, N_MOD, d), lambda i: (jnp.minimum(jnp.maximum(i - 1, 0) // tpl, nb), 0, 0)),
                  pl.BlockSpec((1, d), lambda i: (0, 0))],
        out_specs=pl.BlockSpec((TB, d), prev),
        out_shape=jax.ShapeDtypeStruct((n_tiles * TB, d), F32),
        scratch_shapes=[pltpu.VMEM((2, TOP_K, TB // SUBLANE, SUBLANE, d), F32), pltpu.SemaphoreType.DMA((2,))],
        compiler_params=_params(1),
        name="moe_combine",
    )(dest, ys, info, h, mod, g_final.reshape(1, d))


def _moe(f, info, rt, counts, h, mod, g_final, layer, we_gate, we_up, we_down, *, n_tiles, tpl, nb, final_norm):
    n = f.shape[0]
    n_experts = we_gate.shape[1]
    n_xtiles = -(-TOP_K * n // TM) + n_experts
    n_rows = n_xtiles * TM

    cnt = counts[0, :n_experts].astype(I32)
    tiles_e = (cnt + TM - 1) // TM
    tile_end = jnp.cumsum(tiles_e)
    starts = (tile_end - tiles_e) * TM
    n_used = tile_end[-1:]
    tile_ids = jnp.arange(n_xtiles, dtype=I32)
    tile_expert = jnp.minimum(jnp.sum(tile_ids[:, None] >= tile_end[None, :], axis=1), n_experts - 1).astype(I32)
    tile_valid = jnp.clip(cnt[tile_expert] - (tile_ids * TM - starts[tile_expert]), 0, TM).astype(I32)
    choice = rt[:, 0:TOP_K, :]
    rank = rt[:, TOP_K:2 * TOP_K, :]
    start_of = sum(jnp.where(choice == e, starts[e], 0) for e in range(n_experts))
    dest = (start_of + rank).reshape(n_tiles, 1, TOP_K * TB)
    tail = n_used + jnp.arange(n_experts, dtype=I32)
    pad_at = jnp.concatenate([jnp.where(tiles_e > 0, tile_end - 1, -1),
                              jnp.where(tail < n_xtiles, tail, -1)]).astype(I32)

    xs = _dispatch(f, dest, pad_at, n_tiles=n_tiles, n_rows=n_rows)
    ys = _experts(xs, tile_expert, tile_valid, n_used, layer, we_gate, we_up, we_down)
    return _combine(ys, dest, info, h, mod, g_final, n_tiles=n_tiles, tpl=tpl, nb=nb, final_norm=final_norm)


def kernel(x, c, ctx, c_ctx, w_mod, b_mod, g_mix, g_ffn, g_final, w_in_ab, conv_w, pool_w, pool_scale,
           w_out_ab, w_ff_gate, w_ff_up, w_ff_down, w_qkv, rpb, w_out_na, w_router, we_gate, we_up, we_down):
    nb, seq_l, d = x.shape
    seq_c = ctx.shape[1]
    depth = w_mod.shape[0]
    heads = rpb.shape[1]
    na_rows = (rpb.shape[2] + 1) // 2
    na_cols = (rpb.shape[3] + 1) // 2
    assert seq_l % TB == 0 and seq_c % TB == 0 and nb + 1 <= SUBLANE and depth % 2 == 0
    tpl = seq_l // TB
    nlat = nb * tpl
    nall = nlat + nb * seq_c // TB

    cond = jnp.zeros((SUBLANE, d), F32).at[:nb].set(c).at[nb].set(c_ctx)
    mods = _adaln(cond, w_mod, b_mod).reshape(depth, SUBLANE, N_MOD, d)

    h = jnp.concatenate([x.reshape(nb * seq_l, d), ctx.reshape(nb * seq_c, d)], axis=0)
    bf = lambda a: a.astype(BF16)
    out = None
    for i in range(depth):
        j = i // 2
        last = i == depth - 1
        mod = mods[i]
        if i % 2 == 0:
            h = _mixer(h, mod, g_mix[i], bf(w_in_ab[j]), conv_w[j], bf(pool_w[j]), pool_scale[j], bf(w_out_ab[j]),
                       n_tiles=nall, nlat=nlat, nb=nb, seq_l=seq_l, seq_c=seq_c)
            h = _ffn(h, mod, g_ffn[i], bf(w_ff_gate[j]), bf(w_ff_up[j]), bf(w_ff_down[j]),
                     n_rows=nall * TB, seq_l=seq_l, nb=nb)
        else:
            n_tiles = nlat if last else nall
            q, k, v = _qkv(h, mod, g_mix[i], bf(w_qkv[j]), n_rows=nall * TB, seq_l=seq_l, nb=nb,
                           qscale=(d // heads) ** -0.5)
            o_lat, o_ctx = _natten(q, k, v, _bias_table(rpb[j], na_cols), nb=nb, seq_l=seq_l, seq_c=seq_c,
                                   heads=heads, na_rows=na_rows)
            h, f, info, rt, counts = _route(o_lat, o_ctx, h, mod, g_ffn[i], bf(w_out_na[j]), w_router[j],
                                            n_rows=n_tiles * TB, seq_l=seq_l, nb=nb)
            h = _moe(f, info, rt, counts, h, mod, g_final, j, we_gate, we_up, we_down,
                     n_tiles=n_tiles, tpl=tpl, nb=nb, final_norm=last)
            if last:
                out = h
    return out.reshape(nb, seq_l, d)
```

```python
import functools

import numpy as np
import jax
import jax.numpy as jnp
from jax import lax
from jax.experimental import pallas as pl
from jax.experimental.pallas import tpu as pltpu

F32 = jnp.float32
BF16 = jnp.bfloat16
I32 = jnp.int32

GRID_W = 64
POOL_WINDOWS = (2, 4, 8, 16)
N_MOD = 6
TOP_K = 2
EPS = 1e-6
NEG = -1e30

LANE = 128
SUBLANE = 8
TB = 256
TBD = 512
HALO = SUBLANE
TM = 1024
SUB = 256
FC = 512
NA_GROUP = 4
VMEM_LIMIT = 56 * 1024 * 1024


def _params(n_axes):
    return pltpu.CompilerParams(dimension_semantics=("arbitrary",) * n_axes,
                                vmem_limit_bytes=VMEM_LIMIT)


def _rms_mod(x, g, shift, scale):
    y = x * lax.rsqrt(jnp.mean(x * x, axis=-1, keepdims=True) + EPS)
    return (y * g) * (1.0 + scale) + shift


def _silu(x):
    return x * jax.nn.sigmoid(x)


def _dot(a, b):
    return jnp.dot(a, b, preferred_element_type=F32)


def _dot_nt(a, b):
    return lax.dot_general(a, b, (((1,), (1,)), ((), ())), preferred_element_type=F32)


def _adaln_kernel(cond_ref, w_ref, b_ref, o_ref):
    s = _silu(cond_ref[...]).astype(BF16)
    o_ref[...] = _dot(s, w_ref[...].astype(BF16)) + b_ref[...]


def _adaln(cond, w_mod, b_mod):
    depth, d, nd = w_mod.shape
    tn = 1536
    assert nd % tn == 0
    return pl.pallas_call(
        _adaln_kernel,
        grid=(depth, nd // tn),
        in_specs=[pl.BlockSpec((SUBLANE, d), lambda l, n: (0, 0)),
                  pl.BlockSpec((None, d, tn), lambda l, n: (l, 0, n)),
                  pl.BlockSpec((None, 1, tn), lambda l, n: (l, 0, n))],
        out_specs=pl.BlockSpec((None, SUBLANE, tn), lambda l, n: (l, 0, n)),
        out_shape=jax.ShapeDtypeStruct((depth, SUBLANE, nd), F32),
        compiler_params=_params(2),
        name="adaln",
    )(cond, w_mod, b_mod.reshape(depth, 1, nd))


def _mixer_kernel(ap_ref, ac_ref, an_ref, bp_ref, bc_ref, bn_ref, mod_ref, g_ref, win_ref, cw_ref, pw_ref, ps_ref,
                  wout_ref, o_ref, xs_ref, z_ref, p_ref, winb_ref, woutb_ref,
                  *, nlat, tpl, tpc, seq_l, seq_c, d_conv, pool_group):
    i = pl.program_id(0)
    is_ctx = i >= nlat
    pos = jnp.where(is_ctx, lax.rem(i - nlat, tpc), lax.rem(i, tpl)) * TB
    seq_len = jnp.where(is_ctx, seq_c, seq_l)
    ext = TB + 2 * HALO

    @pl.when(i == 0)
    def _():
        winb_ref[...] = win_ref[...].astype(BF16)
        woutb_ref[...] = wout_ref[...].astype(BF16)

    hc = jnp.where(is_ctx, bc_ref[...], ac_ref[...])
    xs_ref[0:HALO, :] = jnp.where(is_ctx, bp_ref[...], ap_ref[...])
    xs_ref[HALO:HALO + TB, :] = hc
    xs_ref[HALO + TB:ext, :] = jnp.where(is_ctx, bn_ref[...], an_ref[...])
    mod = mod_ref[...]
    a = _rms_mod(xs_ref[...], g_ref[...], mod[0:1], mod[1:2]).astype(BF16)
    u = _dot(a, winb_ref[...])
    srow = lax.broadcasted_iota(I32, (ext, 1), 0) + (pos - HALO)
    u = jnp.where((srow >= 0) & (srow < seq_len), u, 0.0)

    z_ref[...] = u[:, d_conv:2 * d_conv] * u[:, 2 * d_conv:3 * d_conv]
    p_ref[...] = u[:, 3 * d_conv:]
    cw = cw_ref[...]
    conv = (z_ref[HALO - 1:HALO - 1 + TB, :] * cw[0:1] + z_ref[HALO:HALO + TB, :] * cw[1:2]
            + z_ref[HALO + 1:HALO + 1 + TB, :] * cw[2:3])
    pieces = [u[HALO:HALO + TB, 0:d_conv] * conv]

    spos = srow[HALO:HALO + TB]
    ps = ps_ref[...]
    for g, win in enumerate(POOL_WINDOWS):
        lo, hi = win // 2, win - 1 - win // 2
        cols = slice(g * pool_group, (g + 1) * pool_group)
        acc = p_ref[HALO - lo:HALO - lo + TB, cols]
        for dlt in range(-lo + 1, hi + 1):
            acc = acc + p_ref[HALO + dlt:HALO + dlt + TB, cols]
        cnt = jnp.minimum(spos + hi, seq_len - 1) - jnp.maximum(spos - lo, 0) + 1
        diff = acc / cnt.astype(F32) - p_ref[HALO:HALO + TB, cols]
        pieces.append(_dot(diff.astype(BF16), pw_ref[g].astype(BF16)) * ps[:, cols])
    cat = jnp.concatenate(pieces, axis=-1).astype(BF16)
    o_ref[...] = hc + mod[2:3] * _dot(cat, woutb_ref[...])


def _mixer(src_a, src_b, b_off, mod, g, layer, w_in, conv_w, pool_w, pool_scale, w_out,
           *, n_tiles, nlat, nb, seq_l, seq_c):
    d = src_a.shape[1]
    d_conv = conv_w.shape[-1]
    d_pool = pool_scale.shape[-1]
    pool_group = pool_w.shape[-1]
    assert pool_group % LANE == 0 and d_conv % LANE == 0 and len(POOL_WINDOWS) == pool_w.shape[1]
    tpl, tpc = seq_l // TB, seq_c // TB
    ext = TB + 2 * HALO
    hb = TB // HALO
    a_last = src_a.shape[0] // HALO - 1
    b_last = src_b.shape[0] // HALO - 1
    kern = functools.partial(_mixer_kernel, nlat=nlat, tpl=tpl, tpc=tpc, seq_l=seq_l, seq_c=seq_c,
                             d_conv=d_conv, pool_group=pool_group)
    a_tile = lambda i: jnp.minimum(i, nlat - 1)
    b_tile = lambda i: jnp.maximum(i - nlat, 0) + b_off
    once = pl.Buffered(1)
    return pl.pallas_call(
        kern,
        grid=(n_tiles,),
        in_specs=[pl.BlockSpec((HALO, d), lambda i: (jnp.maximum(a_tile(i) * hb - 1, 0), 0)),
                  pl.BlockSpec((TB, d), lambda i: (a_tile(i), 0)),
                  pl.BlockSpec((HALO, d), lambda i: (jnp.minimum((a_tile(i) + 1) * hb, a_last), 0)),
                  pl.BlockSpec((HALO, d), lambda i: (jnp.maximum(b_tile(i) * hb - 1, 0), 0)),
                  pl.BlockSpec((TB, d), lambda i: (b_tile(i), 0)),
                  pl.BlockSpec((HALO, d), lambda i: (jnp.minimum((b_tile(i) + 1) * hb, b_last), 0)),
                  pl.BlockSpec((None, N_MOD, d), lambda i: (jnp.minimum(i // tpl, nb), 0, 0)),
                  pl.BlockSpec((1, d), lambda i: (0, 0)),
                  pl.BlockSpec((None,) + w_in.shape[1:], lambda i: (layer, 0, 0), pipeline_mode=once),
                  pl.BlockSpec((None,) + conv_w.shape[1:], lambda i: (layer, 0, 0)),
                  pl.BlockSpec((None,) + pool_w.shape[1:], lambda i: (layer, 0, 0, 0)),
                  pl.BlockSpec((1, d_pool), lambda i: (0, 0)),
                  pl.BlockSpec((None,) + w_out.shape[1:], lambda i: (layer, 0, 0), pipeline_mode=once)],
        out_specs=pl.BlockSpec((TB, d), lambda i: (i, 0)),
        out_shape=jax.ShapeDtypeStruct((n_tiles * TB, d), F32),
        scratch_shapes=[pltpu.VMEM((ext, d), F32), pltpu.VMEM((ext, d_conv), F32), pltpu.VMEM((ext, d_pool), F32),
                        pltpu.VMEM(w_in.shape[1:], BF16), pltpu.VMEM(w_out.shape[1:], BF16)],
        compiler_params=_params(1),
        name="conv_pool_mixer",
    )(src_a, src_a, src_a, src_b, src_b, src_b, mod, g.reshape(1, d), w_in, conv_w, pool_w,
      pool_scale.reshape(1, d_pool), w_out)


def _ffn_kernel(h_ref, mod_ref, g_ref, wg_ref, wu_ref, wd_ref, o_ref, act_ref, *, chunks):
    mod = mod_ref[...]
    h = h_ref[...]
    f = _rms_mod(h, g_ref[...], mod[3:4], mod[4:5]).astype(BF16)
    for c0, c1 in chunks:
        gate = _dot(f, wg_ref[:, c0:c1])
        up = _dot(f, wu_ref[:, c0:c1])
        act_ref[:, c0:c1] = (_silu(gate) * up).astype(BF16)
    o_ref[...] = h + mod[5:6] * _dot(act_ref[...], wd_ref[...])


def _ffn(h, mod, g, wg, wu, wd, *, n_rows, seq_l, nb):
    t, d = h.shape
    dff = wg.shape[-1]
    step = 768
    chunks = tuple((c, min(c + step, dff)) for c in range(0, dff, step))
    const = lambda i: (0, 0)
    once = pl.Buffered(1)
    assert n_rows % TBD == 0 and seq_l % TBD == 0
    tpl = seq_l // TBD
    return pl.pallas_call(
        functools.partial(_ffn_kernel, chunks=chunks),
        grid=(n_rows // TBD,),
        in_specs=[pl.BlockSpec((TBD, d), lambda i: (i, 0)),
                  pl.BlockSpec((None, N_MOD, d), lambda i: (jnp.minimum(i // tpl, nb), 0, 0)),
                  pl.BlockSpec((1, d), const),
                  pl.BlockSpec(wg.shape, const, pipeline_mode=once),
                  pl.BlockSpec(wu.shape, const, pipeline_mode=once),
                  pl.BlockSpec(wd.shape, const, pipeline_mode=once)],
        out_specs=pl.BlockSpec((TBD, d), lambda i: (i, 0)),
        out_shape=jax.ShapeDtypeStruct((n_rows, d), F32),
        scratch_shapes=[pltpu.VMEM((TBD, dff), BF16)],
        compiler_params=_params(1),
        name="dense_swiglu",
    )(h, mod, g.reshape(1, d), wg, wu, wd)


def _qkv_kernel(h_ref, mod_ref, g_ref, w_ref, q_ref, k_ref, v_ref, wb_ref, *, d, qscale):
    @pl.when(pl.program_id(0) == 0)
    def _():
        wb_ref[...] = w_ref[...].astype(BF16)

    mod = mod_ref[...]
    a = _rms_mod(h_ref[...], g_ref[...], mod[0:1], mod[1:2]).astype(BF16)
    q_ref[...] = (_dot(a, wb_ref[:, 0:d]) * qscale).astype(BF16)
    k_ref[...] = _dot(a, wb_ref[:, d:2 * d]).astype(BF16)
    v_ref[...] = _dot(a, wb_ref[:, 2 * d:3 * d]).astype(BF16)


def _qkv(h, mod, g, layer, w, *, n_rows, seq_l, nb, qscale):
    t, d = h.shape
    const = lambda i: (0, 0)
    out = jax.ShapeDtypeStruct((n_rows, d), BF16)
    assert n_rows % TBD == 0 and seq_l % TBD == 0
    tpl = seq_l // TBD
    return pl.pallas_call(
        functools.partial(_qkv_kernel, d=d, qscale=qscale),
        grid=(n_rows // TBD,),
        in_specs=[pl.BlockSpec((TBD, d), lambda i: (i, 0)),
                  pl.BlockSpec((None, N_MOD, d), lambda i: (jnp.minimum(i // tpl, nb), 0, 0)),
                  pl.BlockSpec((1, d), const),
                  pl.BlockSpec((None,) + w.shape[1:], lambda i: (layer, 0, 0), pipeline_mode=pl.Buffered(1))],
        out_specs=[pl.BlockSpec((TBD, d), lambda i: (i, 0))] * 3,
        out_shape=[out, out, out],
        scratch_shapes=[pltpu.VMEM(w.shape[1:], BF16)],
        compiler_params=_params(1),
        name="qkv_proj",
    )(h, mod, g.reshape(1, d), w)


def _softmax_pv(s_parts, v_parts):
    m = s_parts[0].max(axis=-1, keepdims=True)
    for s in s_parts[1:]:
        m = jnp.maximum(m, s.max(axis=-1, keepdims=True))
    den = 0.0
    out = 0.0
    for s, v in zip(s_parts, v_parts):
        p = jnp.exp(s - m)
        den = den + p.sum(axis=-1, keepdims=True)
        out = out + _dot(p.astype(BF16), v)
    return out / den


def _natten_kernel(q_ref, k_ref, v_ref, kc_ref, vc_ref, qc_ref, bias_ref, o_ref, oc_ref,
                   s_scr, p_scr, den_scr, *, rows, na_rows, head_dim):
    w = GRID_W
    band = na_rows * w
    lane = lax.broadcasted_iota(I32, (1, LANE), 1)
    head0 = lane < head_dim
    zero = jnp.zeros((), BF16)
    kc = kc_ref[...]
    vc = vc_ref[...]

    def stack(q):
        return jnp.concatenate([jnp.where(head0, q, zero), jnp.where(head0, zero, q)], axis=0)

    def unstack(o, n):
        return jnp.where(head0, o[0:n], o[n:2 * n])

    def offsets(r):
        start = jnp.clip(r - na_rows // 2, 0, rows - na_rows)
        return pl.multiple_of(r * w, w), pl.multiple_of(start * w, w), start - r + (na_rows - 1)

    def scores(r, slot):
        q0, k0, d0 = offsets(r)
        qq = stack(q_ref[pl.ds(q0, w), :])
        bias = jnp.concatenate([bias_ref[d0 + 2 * p] for p in range(band // LANE)], axis=-1)
        s_scr[slot, :, 0:band] = _dot_nt(qq, k_ref[pl.ds(k0, band), :]) + bias
        s_scr[slot, :, band:] = _dot_nt(qq, kc)

    def softmax(slot):
        s = s_scr[slot]
        p = jnp.exp(s - s.max(axis=-1, keepdims=True))
        den_scr[slot] = p.sum(axis=-1, keepdims=True)
        p_scr[slot] = p.astype(BF16)

    def values(r, slot):
        q0, k0, _ = offsets(r)
        o = _dot(p_scr[slot, :, 0:band], v_ref[pl.ds(k0, band), :]) + _dot(p_scr[slot, :, band:], vc)
        o_ref[pl.ds(q0, w), :] = unstack(o / den_scr[slot], w).astype(o_ref.dtype)

    n_groups = rows // NA_GROUP
    assert rows % NA_GROUP == 0 and n_groups % 2 == 0 and n_groups >= 4

    def step(u, parity, do_scores, do_softmax, do_values):
        for g in range(NA_GROUP):
            if do_softmax:
                softmax((1 - parity) * NA_GROUP + g)
        for g in range(NA_GROUP):
            if do_scores:
                scores(u * NA_GROUP + g, parity * NA_GROUP + g)
            if do_values:
                values((u - 2) * NA_GROUP + g, parity * NA_GROUP + g)

    step(0, 0, True, False, False)
    step(1, 1, True, True, False)

    def group_pair_body(u2, carry):
        step(2 * u2, 0, True, True, True)
        step(2 * u2 + 1, 1, True, True, True)
        return carry

    lax.fori_loop(1, n_groups // 2, group_pair_body, 0)
    step(n_groups, 0, False, True, True)
    step(n_groups + 1, 1, False, False, True)

    qc = qc_ref[...]
    nc = qc.shape[0]
    oc = _softmax_pv([_dot_nt(stack(qc), kc)], [vc])
    oc_ref[...] = unstack(oc, nc).astype(oc_ref.dtype)


def _bias_table(rpb, na_cols):
    heads, nr, nc = rpb.shape
    w = GRID_W
    jcol = np.arange(w)
    cstart = np.clip(jcol - na_cols // 2, 0, w - na_cols)
    kcol = np.arange(w)
    inside = (kcol[None, :] >= cstart[:, None]) & (kcol[None, :] < cstart[:, None] + na_cols)
    dc = kcol[None, :] - jcol[:, None] + (na_cols - 1)
    pick = jnp.asarray((dc[None] == np.arange(nc)[:, None, None]) & inside[None], F32)
    t2 = jnp.einsum("hdm,mqk->hdqk", rpb, pick, precision=lax.Precision.HIGHEST)
    t2 = jnp.where(inside[None, None], t2, NEG)
    t3 = jnp.concatenate([t2[:, :-1], t2[:, 1:]], axis=-1)
    t3 = t3.reshape(heads // 2, 2, nr - 1, w, 2 * w).transpose(0, 2, 1, 3, 4)
    return t3.reshape(heads // 2, nr - 1, 2 * w, 2 * w).astype(F32)


def _natten(q, k, v, bias, *, nb, seq_l, seq_c, heads, na_rows):
    t, d = q.shape
    head_dim = d // heads
    assert 2 * head_dim == LANE and 2 * GRID_W == LANE and seq_l % GRID_W == 0
    rows = seq_l // GRID_W
    assert (nb * seq_l) % seq_c == 0
    cblk = nb * seq_l // seq_c
    lat = lambda b, hp: (b, hp)
    ctx = lambda b, hp: (cblk + b, hp)
    return pl.pallas_call(
        functools.partial(_natten_kernel, rows=rows, na_rows=na_rows, head_dim=head_dim),
        grid=(nb, heads // 2),
        in_specs=[pl.BlockSpec((seq_l, LANE), lat),
                  pl.BlockSpec((seq_l, LANE), lat),
                  pl.BlockSpec((seq_l, LANE), lat),
                  pl.BlockSpec((seq_c, LANE), ctx),
                  pl.BlockSpec((seq_c, LANE), ctx),
                  pl.BlockSpec((seq_c, LANE), ctx),
                  pl.BlockSpec((None,) + bias.shape[1:], lambda b, hp: (hp, 0, 0, 0))],
        out_specs=[pl.BlockSpec((seq_l, LANE), lat),
                   pl.BlockSpec((seq_c, LANE), lambda b, hp: (b, hp))],
        out_shape=[jax.ShapeDtypeStruct((nb * seq_l, d), BF16),
                   jax.ShapeDtypeStruct((nb * seq_c, d), BF16)],
        scratch_shapes=[pltpu.VMEM((2 * NA_GROUP, 2 * GRID_W, na_rows * GRID_W + seq_c), F32),
                        pltpu.VMEM((2 * NA_GROUP, 2 * GRID_W, na_rows * GRID_W + seq_c), BF16),
                        pltpu.VMEM((2 * NA_GROUP, 2 * GRID_W, 1), F32)],
        compiler_params=_params(2),
        name="neighbourhood_attention",
    )(q, k, v, k, v, q, bias)


def _split_bf16(x):
    hi = x.astype(BF16)
    return hi, (x - hi.astype(F32)).astype(BF16)


def _route_kernel(al_ref, ac_ref, h_ref, mod_ref, g_ref, wo_ref, wr_ref, h_out_ref, f_ref, info_ref, rt_ref,
                  cnt_ref, wob_ref, *, n_experts, nlat):
    i = pl.program_id(0)

    @pl.when(i == 0)
    def _():
        cnt_ref[...] = jnp.zeros_like(cnt_ref)
        wob_ref[...] = wo_ref[...].astype(BF16)

    mod = mod_ref[...]
    attn = jnp.where(i < nlat, al_ref[...], ac_ref[...])
    h = h_ref[...] + mod[2:3] * _dot(attn, wob_ref[...])
    h_out_ref[...] = h
    f = _rms_mod(h, g_ref[...], mod[3:4], mod[4:5])
    f_ref[...] = f

    f_hi, f_lo = _split_bf16(f)
    z_hi = _dot(f_hi, wr_ref[...])
    z_lo = _dot(f_lo, wr_ref[...])
    logits = z_hi[:, 0:LANE] + (z_hi[:, LANE:] + z_lo[:, 0:LANE])
    lane_i = lax.broadcasted_iota(I32, logits.shape, 1)
    lane = lane_i.astype(F32)
    logits = jnp.where(lane_i < n_experts, logits, -jnp.inf)
    v1 = logits.max(axis=-1, keepdims=True)
    i1 = jnp.where(logits == v1, lane, float(LANE)).min(axis=-1, keepdims=True)
    rest = jnp.where(lane == i1, -jnp.inf, logits)
    v2 = rest.max(axis=-1, keepdims=True)
    i2 = jnp.where(rest == v2, lane, float(LANE)).min(axis=-1, keepdims=True)
    e2 = jnp.exp(v2 - v1)
    g1 = 1.0 / (1.0 + e2)
    g2 = e2 * g1

    sel1 = lane == i1
    sel2 = lane == i2
    onehot = jnp.where(sel1 | sel2, 1.0, 0.0)
    tr = lax.broadcasted_iota(I32, (TB, TB), 0)
    tc = lax.broadcasted_iota(I32, (TB, TB), 1)
    before = _dot(jnp.where(tc < tr, 1.0, 0.0).astype(BF16), onehot.astype(BF16)) + cnt_ref[0:1, :]
    r1 = jnp.where(sel1, before, 0.0).sum(axis=-1, keepdims=True)
    r2 = jnp.where(sel2, before, 0.0).sum(axis=-1, keepdims=True)
    cnt_ref[...] = cnt_ref[...] + onehot.sum(axis=0, keepdims=True)

    info = jnp.where(lane_i == 0, i1, 0.0)
    info = jnp.where(lane_i == 1, i2, info)
    info = jnp.where(lane_i == 2, r1, info)
    info = jnp.where(lane_i == 3, r2, info)
    info = jnp.where(lane_i == 4, g1, info)
    info = jnp.where(lane_i == 5, g2, info)
    info_ref[...] = info
    rt_ref[...] = jnp.transpose(info)[0:SUBLANE, :].astype(I32)


def _route(a_lat, a_ctx, h, mod, g, layer, w_out, w_router, *, n_rows, seq_l, nb):
    t, d = h.shape
    n_experts = w_router.shape[-1]
    wr = jnp.concatenate(_split_bf16(jnp.zeros((d, LANE), F32).at[:, :n_experts].set(w_router)), axis=1)
    const = lambda i: (0, 0)
    row = lambda i: (i, 0)
    n = n_rows
    assert n_rows % TB == 0 and seq_l % TB == 0 and a_ctx.shape[0] % TB == 0
    n_tiles = n_rows // TB
    tpl = seq_l // TB
    nlat = nb * tpl
    return pl.pallas_call(
        functools.partial(_route_kernel, n_experts=n_experts, nlat=nlat),
        grid=(n_tiles,),
        in_specs=[pl.BlockSpec((TB, d), lambda i: (jnp.minimum(i, nlat - 1), 0)),
                  pl.BlockSpec((TB, d), lambda i: (jnp.maximum(i - nlat, 0), 0)),
                  pl.BlockSpec((TB, d), row),
                  pl.BlockSpec((None, N_MOD, d), lambda i: (jnp.minimum(i // tpl, nb), 0, 0)),
                  pl.BlockSpec((1, d), const),
                  pl.BlockSpec((None,) + w_out.shape[1:], lambda i: (layer, 0, 0), pipeline_mode=pl.Buffered(1)),
                  pl.BlockSpec(wr.shape, const)],
        out_specs=[pl.BlockSpec((TB, d), row),
                   pl.BlockSpec((TB, d), row),
                   pl.BlockSpec((TB, LANE), row),
                   pl.BlockSpec((None, SUBLANE, TB), lambda i: (i, 0, 0)),
                   pl.BlockSpec((SUBLANE, LANE), const)],
        out_shape=[jax.ShapeDtypeStruct((n, d), F32),
                   jax.ShapeDtypeStruct((n, d), F32),
                   jax.ShapeDtypeStruct((n, LANE), F32),
                   jax.ShapeDtypeStruct((n_tiles, SUBLANE, TB), I32),
                   jax.ShapeDtypeStruct((SUBLANE, LANE), F32)],
        scratch_shapes=[pltpu.VMEM(w_out.shape[1:], BF16)],
        compiler_params=_params(1),
        name="attn_out_router",
    )(a_lat, a_ctx, h, mod, g.reshape(1, d), w_out, wr)


def _row_copies(src_ref, dst_ref, idx_ref, sem, scatter, wait_here=True):
    def issue(jj, wait):
        for u in range(SUBLANE):
            for k in range(TOP_K):
                row = idx_ref[0, k * TB + jj * SUBLANE + u]
                if scatter:
                    cp = pltpu.make_async_copy(src_ref.at[jj, pl.ds(u, 1)], dst_ref.at[pl.ds(row, 1)], sem)
                else:
                    cp = pltpu.make_async_copy(src_ref.at[pl.ds(row, 1)], dst_ref.at[k, jj, pl.ds(u, 1)], sem)
                if wait:
                    cp.wait()
                else:
                    cp.start(priority=k % 2)

    def start_body(jj, c):
        issue(jj, False)
        return c

    def wait_body(jj, c):
        issue(jj, True)
        return c

    lax.fori_loop(0, TB // SUBLANE, start_body, 0)
    if wait_here:
        lax.fori_loop(0, TB // SUBLANE, wait_body, 0)


def _dispatch_kernel(pad_ref, dest_ref, f_ref, xs_ref, zbuf, fbuf, sem, zsem, fsem, *, n_fills, n_tiles):
    i = pl.program_id(0)
    tile_groups = TB // SUBLANE

    def stage(tile):
        s = lax.rem(tile, 3)
        return pltpu.make_async_copy(f_ref.at[pl.ds(tile * tile_groups, tile_groups)], fbuf.at[s], fsem.at[s])

    def wait_rows(tile):
        s = lax.rem(tile, 3)
        for _ in range(TOP_K):
            pltpu.make_async_copy(fbuf.at[s], fbuf.at[s], sem.at[s]).wait()

    @pl.when(i == 0)
    def _():
        stage(i).start()
        zbuf[...] = jnp.zeros_like(zbuf)

        def fill(e):
            return pltpu.make_async_copy(zbuf, xs_ref.at[pl.ds(pl.multiple_of(pad_ref[e] * TM, TM), TM)], zsem)

        for e in range(n_fills):
            @pl.when(pad_ref[e] >= 0)
            def _():
                fill(e).start()
        for e in range(n_fills):
            @pl.when(pad_ref[e] >= 0)
            def _():
                fill(e).wait()

    stage(i).wait()

    @pl.when(i + 1 < n_tiles)
    def _():
        stage(i + 1).start()

    slot = lax.rem(i, 3)
    _row_copies(fbuf.at[slot], xs_ref, dest_ref, sem.at[slot], scatter=True, wait_here=False)

    @pl.when(i > 0)
    def _():
        wait_rows(i - 1)

    @pl.when(i == n_tiles - 1)
    def _():
        wait_rows(i)


def _dispatch(f, dest, pad_at, *, n_tiles, n_rows):
    n_fills = pad_at.shape[0]
    n, d = f.shape
    grid_spec = pltpu.PrefetchScalarGridSpec(
        num_scalar_prefetch=1,
        grid=(n_tiles,),
        in_specs=[pl.BlockSpec((None, 1, TOP_K * TB), lambda i, pad: (i, 0, 0), memory_space=pltpu.SMEM),
                  pl.BlockSpec(memory_space=pl.ANY)],
        out_specs=pl.BlockSpec(memory_space=pl.ANY),
        scratch_shapes=[pltpu.VMEM((TM, d), F32), pltpu.VMEM((3, TB // SUBLANE, SUBLANE, d), F32),
                        pltpu.SemaphoreType.DMA((3,)), pltpu.SemaphoreType.DMA(()), pltpu.SemaphoreType.DMA((3,))],
    )
    return pl.pallas_call(
        functools.partial(_dispatch_kernel, n_fills=n_fills, n_tiles=n_tiles),
        grid_spec=grid_spec,
        out_shape=jax.ShapeDtypeStruct((n_rows, d), F32),
        compiler_params=_params(1),
        name="moe_dispatch",
    )(pad_at, dest, f.reshape(n // SUBLANE, SUBLANE, d))


def _experts_kernel(te_ref, tv_ref, nu_ref, x_ref, wg_ref, wu_ref, wd_ref, o_ref, xb_ref):
    i = pl.program_id(0)
    j = pl.program_id(1)
    used = i < nu_ref[0]
    valid = tv_ref[i]

    @pl.when(j == 0)
    def _():
        o_ref[...] = jnp.zeros_like(o_ref)

    def compute(n_rows):
        @pl.when(j == 0)
        def _():
            xb_ref[0:n_rows, :] = x_ref[0:n_rows, :].astype(BF16)

        xb = xb_ref[0:n_rows, :]
        act = _silu(_dot(xb, wg_ref[...].astype(BF16))) * _dot(xb, wu_ref[...].astype(BF16))
        o_ref[0:n_rows, :] += _dot(act.astype(BF16), wd_ref[...].astype(BF16))

    for n_rows in range(SUB, TM + 1, SUB):
        @pl.when(used & (valid > n_rows - SUB) & (valid <= n_rows))
        def _():
            compute(n_rows)


def _experts(xs, tile_expert, tile_valid, n_used, layer, we_gate, we_up, we_down):
    p, d = xs.shape
    dfe = we_gate.shape[-1]
    assert dfe % FC == 0 and p % TM == 0 and TM % SUB == 0
    n_chunks = dfe // FC
    n_tiles = p // TM

    def tile(i, nu):
        return jnp.minimum(i, nu[0] - 1)

    def chunk(i, j, nu):
        return jnp.where(i < nu[0], j, n_chunks - 1)

    grid_spec = pltpu.PrefetchScalarGridSpec(
        num_scalar_prefetch=3,
        grid=(n_tiles, n_chunks),
        in_specs=[pl.BlockSpec((TM, d), lambda i, j, te, tv, nu: (tile(i, nu), 0)),
                  pl.BlockSpec((None, None, d, FC),
                               lambda i, j, te, tv, nu: (layer, te[tile(i, nu)], 0, chunk(i, j, nu))),
                  pl.BlockSpec((None, None, d, FC),
                               lambda i, j, te, tv, nu: (layer, te[tile(i, nu)], 0, chunk(i, j, nu))),
                  pl.BlockSpec((None, None, FC, d),
                               lambda i, j, te, tv, nu: (layer, te[tile(i, nu)], chunk(i, j, nu), 0))],
        out_specs=pl.BlockSpec((TM, d), lambda i, j, te, tv, nu: (i, 0)),
        scratch_shapes=[pltpu.VMEM((TM, d), BF16)],
    )
    return pl.pallas_call(
        _experts_kernel,
        grid_spec=grid_spec,
        out_shape=jax.ShapeDtypeStruct((p, d), F32),
        compiler_params=_params(2),
        name="expert_swiglu",
    )(tile_expert, tile_valid, n_used, xs, we_gate, we_up, we_down)


def _combine_kernel(dest_ref, ys_ref, info_ref, h_ref, mod_ref, gf_ref, o_ref, ybuf, sem, *, final_norm, n_tiles):
    i = pl.program_id(0)
    slot = lax.rem(i, 2)

    @pl.when(i < n_tiles)
    def _():
        _row_copies(ys_ref, ybuf.at[slot], dest_ref, sem.at[slot], scatter=False, wait_here=False)

    @pl.when(i > 0)
    def _():
        done = ybuf.at[1 - slot]
        pltpu.make_async_copy(done, done, sem.at[1 - slot]).wait()
        info = info_ref[...]
        d = h_ref.shape[1]
        mix = info[:, 4:5] * done[0].reshape(TB, d) + info[:, 5:6] * done[1].reshape(TB, d)
        h = h_ref[...] + mod_ref[...][5:6] * mix
        if final_norm:
            h = (h * lax.rsqrt(jnp.mean(h * h, axis=-1, keepdims=True) + EPS)) * gf_ref[...]
        o_ref[...] = h


def _combine(ys, dest, info, h, mod, g_final, *, n_tiles, tpl, nb, final_norm):
    n, d = h.shape
    prev = lambda i: (jnp.maximum(i - 1, 0), 0)
    return pl.pallas_call(
        functools.partial(_combine_kernel, final_norm=final_norm, n_tiles=n_tiles),
        grid=(n_tiles + 1,),
        in_specs=[pl.BlockSpec((None, 1, TOP_K * TB), lambda i: (jnp.minimum(i, n_tiles - 1), 0, 0),
                               memory_space=pltpu.SMEM),
                  pl.BlockSpec(memory_space=pl.ANY),
                  pl.BlockSpec((TB, LANE), prev),
                  pl.BlockSpec((TB, d), prev),
                  pl.BlockSpec((None, N_MOD, d), lambda i: (jnp.minimum(jnp.maximum(i - 1, 0) // tpl, nb), 0, 0)),
                  pl.BlockSpec((1, d), lambda i: (0, 0))],
        out_specs=pl.BlockSpec((TB, d), prev),
        out_shape=jax.ShapeDtypeStruct((n_tiles * TB, d), F32),
        scratch_shapes=[pltpu.VMEM((2, TOP_K, TB // SUBLANE, SUBLANE, d), F32), pltpu.SemaphoreType.DMA((2,))],
        compiler_params=_params(1),
        name="moe_combine",
    )(dest, ys, info, h, mod, g_final.reshape(1, d))


def _moe(f, info, rt, counts, h, mod, g_final, layer, we_gate, we_up, we_down, *, n_tiles, tpl, nb, final_norm):
    n = f.shape[0]
    n_experts = we_gate.shape[1]
    n_xtiles = -(-TOP_K * n // TM) + n_experts
    n_rows = n_xtiles * TM

    cnt = counts[0, :n_experts].astype(I32)
    tiles_e = (cnt + TM - 1) // TM
    tile_end = jnp.cumsum(tiles_e)
    starts = (tile_end - tiles_e) * TM
    n_used = tile_end[-1:]
    tile_ids = jnp.arange(n_xtiles, dtype=I32)
    tile_expert = jnp.minimum(jnp.sum(tile_ids[:, None] >= tile_end[None, :], axis=1), n_experts - 1).astype(I32)
    tile_valid = jnp.clip(cnt[tile_expert] - (tile_ids * TM - starts[tile_expert]), 0, TM).astype(I32)
    choice = rt[:, 0:TOP_K, :]
    rank = rt[:, TOP_K:2 * TOP_K, :]
    start_of = sum(jnp.where(choice == e, starts[e], 0) for e in range(n_experts))
    dest = (start_of + rank).reshape(n_tiles, 1, TOP_K * TB)
    tail = n_used + jnp.arange(n_experts, dtype=I32)
    pad_at = jnp.concatenate([jnp.where(tiles_e > 0, tile_end - 1, -1),
                              jnp.where(tail < n_xtiles, tail, -1)]).astype(I32)

    xs = _dispatch(f, dest, pad_at, n_tiles=n_tiles, n_rows=n_rows)
    ys = _experts(xs, tile_expert, tile_valid, n_used, layer, we_gate, we_up, we_down)
    return _combine(ys, dest, info, h, mod, g_final, n_tiles=n_tiles, tpl=tpl, nb=nb, final_norm=final_norm)


def kernel(x, c, ctx, c_ctx, w_mod, b_mod, g_mix, g_ffn, g_final, w_in_ab, conv_w, pool_w, pool_scale,
           w_out_ab, w_ff_gate, w_ff_up, w_ff_down, w_qkv, rpb, w_out_na, w_router, we_gate, we_up, we_down):
    nb, seq_l, d = x.shape
    seq_c = ctx.shape[1]
    depth = w_mod.shape[0]
    heads = rpb.shape[1]
    na_rows = (rpb.shape[2] + 1) // 2
    na_cols = (rpb.shape[3] + 1) // 2
    assert seq_l % TB == 0 and seq_c % TB == 0 and nb + 1 <= SUBLANE and depth % 2 == 0
    tpl = seq_l // TB
    nlat = nb * tpl
    nall = nlat + nb * seq_c // TB

    cond = jnp.zeros((SUBLANE, d), F32).at[:nb].set(c).at[nb].set(c_ctx)
    mods = _adaln(cond, w_mod, b_mod).reshape(depth, SUBLANE, N_MOD, d)

    bf = lambda a: a.astype(BF16)
    h = out = None
    for i in range(depth):
        j = i // 2
        last = i == depth - 1
        mod = mods[i]
        if i % 2 == 0:
            srcs = (x.reshape(nb * seq_l, d), ctx.reshape(nb * seq_c, d), 0) if i == 0 else (h, h, nlat)
            h = _mixer(*srcs, mod, g_mix[i], j, w_in_ab, conv_w, pool_w, pool_scale[j], w_out_ab,
                       n_tiles=nall, nlat=nlat, nb=nb, seq_l=seq_l, seq_c=seq_c)
            h = _ffn(h, mod, g_ffn[i], bf(w_ff_gate[j]), bf(w_ff_up[j]), bf(w_ff_down[j]),
                     n_rows=nall * TB, seq_l=seq_l, nb=nb)
        else:
            n_tiles = nlat if last else nall
            q, k, v = _qkv(h, mod, g_mix[i], j, w_qkv, n_rows=nall * TB, seq_l=seq_l, nb=nb,
                           qscale=(d // heads) ** -0.5)
            o_lat, o_ctx = _natten(q, k, v, _bias_table(rpb[j], na_cols), nb=nb, seq_l=seq_l, seq_c=seq_c,
                                   heads=heads, na_rows=na_rows)
            h, f, info, rt, counts = _route(o_lat, o_ctx, h, mod, g_ffn[i], j, w_out_na, w_router[j],
                                            n_rows=n_tiles * TB, seq_l=seq_l, nb=nb)
            h = _moe(f, info, rt, counts, h, mod, g_final, j, we_gate, we_up, we_down,
                     n_tiles=n_tiles, tpl=tpl, nb=nb, final_norm=last)
            if last:
                out = h
    return out.reshape(nb, seq_l, d)
```

```python
import functools

import numpy as np
import jax
import jax.numpy as jnp
from jax import lax
from jax.experimental import pallas as pl
from jax.experimental.pallas import tpu as pltpu

F32 = jnp.float32
BF16 = jnp.bfloat16
I32 = jnp.int32

GRID_W = 64
POOL_WINDOWS = (2, 4, 8, 16)
N_MOD = 6
TOP_K = 2
EPS = 1e-6
NEG = -1e30

LANE = 128
SUBLANE = 8
TB = 256
TBD = 512
HALO = SUBLANE
TM = 1024
SUB = 256
FC = 512
NA_GROUP = 4
ADALN_COLS = 1536
FFN_COLS = 768
VMEM_LIMIT = 56 * 1024 * 1024


def _params(n_axes):
    return pltpu.CompilerParams(dimension_semantics=("arbitrary",) * n_axes,
                                vmem_limit_bytes=VMEM_LIMIT)


def _rms_mod(x, g, shift, scale):
    y = x * lax.rsqrt(jnp.mean(x * x, axis=-1, keepdims=True) + EPS)
    return (y * g) * (1.0 + scale) + shift


def _silu(x):
    return x * jax.nn.sigmoid(x)


def _dot(a, b):
    return jnp.dot(a, b, preferred_element_type=F32)


def _dot_nt(a, b):
    return lax.dot_general(a, b, (((1,), (1,)), ((), ())), preferred_element_type=F32)


def _adaln_kernel(cond_ref, w_ref, b_ref, o_ref):
    s = _silu(cond_ref[...]).astype(BF16)
    o_ref[...] = _dot(s, w_ref[...].astype(BF16)) + b_ref[...]


def _adaln(cond, w_mod, b_mod):
    depth, d, nd = w_mod.shape
    tn = ADALN_COLS
    assert nd % tn == 0
    return pl.pallas_call(
        _adaln_kernel,
        grid=(depth, nd // tn),
        in_specs=[pl.BlockSpec((SUBLANE, d), lambda l, n: (0, 0)),
                  pl.BlockSpec((None, d, tn), lambda l, n: (l, 0, n)),
                  pl.BlockSpec((None, 1, tn), lambda l, n: (l, 0, n))],
        out_specs=pl.BlockSpec((None, SUBLANE, tn), lambda l, n: (l, 0, n)),
        out_shape=jax.ShapeDtypeStruct((depth, SUBLANE, nd), F32),
        compiler_params=_params(2),
        name="adaln",
    )(cond, w_mod, b_mod.reshape(depth, 1, nd))


def _mixer_kernel(ap_ref, ac_ref, an_ref, bp_ref, bc_ref, bn_ref, mod_ref, g_ref, win_ref, cw_ref, pw_ref, ps_ref,
                  wout_ref, o_ref, xs_ref, z_ref, p_ref, winb_ref, woutb_ref,
                  *, nlat, tpl, tpc, seq_l, seq_c, d_conv, pool_group):
    i = pl.program_id(0)
    is_ctx = i >= nlat
    pos = jnp.where(is_ctx, lax.rem(i - nlat, tpc), lax.rem(i, tpl)) * TB
    seq_len = jnp.where(is_ctx, seq_c, seq_l)
    ext = TB + 2 * HALO

    @pl.when(i == 0)
    def _():
        winb_ref[...] = win_ref[...].astype(BF16)
        woutb_ref[...] = wout_ref[...].astype(BF16)

    hc = jnp.where(is_ctx, bc_ref[...], ac_ref[...])
    xs_ref[0:HALO, :] = jnp.where(is_ctx, bp_ref[...], ap_ref[...])
    xs_ref[HALO:HALO + TB, :] = hc
    xs_ref[HALO + TB:ext, :] = jnp.where(is_ctx, bn_ref[...], an_ref[...])
    mod = mod_ref[...]
    a = _rms_mod(xs_ref[...], g_ref[...], mod[0:1], mod[1:2]).astype(BF16)
    u = _dot(a, winb_ref[...])
    srow = lax.broadcasted_iota(I32, (ext, 1), 0) + (pos - HALO)
    u = jnp.where((srow >= 0) & (srow < seq_len), u, 0.0)

    z_ref[...] = u[:, d_conv:2 * d_conv] * u[:, 2 * d_conv:3 * d_conv]
    p_ref[...] = u[:, 3 * d_conv:]
    cw = cw_ref[...]
    conv = (z_ref[HALO - 1:HALO - 1 + TB, :] * cw[0:1] + z_ref[HALO:HALO + TB, :] * cw[1:2]
            + z_ref[HALO + 1:HALO + 1 + TB, :] * cw[2:3])
    pieces = [u[HALO:HALO + TB, 0:d_conv] * conv]

    spos = srow[HALO:HALO + TB]
    ps = ps_ref[...]
    for g, win in enumerate(POOL_WINDOWS):
        lo, hi = win // 2, win - 1 - win // 2
        cols = slice(g * pool_group, (g + 1) * pool_group)
        acc = p_ref[HALO - lo:HALO - lo + TB, cols]
        for dlt in range(-lo + 1, hi + 1):
            acc = acc + p_ref[HALO + dlt:HALO + dlt + TB, cols]
        cnt = jnp.minimum(spos + hi, seq_len - 1) - jnp.maximum(spos - lo, 0) + 1
        diff = acc / cnt.astype(F32) - p_ref[HALO:HALO + TB, cols]
        pieces.append(_dot(diff.astype(BF16), pw_ref[g].astype(BF16)) * ps[:, cols])
    cat = jnp.concatenate(pieces, axis=-1).astype(BF16)
    o_ref[...] = hc + mod[2:3] * _dot(cat, woutb_ref[...])


def _mixer(src_a, src_b, b_off, mod, g, layer, w_in, conv_w, pool_w, pool_scale, w_out,
           *, n_tiles, nlat, nb, seq_l, seq_c):
    d = src_a.shape[1]
    d_conv = conv_w.shape[-1]
    d_pool = pool_scale.shape[-1]
    pool_group = pool_w.shape[-1]
    assert pool_group % LANE == 0 and d_conv % LANE == 0 and len(POOL_WINDOWS) == pool_w.shape[1]
    tpl, tpc = seq_l // TB, seq_c // TB
    ext = TB + 2 * HALO
    hb = TB // HALO
    a_last = src_a.shape[0] // HALO - 1
    b_last = src_b.shape[0] // HALO - 1
    kern = functools.partial(_mixer_kernel, nlat=nlat, tpl=tpl, tpc=tpc, seq_l=seq_l, seq_c=seq_c,
                             d_conv=d_conv, pool_group=pool_group)
    a_tile = lambda i: jnp.minimum(i, nlat - 1)
    b_tile = lambda i: jnp.maximum(i - nlat, 0) + b_off
    once = pl.Buffered(1)
    return pl.pallas_call(
        kern,
        grid=(n_tiles,),
        in_specs=[pl.BlockSpec((HALO, d), lambda i: (jnp.maximum(a_tile(i) * hb - 1, 0), 0)),
                  pl.BlockSpec((TB, d), lambda i: (a_tile(i), 0)),
                  pl.BlockSpec((HALO, d), lambda i: (jnp.minimum((a_tile(i) + 1) * hb, a_last), 0)),
                  pl.BlockSpec((HALO, d), lambda i: (jnp.maximum(b_tile(i) * hb - 1, 0), 0)),
                  pl.BlockSpec((TB, d), lambda i: (b_tile(i), 0)),
                  pl.BlockSpec((HALO, d), lambda i: (jnp.minimum((b_tile(i) + 1) * hb, b_last), 0)),
                  pl.BlockSpec((None, N_MOD, d), lambda i: (jnp.minimum(i // tpl, nb), 0, 0)),
                  pl.BlockSpec((1, d), lambda i: (0, 0)),
                  pl.BlockSpec((None,) + w_in.shape[1:], lambda i: (layer, 0, 0), pipeline_mode=once),
                  pl.BlockSpec((None,) + conv_w.shape[1:], lambda i: (layer, 0, 0)),
                  pl.BlockSpec((None,) + pool_w.shape[1:], lambda i: (layer, 0, 0, 0)),
                  pl.BlockSpec((1, d_pool), lambda i: (0, 0)),
                  pl.BlockSpec((None,) + w_out.shape[1:], lambda i: (layer, 0, 0), pipeline_mode=once)],
        out_specs=pl.BlockSpec((TB, d), lambda i: (i, 0)),
        out_shape=jax.ShapeDtypeStruct((n_tiles * TB, d), F32),
        scratch_shapes=[pltpu.VMEM((ext, d), F32), pltpu.VMEM((ext, d_conv), F32), pltpu.VMEM((ext, d_pool), F32),
                        pltpu.VMEM(w_in.shape[1:], BF16), pltpu.VMEM(w_out.shape[1:], BF16)],
        compiler_params=_params(1),
        name="conv_pool_mixer",
    )(src_a, src_a, src_a, src_b, src_b, src_b, mod, g.reshape(1, d), w_in, conv_w, pool_w,
      pool_scale.reshape(1, d_pool), w_out)


def _ffn_kernel(h_ref, mod_ref, g_ref, wg_ref, wu_ref, wd_ref, o_ref, act_ref, *, chunks):
    mod = mod_ref[...]
    h = h_ref[...]
    f = _rms_mod(h, g_ref[...], mod[3:4], mod[4:5]).astype(BF16)
    for c0, c1 in chunks:
        gate = _dot(f, wg_ref[:, c0:c1])
        up = _dot(f, wu_ref[:, c0:c1])
        act_ref[:, c0:c1] = (_silu(gate) * up).astype(BF16)
    o_ref[...] = h + mod[5:6] * _dot(act_ref[...], wd_ref[...])


def _ffn(h, mod, g, wg, wu, wd, *, n_rows, seq_l, nb):
    t, d = h.shape
    dff = wg.shape[-1]
    step = FFN_COLS
    chunks = tuple((c, min(c + step, dff)) for c in range(0, dff, step))
    const = lambda i: (0, 0)
    once = pl.Buffered(1)
    assert n_rows % TBD == 0 and seq_l % TBD == 0
    tpl = seq_l // TBD
    return pl.pallas_call(
        functools.partial(_ffn_kernel, chunks=chunks),
        grid=(n_rows // TBD,),
        in_specs=[pl.BlockSpec((TBD, d), lambda i: (i, 0)),
                  pl.BlockSpec((None, N_MOD, d), lambda i: (jnp.minimum(i // tpl, nb), 0, 0)),
                  pl.BlockSpec((1, d), const),
                  pl.BlockSpec(wg.shape, const, pipeline_mode=once),
                  pl.BlockSpec(wu.shape, const, pipeline_mode=once),
                  pl.BlockSpec(wd.shape, const, pipeline_mode=once)],
        out_specs=pl.BlockSpec((TBD, d), lambda i: (i, 0)),
        out_shape=jax.ShapeDtypeStruct((n_rows, d), F32),
        scratch_shapes=[pltpu.VMEM((TBD, dff), BF16)],
        compiler_params=_params(1),
        name="dense_swiglu",
    )(h, mod, g.reshape(1, d), wg, wu, wd)


def _qkv_kernel(h_ref, mod_ref, g_ref, w_ref, q_ref, k_ref, v_ref, wb_ref, *, d, qscale):
    @pl.when(pl.program_id(0) == 0)
    def _():
        wb_ref[...] = w_ref[...].astype(BF16)

    mod = mod_ref[...]
    a = _rms_mod(h_ref[...], g_ref[...], mod[0:1], mod[1:2]).astype(BF16)
    q_ref[...] = (_dot(a, wb_ref[:, 0:d]) * qscale).astype(BF16)
    k_ref[...] = _dot(a, wb_ref[:, d:2 * d]).astype(BF16)
    v_ref[...] = _dot(a, wb_ref[:, 2 * d:3 * d]).astype(BF16)


def _qkv(h, mod, g, layer, w, *, n_rows, seq_l, nb, qscale):
    t, d = h.shape
    const = lambda i: (0, 0)
    out = jax.ShapeDtypeStruct((n_rows, d), BF16)
    assert n_rows % TBD == 0 and seq_l % TBD == 0
    tpl = seq_l // TBD
    return pl.pallas_call(
        functools.partial(_qkv_kernel, d=d, qscale=qscale),
        grid=(n_rows // TBD,),
        in_specs=[pl.BlockSpec((TBD, d), lambda i: (i, 0)),
                  pl.BlockSpec((None, N_MOD, d), lambda i: (jnp.minimum(i // tpl, nb), 0, 0)),
                  pl.BlockSpec((1, d), const),
                  pl.BlockSpec((None,) + w.shape[1:], lambda i: (layer, 0, 0), pipeline_mode=pl.Buffered(1))],
        out_specs=[pl.BlockSpec((TBD, d), lambda i: (i, 0))] * 3,
        out_shape=[out, out, out],
        scratch_shapes=[pltpu.VMEM(w.shape[1:], BF16)],
        compiler_params=_params(1),
        name="qkv_proj",
    )(h, mod, g.reshape(1, d), w)


def _softmax_pv(s_parts, v_parts):
    m = s_parts[0].max(axis=-1, keepdims=True)
    for s in s_parts[1:]:
        m = jnp.maximum(m, s.max(axis=-1, keepdims=True))
    den = 0.0
    out = 0.0
    for s, v in zip(s_parts, v_parts):
        p = jnp.exp(s - m)
        den = den + p.sum(axis=-1, keepdims=True)
        out = out + _dot(p.astype(BF16), v)
    return out / den


def _natten_kernel(q_ref, k_ref, v_ref, kc_ref, vc_ref, qc_ref, bias_ref, o_ref, oc_ref,
                   s_scr, p_scr, den_scr, *, rows, na_rows, head_dim):
    w = GRID_W
    band = na_rows * w
    lane = lax.broadcasted_iota(I32, (1, LANE), 1)
    head0 = lane < head_dim
    zero = jnp.zeros((), BF16)
    kc = kc_ref[...]
    vc = vc_ref[...]

    def stack(q):
        return jnp.concatenate([jnp.where(head0, q, zero), jnp.where(head0, zero, q)], axis=0)

    def unstack(o, n):
        return jnp.where(head0, o[0:n], o[n:2 * n])

    def offsets(r):
        start = jnp.clip(r - na_rows // 2, 0, rows - na_rows)
        return pl.multiple_of(r * w, w), pl.multiple_of(start * w, w), start - r + (na_rows - 1)

    def scores(r, slot):
        q0, k0, d0 = offsets(r)
        qq = stack(q_ref[pl.ds(q0, w), :])
        bias = jnp.concatenate([bias_ref[d0 + 2 * p] for p in range(band // LANE)], axis=-1)
        s_scr[slot, :, 0:band] = _dot_nt(qq, k_ref[pl.ds(k0, band), :]) + bias
        s_scr[slot, :, band:] = _dot_nt(qq, kc)

    def softmax(slot):
        s = s_scr[slot]
        p = jnp.exp(s - s.max(axis=-1, keepdims=True))
        den_scr[slot] = p.sum(axis=-1, keepdims=True)
        p_scr[slot] = p.astype(BF16)

    def values(r, slot):
        q0, k0, _ = offsets(r)
        o = _dot(p_scr[slot, :, 0:band], v_ref[pl.ds(k0, band), :]) + _dot(p_scr[slot, :, band:], vc)
        o_ref[pl.ds(q0, w), :] = unstack(o / den_scr[slot], w).astype(o_ref.dtype)

    n_groups = rows // NA_GROUP
    assert rows % NA_GROUP == 0 and n_groups % 2 == 0 and n_groups >= 4

    def step(u, parity, do_scores, do_softmax, do_values):
        for g in range(NA_GROUP):
            if do_softmax:
                softmax((1 - parity) * NA_GROUP + g)
        for g in range(NA_GROUP):
            if do_scores:
                scores(u * NA_GROUP + g, parity * NA_GROUP + g)
            if do_values:
                values((u - 2) * NA_GROUP + g, parity * NA_GROUP + g)

    step(0, 0, True, False, False)
    step(1, 1, True, True, False)

    def group_pair_body(u2, carry):
        step(2 * u2, 0, True, True, True)
        step(2 * u2 + 1, 1, True, True, True)
        return carry

    lax.fori_loop(1, n_groups // 2, group_pair_body, 0)
    step(n_groups, 0, False, True, True)
    step(n_groups + 1, 1, False, False, True)

    qc = qc_ref[...]
    nc = qc.shape[0]
    oc = _softmax_pv([_dot_nt(stack(qc), kc)], [vc])
    oc_ref[...] = unstack(oc, nc).astype(oc_ref.dtype)


def _bias_table(rpb, na_cols):
    heads, nr, nc = rpb.shape
    w = GRID_W
    jcol = np.arange(w)
    cstart = np.clip(jcol - na_cols // 2, 0, w - na_cols)
    kcol = np.arange(w)
    inside = (kcol[None, :] >= cstart[:, None]) & (kcol[None, :] < cstart[:, None] + na_cols)
    dc = kcol[None, :] - jcol[:, None] + (na_cols - 1)
    pick = jnp.asarray((dc[None] == np.arange(nc)[:, None, None]) & inside[None], F32)
    t2 = jnp.einsum("hdm,mqk->hdqk", rpb, pick, precision=lax.Precision.HIGHEST)
    t2 = jnp.where(inside[None, None], t2, NEG)
    t3 = jnp.concatenate([t2[:, :-1], t2[:, 1:]], axis=-1)
    t3 = t3.reshape(heads // 2, 2, nr - 1, w, 2 * w).transpose(0, 2, 1, 3, 4)
    return t3.reshape(heads // 2, nr - 1, 2 * w, 2 * w).astype(F32)


def _natten(q, k, v, bias, *, nb, seq_l, seq_c, heads, na_rows):
    t, d = q.shape
    head_dim = d // heads
    assert 2 * head_dim == LANE and 2 * GRID_W == LANE and seq_l % GRID_W == 0
    rows = seq_l // GRID_W
    assert (nb * seq_l) % seq_c == 0
    cblk = nb * seq_l // seq_c
    lat = lambda b, hp: (b, hp)
    ctx = lambda b, hp: (cblk + b, hp)
    return pl.pallas_call(
        functools.partial(_natten_kernel, rows=rows, na_rows=na_rows, head_dim=head_dim),
        grid=(nb, heads // 2),
        in_specs=[pl.BlockSpec((seq_l, LANE), lat),
                  pl.BlockSpec((seq_l, LANE), lat),
                  pl.BlockSpec((seq_l, LANE), lat),
                  pl.BlockSpec((seq_c, LANE), ctx),
                  pl.BlockSpec((seq_c, LANE), ctx),
                  pl.BlockSpec((seq_c, LANE), ctx),
                  pl.BlockSpec((None,) + bias.shape[1:], lambda b, hp: (hp, 0, 0, 0))],
        out_specs=[pl.BlockSpec((seq_l, LANE), lat),
                   pl.BlockSpec((seq_c, LANE), lambda b, hp: (b, hp))],
        out_shape=[jax.ShapeDtypeStruct((nb * seq_l, d), BF16),
                   jax.ShapeDtypeStruct((nb * seq_c, d), BF16)],
        scratch_shapes=[pltpu.VMEM((2 * NA_GROUP, 2 * GRID_W, na_rows * GRID_W + seq_c), F32),
                        pltpu.VMEM((2 * NA_GROUP, 2 * GRID_W, na_rows * GRID_W + seq_c), BF16),
                        pltpu.VMEM((2 * NA_GROUP, 2 * GRID_W, 1), F32)],
        compiler_params=_params(2),
        name="neighbourhood_attention",
    )(q, k, v, k, v, q, bias)


def _split_bf16(x):
    hi = x.astype(BF16)
    return hi, (x - hi.astype(F32)).astype(BF16)


def _route_kernel(al_ref, ac_ref, h_ref, mod_ref, g_ref, wo_ref, wr_ref, h_out_ref, f_ref, info_ref, rt_ref,
                  cnt_ref, wob_ref, *, n_experts, nlat):
    i = pl.program_id(0)

    @pl.when(i == 0)
    def _():
        cnt_ref[...] = jnp.zeros_like(cnt_ref)
        wob_ref[...] = wo_ref[...].astype(BF16)

    mod = mod_ref[...]
    attn = jnp.where(i < nlat, al_ref[...], ac_ref[...])
    h = h_ref[...] + mod[2:3] * _dot(attn, wob_ref[...])
    h_out_ref[...] = h
    f = _rms_mod(h, g_ref[...], mod[3:4], mod[4:5])
    f_ref[...] = f

    f_hi, f_lo = _split_bf16(f)
    z_hi = _dot(f_hi, wr_ref[...])
    z_lo = _dot(f_lo, wr_ref[...])
    logits = z_hi[:, 0:LANE] + (z_hi[:, LANE:] + z_lo[:, 0:LANE])
    lane_i = lax.broadcasted_iota(I32, logits.shape, 1)
    lane = lane_i.astype(F32)
    logits = jnp.where(lane_i < n_experts, logits, -jnp.inf)
    v1 = logits.max(axis=-1, keepdims=True)
    i1 = jnp.where(logits == v1, lane, float(LANE)).min(axis=-1, keepdims=True)
    rest = jnp.where(lane == i1, -jnp.inf, logits)
    v2 = rest.max(axis=-1, keepdims=True)
    i2 = jnp.where(rest == v2, lane, float(LANE)).min(axis=-1, keepdims=True)
    e2 = jnp.exp(v2 - v1)
    g1 = 1.0 / (1.0 + e2)
    g2 = e2 * g1

    sel1 = lane == i1
    sel2 = lane == i2
    onehot = jnp.where(sel1 | sel2, 1.0, 0.0)
    tr = lax.broadcasted_iota(I32, (TB, TB), 0)
    tc = lax.broadcasted_iota(I32, (TB, TB), 1)
    before = _dot(jnp.where(tc < tr, 1.0, 0.0).astype(BF16), onehot.astype(BF16)) + cnt_ref[0:1, :]
    r1 = jnp.where(sel1, before, 0.0).sum(axis=-1, keepdims=True)
    r2 = jnp.where(sel2, before, 0.0).sum(axis=-1, keepdims=True)
    cnt_ref[...] = cnt_ref[...] + onehot.sum(axis=0, keepdims=True)

    info = jnp.where(lane_i == 0, i1, 0.0)
    info = jnp.where(lane_i == 1, i2, info)
    info = jnp.where(lane_i == 2, r1, info)
    info = jnp.where(lane_i == 3, r2, info)
    info = jnp.where(lane_i == 4, g1, info)
    info = jnp.where(lane_i == 5, g2, info)
    info_ref[...] = info
    rt_ref[...] = jnp.transpose(info)[0:SUBLANE, :].astype(I32)


def _route(a_lat, a_ctx, h, mod, g, layer, w_out, w_router, *, n_rows, seq_l, nb):
    t, d = h.shape
    n_experts = w_router.shape[-1]
    wr = jnp.concatenate(_split_bf16(jnp.zeros((d, LANE), F32).at[:, :n_experts].set(w_router)), axis=1)
    const = lambda i: (0, 0)
    row = lambda i: (i, 0)
    n = n_rows
    assert n_rows % TB == 0 and seq_l % TB == 0 and a_ctx.shape[0] % TB == 0
    n_tiles = n_rows // TB
    tpl = seq_l // TB
    nlat = nb * tpl
    return pl.pallas_call(
        functools.partial(_route_kernel, n_experts=n_experts, nlat=nlat),
        grid=(n_tiles,),
        in_specs=[pl.BlockSpec((TB, d), lambda i: (jnp.minimum(i, nlat - 1), 0)),
                  pl.BlockSpec((TB, d), lambda i: (jnp.maximum(i - nlat, 0), 0)),
                  pl.BlockSpec((TB, d), row),
                  pl.BlockSpec((None, N_MOD, d), lambda i: (jnp.minimum(i // tpl, nb), 0, 0)),
                  pl.BlockSpec((1, d), const),
                  pl.BlockSpec((None,) + w_out.shape[1:], lambda i: (layer, 0, 0), pipeline_mode=pl.Buffered(1)),
                  pl.BlockSpec(wr.shape, const)],
        out_specs=[pl.BlockSpec((TB, d), row),
                   pl.BlockSpec((TB, d), row),
                   pl.BlockSpec((TB, LANE), row),
                   pl.BlockSpec((None, SUBLANE, TB), lambda i: (i, 0, 0)),
                   pl.BlockSpec((SUBLANE, LANE), const)],
        out_shape=[jax.ShapeDtypeStruct((n, d), F32),
                   jax.ShapeDtypeStruct((n, d), F32),
                   jax.ShapeDtypeStruct((n, LANE), F32),
                   jax.ShapeDtypeStruct((n_tiles, SUBLANE, TB), I32),
                   jax.ShapeDtypeStruct((SUBLANE, LANE), F32)],
        scratch_shapes=[pltpu.VMEM(w_out.shape[1:], BF16)],
        compiler_params=_params(1),
        name="attn_out_router",
    )(a_lat, a_ctx, h, mod, g.reshape(1, d), w_out, wr)


def _row_copies(src_ref, dst_ref, idx_ref, sem, scatter, wait_here=True):
    def issue(jj, wait):
        for u in range(SUBLANE):
            for k in range(TOP_K):
                row = idx_ref[0, k * TB + jj * SUBLANE + u]
                if scatter:
                    cp = pltpu.make_async_copy(src_ref.at[jj, pl.ds(u, 1)], dst_ref.at[pl.ds(row, 1)], sem)
                else:
                    cp = pltpu.make_async_copy(src_ref.at[pl.ds(row, 1)], dst_ref.at[k, jj, pl.ds(u, 1)], sem)
                if wait:
                    cp.wait()
                else:
                    cp.start(priority=k % 2)

    def start_body(jj, c):
        issue(jj, False)
        return c

    def wait_body(jj, c):
        issue(jj, True)
        return c

    lax.fori_loop(0, TB // SUBLANE, start_body, 0)
    if wait_here:
        lax.fori_loop(0, TB // SUBLANE, wait_body, 0)


def _dispatch_kernel(pad_ref, dest_ref, f_ref, xs_ref, zbuf, fbuf, sem, zsem, fsem, *, n_fills, n_tiles):
    i = pl.program_id(0)
    tile_groups = TB // SUBLANE

    def stage(tile):
        s = lax.rem(tile, 3)
        return pltpu.make_async_copy(f_ref.at[pl.ds(tile * tile_groups, tile_groups)], fbuf.at[s], fsem.at[s])

    def wait_rows(tile):
        s = lax.rem(tile, 3)
        for _ in range(TOP_K):
            pltpu.make_async_copy(fbuf.at[s], fbuf.at[s], sem.at[s]).wait()

    @pl.when(i == 0)
    def _():
        stage(i).start()
        zbuf[...] = jnp.zeros_like(zbuf)

        def fill(e):
            return pltpu.make_async_copy(zbuf, xs_ref.at[pl.ds(pl.multiple_of(pad_ref[e] * TM, TM), TM)], zsem)

        for e in range(n_fills):
            @pl.when(pad_ref[e] >= 0)
            def _():
                fill(e).start()
        for e in range(n_fills):
            @pl.when(pad_ref[e] >= 0)
            def _():
                fill(e).wait()

    stage(i).wait()

    @pl.when(i + 1 < n_tiles)
    def _():
        stage(i + 1).start()

    slot = lax.rem(i, 3)
    _row_copies(fbuf.at[slot], xs_ref, dest_ref, sem.at[slot], scatter=True, wait_here=False)

    @pl.when(i > 0)
    def _():
        wait_rows(i - 1)

    @pl.when(i == n_tiles - 1)
    def _():
        wait_rows(i)


def _dispatch(f, dest, pad_at, *, n_tiles, n_rows):
    n_fills = pad_at.shape[0]
    n, d = f.shape
    grid_spec = pltpu.PrefetchScalarGridSpec(
        num_scalar_prefetch=1,
        grid=(n_tiles,),
        in_specs=[pl.BlockSpec((None, 1, TOP_K * TB), lambda i, pad: (i, 0, 0), memory_space=pltpu.SMEM),
                  pl.BlockSpec(memory_space=pl.ANY)],
        out_specs=pl.BlockSpec(memory_space=pl.ANY),
        scratch_shapes=[pltpu.VMEM((TM, d), F32), pltpu.VMEM((3, TB // SUBLANE, SUBLANE, d), F32),
                        pltpu.SemaphoreType.DMA((3,)), pltpu.SemaphoreType.DMA(()), pltpu.SemaphoreType.DMA((3,))],
    )
    return pl.pallas_call(
        functools.partial(_dispatch_kernel, n_fills=n_fills, n_tiles=n_tiles),
        grid_spec=grid_spec,
        out_shape=jax.ShapeDtypeStruct((n_rows, d), F32),
        compiler_params=_params(1),
        name="moe_dispatch",
    )(pad_at, dest, f.reshape(n // SUBLANE, SUBLANE, d))


def _experts_kernel(te_ref, tv_ref, nu_ref, x_ref, wg_ref, wu_ref, wd_ref, o_ref, xb_ref, wg_buf, wu_buf, wd_buf, wsem,
                    *, layer, n_chunks):
    i = pl.program_id(0)
    n_used = nu_ref[0]
    used = i < n_used
    valid = tv_ref[i]

    def copies(tile, j, slot):
        e = te_ref[tile]
        cols = pl.ds(pl.multiple_of(j * FC, FC), FC)
        return (pltpu.make_async_copy(wg_ref.at[layer, e, :, cols], wg_buf.at[slot], wsem.at[0, slot]),
                pltpu.make_async_copy(wu_ref.at[layer, e, :, cols], wu_buf.at[slot], wsem.at[1, slot]),
                pltpu.make_async_copy(wd_ref.at[layer, e, cols, :], wd_buf.at[slot], wsem.at[2, slot]))

    @pl.when(i == 0)
    def _():
        for cp in copies(i, 0, 0):
            cp.start()

    o_ref[...] = jnp.zeros_like(o_ref)

    def compute(n_rows):
        xb_ref[0:n_rows, :] = x_ref[0:n_rows, :].astype(BF16)

        def chunk_body(j, carry):
            slot = lax.rem(i * n_chunks + j, 2)
            for cp in copies(i, j, slot):
                cp.wait()

            @pl.when(j + 1 < n_chunks)
            def _():
                for cp in copies(i, j + 1, 1 - slot):
                    cp.start()

            @pl.when((j + 1 == n_chunks) & (i + 1 < n_used))
            def _():
                for cp in copies(i + 1, 0, 1 - slot):
                    cp.start()

            xb = xb_ref[0:n_rows, :]
            act = _silu(_dot(xb, wg_buf[slot].astype(BF16))) * _dot(xb, wu_buf[slot].astype(BF16))
            o_ref[0:n_rows, :] += _dot(act.astype(BF16), wd_buf[slot].astype(BF16))
            return carry

        lax.fori_loop(0, n_chunks, chunk_body, 0)

    for n_rows in range(SUB, TM + 1, SUB):
        @pl.when(used & (valid > n_rows - SUB) & (valid <= n_rows))
        def _():
            compute(n_rows)


def _experts(xs, tile_expert, tile_valid, n_used, layer, we_gate, we_up, we_down):
    p, d = xs.shape
    dfe = we_gate.shape[-1]
    assert dfe % FC == 0 and p % TM == 0 and TM % SUB == 0
    n_chunks = dfe // FC
    n_tiles = p // TM

    grid_spec = pltpu.PrefetchScalarGridSpec(
        num_scalar_prefetch=3,
        grid=(n_tiles,),
        in_specs=[pl.BlockSpec((TM, d), lambda i, te, tv, nu: (jnp.minimum(i, nu[0] - 1), 0)),
                  pl.BlockSpec(memory_space=pl.ANY),
                  pl.BlockSpec(memory_space=pl.ANY),
                  pl.BlockSpec(memory_space=pl.ANY)],
        out_specs=pl.BlockSpec((TM, d), lambda i, te, tv, nu: (i, 0)),
        scratch_shapes=[pltpu.VMEM((TM, d), BF16), pltpu.VMEM((2, d, FC), F32), pltpu.VMEM((2, d, FC), F32),
                        pltpu.VMEM((2, FC, d), F32), pltpu.SemaphoreType.DMA((3, 2))],
    )
    return pl.pallas_call(
        functools.partial(_experts_kernel, layer=layer, n_chunks=n_chunks),
        grid_spec=grid_spec,
        out_shape=jax.ShapeDtypeStruct((p, d), F32),
        compiler_params=_params(1),
        name="expert_swiglu",
    )(tile_expert, tile_valid, n_used, xs, we_gate, we_up, we_down)


def _combine_kernel(dest_ref, ys_ref, info_ref, h_ref, mod_ref, gf_ref, o_ref, ybuf, sem, *, final_norm, n_tiles):
    i = pl.program_id(0)
    slot = lax.rem(i, 2)

    @pl.when(i < n_tiles)
    def _():
        _row_copies(ys_ref, ybuf.at[slot], dest_ref, sem.at[slot], scatter=False, wait_here=False)

    @pl.when(i > 0)
    def _():
        done = ybuf.at[1 - slot]
        pltpu.make_async_copy(done, done, sem.at[1 - slot]).wait()
        info = info_ref[...]
        d = h_ref.shape[1]
        mix = info[:, 4:5] * done[0].reshape(TB, d) + info[:, 5:6] * done[1].reshape(TB, d)
        h = h_ref[...] + mod_ref[...][5:6] * mix
        if final_norm:
            h = (h * lax.rsqrt(jnp.mean(h * h, axis=-1, keepdims=True) + EPS)) * gf_ref[...]
        o_ref[...] = h


def _combine(ys, dest, info, h, mod, g_final, *, n_tiles, tpl, nb, final_norm):
    n, d = h.shape
    prev = lambda i: (jnp.maximum(i - 1, 0), 0)
    return pl.pallas_call(
        functools.partial(_combine_kernel, final_norm=final_norm, n_tiles=n_tiles),
        grid=(n_tiles + 1,),
        in_specs=[pl.BlockSpec((None, 1, TOP_K * TB), lambda i: (jnp.minimum(i, n_tiles - 1), 0, 0),
                               memory_space=pltpu.SMEM),
                  pl.BlockSpec(memory_space=pl.ANY),
                  pl.BlockSpec((TB, LANE), prev),
                  pl.BlockSpec((TB, d), prev),
                  pl.BlockSpec((None, N_MOD, d), lambda i: (jnp.minimum(jnp.maximum(i - 1, 0) // tpl, nb), 0, 0)),
                  pl.BlockSpec((1, d), lambda i: (0, 0))],
        out_specs=pl.BlockSpec((TB, d), prev),
        out_shape=jax.ShapeDtypeStruct((n_tiles * TB, d), F32),
        scratch_shapes=[pltpu.VMEM((2, TOP_K, TB // SUBLANE, SUBLANE, d), F32), pltpu.SemaphoreType.DMA((2,))],
        compiler_params=_params(1),
        name="moe_combine",
    )(dest, ys, info, h, mod, g_final.reshape(1, d))


def _moe(f, info, rt, counts, h, mod, g_final, layer, we_gate, we_up, we_down, *, n_tiles, tpl, nb, final_norm):
    n = f.shape[0]
    n_experts = we_gate.shape[1]
    n_xtiles = -(-TOP_K * n // TM) + n_experts
    n_rows = n_xtiles * TM

    cnt = counts[0, :n_experts].astype(I32)
    tiles_e = (cnt + TM - 1) // TM
    tile_end = jnp.cumsum(tiles_e)
    starts = (tile_end - tiles_e) * TM
    n_used = tile_end[-1:]
    tile_ids = jnp.arange(n_xtiles, dtype=I32)
    tile_expert = jnp.minimum(jnp.sum(tile_ids[:, None] >= tile_end[None, :], axis=1), n_experts - 1).astype(I32)
    tile_valid = jnp.clip(cnt[tile_expert] - (tile_ids * TM - starts[tile_expert]), 0, TM).astype(I32)
    choice = rt[:, 0:TOP_K, :]
    rank = rt[:, TOP_K:2 * TOP_K, :]
    start_of = sum(jnp.where(choice == e, starts[e], 0) for e in range(n_experts))
    dest = (start_of + rank).reshape(n_tiles, 1, TOP_K * TB)
    tail = n_used + jnp.arange(n_experts, dtype=I32)
    pad_at = jnp.concatenate([jnp.where(tiles_e > 0, tile_end - 1, -1),
                              jnp.where(tail < n_xtiles, tail, -1)]).astype(I32)

    xs = _dispatch(f, dest, pad_at, n_tiles=n_tiles, n_rows=n_rows)
    ys = _experts(xs, tile_expert, tile_valid, n_used, layer, we_gate, we_up, we_down)
    return _combine(ys, dest, info, h, mod, g_final, n_tiles=n_tiles, tpl=tpl, nb=nb, final_norm=final_norm)


def kernel(x, c, ctx, c_ctx, w_mod, b_mod, g_mix, g_ffn, g_final, w_in_ab, conv_w, pool_w, pool_scale,
           w_out_ab, w_ff_gate, w_ff_up, w_ff_down, w_qkv, rpb, w_out_na, w_router, we_gate, we_up, we_down):
    nb, seq_l, d = x.shape
    seq_c = ctx.shape[1]
    depth = w_mod.shape[0]
    heads = rpb.shape[1]
    na_rows = (rpb.shape[2] + 1) // 2
    na_cols = (rpb.shape[3] + 1) // 2
    assert seq_l % TB == 0 and seq_c % TB == 0 and nb + 1 <= SUBLANE and depth % 2 == 0
    tpl = seq_l // TB
    nlat = nb * tpl
    nall = nlat + nb * seq_c // TB

    cond = jnp.zeros((SUBLANE, d), F32).at[:nb].set(c).at[nb].set(c_ctx)
    mods = _adaln(cond, w_mod, b_mod).reshape(depth, SUBLANE, N_MOD, d)

    bf = lambda a: a.astype(BF16)
    h = out = None
    for i in range(depth):
        j = i // 2
        last = i == depth - 1
        mod = mods[i]
        if i % 2 == 0:
            srcs = (x.reshape(nb * seq_l, d), ctx.reshape(nb * seq_c, d), 0) if i == 0 else (h, h, nlat)
            h = _mixer(*srcs, mod, g_mix[i], j, w_in_ab, conv_w, pool_w, pool_scale[j], w_out_ab,
                       n_tiles=nall, nlat=nlat, nb=nb, seq_l=seq_l, seq_c=seq_c)
            h = _ffn(h, mod, g_ffn[i], bf(w_ff_gate[j]), bf(w_ff_up[j]), bf(w_ff_down[j]),
                     n_rows=nall * TB, seq_l=seq_l, nb=nb)
        else:
            n_tiles = nlat if last else nall
            q, k, v = _qkv(h, mod, g_mix[i], j, w_qkv, n_rows=nall * TB, seq_l=seq_l, nb=nb,
                           qscale=(d // heads) ** -0.5)
            o_lat, o_ctx = _natten(q, k, v, _bias_table(rpb[j], na_cols), nb=nb, seq_l=seq_l, seq_c=seq_c,
                                   heads=heads, na_rows=na_rows)
            h, f, info, rt, counts = _route(o_lat, o_ctx, h, mod, g_ffn[i], j, w_out_na, w_router[j],
                                            n_rows=n_tiles * TB, seq_l=seq_l, nb=nb)
            h = _moe(f, info, rt, counts, h, mod, g_final, j, we_gate, we_up, we_down,
                     n_tiles=n_tiles, tpl=tpl, nb=nb, final_norm=last)
            if last:
                out = h
    return out.reshape(nb, seq_l, d)
```

```python
import functools

import numpy as np
import jax
import jax.numpy as jnp
from jax import lax
from jax.experimental import pallas as pl
from jax.experimental.pallas import tpu as pltpu

F32 = jnp.float32
BF16 = jnp.bfloat16
I32 = jnp.int32

GRID_W = 64
POOL_WINDOWS = (2, 4, 8, 16)
N_MOD = 6
TOP_K = 2
EPS = 1e-6
NEG = -1e30

LANE = 128
SUBLANE = 8
TB = 256
TBD = 512
HALO = SUBLANE
TM = 1024
SUB = 256
FC = 512
W_BUFFERS = 3
NA_GROUP = 4
ADALN_COLS = 1536
FFN_COLS = 768
VMEM_LIMIT = 56 * 1024 * 1024


def _params(n_axes):
    return pltpu.CompilerParams(dimension_semantics=("arbitrary",) * n_axes,
                                vmem_limit_bytes=VMEM_LIMIT)


def _rms_mod(x, g, shift, scale):
    y = x * lax.rsqrt(jnp.mean(x * x, axis=-1, keepdims=True) + EPS)
    return (y * g) * (1.0 + scale) + shift


def _silu(x):
    return x * jax.nn.sigmoid(x)


def _dot(a, b):
    return jnp.dot(a, b, preferred_element_type=F32)


def _dot_nt(a, b):
    return lax.dot_general(a, b, (((1,), (1,)), ((), ())), preferred_element_type=F32)


def _adaln_kernel(cond_ref, w_ref, b_ref, o_ref):
    s = _silu(cond_ref[...]).astype(BF16)
    o_ref[...] = _dot(s, w_ref[...].astype(BF16)) + b_ref[...]


def _adaln(cond, w_mod, b_mod):
    depth, d, nd = w_mod.shape
    tn = ADALN_COLS
    assert nd % tn == 0
    return pl.pallas_call(
        _adaln_kernel,
        grid=(depth, nd // tn),
        in_specs=[pl.BlockSpec((SUBLANE, d), lambda l, n: (0, 0)),
                  pl.BlockSpec((None, d, tn), lambda l, n: (l, 0, n)),
                  pl.BlockSpec((None, 1, tn), lambda l, n: (l, 0, n))],
        out_specs=pl.BlockSpec((None, SUBLANE, tn), lambda l, n: (l, 0, n)),
        out_shape=jax.ShapeDtypeStruct((depth, SUBLANE, nd), F32),
        compiler_params=_params(2),
        name="adaln",
    )(cond, w_mod, b_mod.reshape(depth, 1, nd))


def _mixer_kernel(ap_ref, ac_ref, an_ref, bp_ref, bc_ref, bn_ref, mod_ref, g_ref, win_ref, cw_ref, pw_ref, ps_ref,
                  wout_ref, o_ref, xs_ref, z_ref, p_ref, winb_ref, woutb_ref,
                  *, nlat, tpl, tpc, seq_l, seq_c, d_conv, pool_group):
    i = pl.program_id(0)
    is_ctx = i >= nlat
    pos = jnp.where(is_ctx, lax.rem(i - nlat, tpc), lax.rem(i, tpl)) * TB
    seq_len = jnp.where(is_ctx, seq_c, seq_l)
    ext = TB + 2 * HALO

    @pl.when(i == 0)
    def _():
        winb_ref[...] = win_ref[...].astype(BF16)
        woutb_ref[...] = wout_ref[...].astype(BF16)

    hc = jnp.where(is_ctx, bc_ref[...], ac_ref[...])
    xs_ref[0:HALO, :] = jnp.where(is_ctx, bp_ref[...], ap_ref[...])
    xs_ref[HALO:HALO + TB, :] = hc
    xs_ref[HALO + TB:ext, :] = jnp.where(is_ctx, bn_ref[...], an_ref[...])
    mod = mod_ref[...]
    a = _rms_mod(xs_ref[...], g_ref[...], mod[0:1], mod[1:2]).astype(BF16)
    u = _dot(a, winb_ref[...])
    srow = lax.broadcasted_iota(I32, (ext, 1), 0) + (pos - HALO)
    u = jnp.where((srow >= 0) & (srow < seq_len), u, 0.0)

    z_ref[...] = u[:, d_conv:2 * d_conv] * u[:, 2 * d_conv:3 * d_conv]
    p_ref[...] = u[:, 3 * d_conv:]
    cw = cw_ref[...]
    conv = (z_ref[HALO - 1:HALO - 1 + TB, :] * cw[0:1] + z_ref[HALO:HALO + TB, :] * cw[1:2]
            + z_ref[HALO + 1:HALO + 1 + TB, :] * cw[2:3])
    pieces = [u[HALO:HALO + TB, 0:d_conv] * conv]

    spos = srow[HALO:HALO + TB]
    ps = ps_ref[...]
    for g, win in enumerate(POOL_WINDOWS):
        lo, hi = win // 2, win - 1 - win // 2
        cols = slice(g * pool_group, (g + 1) * pool_group)
        acc = p_ref[HALO - lo:HALO - lo + TB, cols]
        for dlt in range(-lo + 1, hi + 1):
            acc = acc + p_ref[HALO + dlt:HALO + dlt + TB, cols]
        cnt = jnp.minimum(spos + hi, seq_len - 1) - jnp.maximum(spos - lo, 0) + 1
        diff = acc / cnt.astype(F32) - p_ref[HALO:HALO + TB, cols]
        pieces.append(_dot(diff.astype(BF16), pw_ref[g].astype(BF16)) * ps[:, cols])
    cat = jnp.concatenate(pieces, axis=-1).astype(BF16)
    o_ref[...] = hc + mod[2:3] * _dot(cat, woutb_ref[...])


def _mixer(src_a, src_b, b_off, mod, g, layer, w_in, conv_w, pool_w, pool_scale, w_out,
           *, n_tiles, nlat, nb, seq_l, seq_c):
    d = src_a.shape[1]
    d_conv = conv_w.shape[-1]
    d_pool = pool_scale.shape[-1]
    pool_group = pool_w.shape[-1]
    assert pool_group % LANE == 0 and d_conv % LANE == 0 and len(POOL_WINDOWS) == pool_w.shape[1]
    tpl, tpc = seq_l // TB, seq_c // TB
    ext = TB + 2 * HALO
    hb = TB // HALO
    a_last = src_a.shape[0] // HALO - 1
    b_last = src_b.shape[0] // HALO - 1
    kern = functools.partial(_mixer_kernel, nlat=nlat, tpl=tpl, tpc=tpc, seq_l=seq_l, seq_c=seq_c,
                             d_conv=d_conv, pool_group=pool_group)
    a_tile = lambda i: jnp.minimum(i, nlat - 1)
    b_tile = lambda i: jnp.maximum(i - nlat, 0) + b_off
    once = pl.Buffered(1)
    return pl.pallas_call(
        kern,
        grid=(n_tiles,),
        in_specs=[pl.BlockSpec((HALO, d), lambda i: (jnp.maximum(a_tile(i) * hb - 1, 0), 0)),
                  pl.BlockSpec((TB, d), lambda i: (a_tile(i), 0)),
                  pl.BlockSpec((HALO, d), lambda i: (jnp.minimum((a_tile(i) + 1) * hb, a_last), 0)),
                  pl.BlockSpec((HALO, d), lambda i: (jnp.maximum(b_tile(i) * hb - 1, 0), 0)),
                  pl.BlockSpec((TB, d), lambda i: (b_tile(i), 0)),
                  pl.BlockSpec((HALO, d), lambda i: (jnp.minimum((b_tile(i) + 1) * hb, b_last), 0)),
                  pl.BlockSpec((None, N_MOD, d), lambda i: (jnp.minimum(i // tpl, nb), 0, 0)),
                  pl.BlockSpec((1, d), lambda i: (0, 0)),
                  pl.BlockSpec((None,) + w_in.shape[1:], lambda i: (layer, 0, 0), pipeline_mode=once),
                  pl.BlockSpec((None,) + conv_w.shape[1:], lambda i: (layer, 0, 0)),
                  pl.BlockSpec((None,) + pool_w.shape[1:], lambda i: (layer, 0, 0, 0)),
                  pl.BlockSpec((1, d_pool), lambda i: (0, 0)),
                  pl.BlockSpec((None,) + w_out.shape[1:], lambda i: (layer, 0, 0), pipeline_mode=once)],
        out_specs=pl.BlockSpec((TB, d), lambda i: (i, 0)),
        out_shape=jax.ShapeDtypeStruct((n_tiles * TB, d), F32),
        scratch_shapes=[pltpu.VMEM((ext, d), F32), pltpu.VMEM((ext, d_conv), F32), pltpu.VMEM((ext, d_pool), F32),
                        pltpu.VMEM(w_in.shape[1:], BF16), pltpu.VMEM(w_out.shape[1:], BF16)],
        compiler_params=_params(1),
        name="conv_pool_mixer",
    )(src_a, src_a, src_a, src_b, src_b, src_b, mod, g.reshape(1, d), w_in, conv_w, pool_w,
      pool_scale.reshape(1, d_pool), w_out)


def _ffn_kernel(h_ref, mod_ref, g_ref, wg_ref, wu_ref, wd_ref, o_ref, act_ref, *, chunks):
    mod = mod_ref[...]
    h = h_ref[...]
    f = _rms_mod(h, g_ref[...], mod[3:4], mod[4:5]).astype(BF16)
    for c0, c1 in chunks:
        gate = _dot(f, wg_ref[:, c0:c1])
        up = _dot(f, wu_ref[:, c0:c1])
        act_ref[:, c0:c1] = (_silu(gate) * up).astype(BF16)
    o_ref[...] = h + mod[5:6] * _dot(act_ref[...], wd_ref[...])


def _ffn(h, mod, g, wg, wu, wd, *, n_rows, seq_l, nb):
    t, d = h.shape
    dff = wg.shape[-1]
    step = FFN_COLS
    chunks = tuple((c, min(c + step, dff)) for c in range(0, dff, step))
    const = lambda i: (0, 0)
    once = pl.Buffered(1)
    assert n_rows % TBD == 0 and seq_l % TBD == 0
    tpl = seq_l // TBD
    return pl.pallas_call(
        functools.partial(_ffn_kernel, chunks=chunks),
        grid=(n_rows // TBD,),
        in_specs=[pl.BlockSpec((TBD, d), lambda i: (i, 0)),
                  pl.BlockSpec((None, N_MOD, d), lambda i: (jnp.minimum(i // tpl, nb), 0, 0)),
                  pl.BlockSpec((1, d), const),
                  pl.BlockSpec(wg.shape, const, pipeline_mode=once),
                  pl.BlockSpec(wu.shape, const, pipeline_mode=once),
                  pl.BlockSpec(wd.shape, const, pipeline_mode=once)],
        out_specs=pl.BlockSpec((TBD, d), lambda i: (i, 0)),
        out_shape=jax.ShapeDtypeStruct((n_rows, d), F32),
        scratch_shapes=[pltpu.VMEM((TBD, dff), BF16)],
        compiler_params=_params(1),
        name="dense_swiglu",
    )(h, mod, g.reshape(1, d), wg, wu, wd)


def _qkv_kernel(h_ref, mod_ref, g_ref, w_ref, q_ref, k_ref, v_ref, wb_ref, *, d, qscale):
    @pl.when(pl.program_id(0) == 0)
    def _():
        wb_ref[...] = w_ref[...].astype(BF16)

    mod = mod_ref[...]
    a = _rms_mod(h_ref[...], g_ref[...], mod[0:1], mod[1:2]).astype(BF16)
    q_ref[...] = (_dot(a, wb_ref[:, 0:d]) * qscale).astype(BF16)
    k_ref[...] = _dot(a, wb_ref[:, d:2 * d]).astype(BF16)
    v_ref[...] = _dot(a, wb_ref[:, 2 * d:3 * d]).astype(BF16)


def _qkv(h, mod, g, layer, w, *, n_rows, seq_l, nb, qscale):
    t, d = h.shape
    const = lambda i: (0, 0)
    out = jax.ShapeDtypeStruct((n_rows, d), BF16)
    assert n_rows % TBD == 0 and seq_l % TBD == 0
    tpl = seq_l // TBD
    return pl.pallas_call(
        functools.partial(_qkv_kernel, d=d, qscale=qscale),
        grid=(n_rows // TBD,),
        in_specs=[pl.BlockSpec((TBD, d), lambda i: (i, 0)),
                  pl.BlockSpec((None, N_MOD, d), lambda i: (jnp.minimum(i // tpl, nb), 0, 0)),
                  pl.BlockSpec((1, d), const),
                  pl.BlockSpec((None,) + w.shape[1:], lambda i: (layer, 0, 0), pipeline_mode=pl.Buffered(1))],
        out_specs=[pl.BlockSpec((TBD, d), lambda i: (i, 0))] * 3,
        out_shape=[out, out, out],
        scratch_shapes=[pltpu.VMEM(w.shape[1:], BF16)],
        compiler_params=_params(1),
        name="qkv_proj",
    )(h, mod, g.reshape(1, d), w)


def _softmax_pv(s_parts, v_parts):
    m = s_parts[0].max(axis=-1, keepdims=True)
    for s in s_parts[1:]:
        m = jnp.maximum(m, s.max(axis=-1, keepdims=True))
    den = 0.0
    out = 0.0
    for s, v in zip(s_parts, v_parts):
        p = jnp.exp(s - m)
        den = den + p.sum(axis=-1, keepdims=True)
        out = out + _dot(p.astype(BF16), v)
    return out / den


def _natten_kernel(q_ref, k_ref, v_ref, kc_ref, vc_ref, qc_ref, bias_ref, o_ref, oc_ref,
                   s_scr, p_scr, den_scr, *, rows, na_rows, head_dim):
    w = GRID_W
    band = na_rows * w
    lane = lax.broadcasted_iota(I32, (1, LANE), 1)
    head0 = lane < head_dim
    zero = jnp.zeros((), BF16)
    kc = kc_ref[...]
    vc = vc_ref[...]

    def stack(q):
        return jnp.concatenate([jnp.where(head0, q, zero), jnp.where(head0, zero, q)], axis=0)

    def unstack(o, n):
        return jnp.where(head0, o[0:n], o[n:2 * n])

    def offsets(r):
        start = jnp.clip(r - na_rows // 2, 0, rows - na_rows)
        return pl.multiple_of(r * w, w), pl.multiple_of(start * w, w), start - r + (na_rows - 1)

    def scores(r, slot):
        q0, k0, d0 = offsets(r)
        qq = stack(q_ref[pl.ds(q0, w), :])
        bias = jnp.concatenate([bias_ref[d0 + 2 * p] for p in range(band // LANE)], axis=-1)
        s_scr[slot, :, 0:band] = _dot_nt(qq, k_ref[pl.ds(k0, band), :]) + bias
        s_scr[slot, :, band:] = _dot_nt(qq, kc)

    def softmax(slot):
        s = s_scr[slot]
        p = jnp.exp(s - s.max(axis=-1, keepdims=True))
        den_scr[slot] = p.sum(axis=-1, keepdims=True)
        p_scr[slot] = p.astype(BF16)

    def values(r, slot):
        q0, k0, _ = offsets(r)
        o = _dot(p_scr[slot, :, 0:band], v_ref[pl.ds(k0, band), :]) + _dot(p_scr[slot, :, band:], vc)
        o_ref[pl.ds(q0, w), :] = unstack(o / den_scr[slot], w).astype(o_ref.dtype)

    n_groups = rows // NA_GROUP
    assert rows % NA_GROUP == 0 and n_groups % 2 == 0 and n_groups >= 4

    def step(u, parity, do_scores, do_softmax, do_values):
        for g in range(NA_GROUP):
            if do_softmax:
                softmax((1 - parity) * NA_GROUP + g)
        for g in range(NA_GROUP):
            if do_scores:
                scores(u * NA_GROUP + g, parity * NA_GROUP + g)
            if do_values:
                values((u - 2) * NA_GROUP + g, parity * NA_GROUP + g)

    step(0, 0, True, False, False)
    step(1, 1, True, True, False)

    def group_pair_body(u2, carry):
        step(2 * u2, 0, True, True, True)
        step(2 * u2 + 1, 1, True, True, True)
        return carry

    lax.fori_loop(1, n_groups // 2, group_pair_body, 0)
    step(n_groups, 0, False, True, True)
    step(n_groups + 1, 1, False, False, True)

    qc = qc_ref[...]
    nc = qc.shape[0]
    oc = _softmax_pv([_dot_nt(stack(qc), kc)], [vc])
    oc_ref[...] = unstack(oc, nc).astype(oc_ref.dtype)


def _bias_table(rpb, na_cols):
    heads, nr, nc = rpb.shape
    w = GRID_W
    jcol = np.arange(w)
    cstart = np.clip(jcol - na_cols // 2, 0, w - na_cols)
    kcol = np.arange(w)
    inside = (kcol[None, :] >= cstart[:, None]) & (kcol[None, :] < cstart[:, None] + na_cols)
    dc = kcol[None, :] - jcol[:, None] + (na_cols - 1)
    pick = jnp.asarray((dc[None] == np.arange(nc)[:, None, None]) & inside[None], F32)
    t2 = jnp.einsum("hdm,mqk->hdqk", rpb, pick, precision=lax.Precision.HIGHEST)
    t2 = jnp.where(inside[None, None], t2, NEG)
    t3 = jnp.concatenate([t2[:, :-1], t2[:, 1:]], axis=-1)
    t3 = t3.reshape(heads // 2, 2, nr - 1, w, 2 * w).transpose(0, 2, 1, 3, 4)
    return t3.reshape(heads // 2, nr - 1, 2 * w, 2 * w).astype(F32)


def _natten(q, k, v, bias, *, nb, seq_l, seq_c, heads, na_rows):
    t, d = q.shape
    head_dim = d // heads
    assert 2 * head_dim == LANE and 2 * GRID_W == LANE and seq_l % GRID_W == 0
    rows = seq_l // GRID_W
    assert (nb * seq_l) % seq_c == 0
    cblk = nb * seq_l // seq_c
    lat = lambda b, hp: (b, hp)
    ctx = lambda b, hp: (cblk + b, hp)
    return pl.pallas_call(
        functools.partial(_natten_kernel, rows=rows, na_rows=na_rows, head_dim=head_dim),
        grid=(nb, heads // 2),
        in_specs=[pl.BlockSpec((seq_l, LANE), lat),
                  pl.BlockSpec((seq_l, LANE), lat),
                  pl.BlockSpec((seq_l, LANE), lat),
                  pl.BlockSpec((seq_c, LANE), ctx),
                  pl.BlockSpec((seq_c, LANE), ctx),
                  pl.BlockSpec((seq_c, LANE), ctx),
                  pl.BlockSpec((None,) + bias.shape[1:], lambda b, hp: (hp, 0, 0, 0))],
        out_specs=[pl.BlockSpec((seq_l, LANE), lat),
                   pl.BlockSpec((seq_c, LANE), lambda b, hp: (b, hp))],
        out_shape=[jax.ShapeDtypeStruct((nb * seq_l, d), BF16),
                   jax.ShapeDtypeStruct((nb * seq_c, d), BF16)],
        scratch_shapes=[pltpu.VMEM((2 * NA_GROUP, 2 * GRID_W, na_rows * GRID_W + seq_c), F32),
                        pltpu.VMEM((2 * NA_GROUP, 2 * GRID_W, na_rows * GRID_W + seq_c), BF16),
                        pltpu.VMEM((2 * NA_GROUP, 2 * GRID_W, 1), F32)],
        compiler_params=_params(2),
        name="neighbourhood_attention",
    )(q, k, v, k, v, q, bias)


def _split_bf16(x):
    hi = x.astype(BF16)
    return hi, (x - hi.astype(F32)).astype(BF16)


def _route_kernel(al_ref, ac_ref, h_ref, mod_ref, g_ref, wo_ref, wr_ref, h_out_ref, f_ref, info_ref, rt_ref,
                  cnt_ref, wob_ref, *, n_experts, nlat):
    i = pl.program_id(0)

    @pl.when(i == 0)
    def _():
        cnt_ref[...] = jnp.zeros_like(cnt_ref)
        wob_ref[...] = wo_ref[...].astype(BF16)

    mod = mod_ref[...]
    attn = jnp.where(i < nlat, al_ref[...], ac_ref[...])
    h = h_ref[...] + mod[2:3] * _dot(attn, wob_ref[...])
    h_out_ref[...] = h
    f = _rms_mod(h, g_ref[...], mod[3:4], mod[4:5])
    f_ref[...] = f

    f_hi, f_lo = _split_bf16(f)
    z_hi = _dot(f_hi, wr_ref[...])
    z_lo = _dot(f_lo, wr_ref[...])
    logits = z_hi[:, 0:LANE] + (z_hi[:, LANE:] + z_lo[:, 0:LANE])
    lane_i = lax.broadcasted_iota(I32, logits.shape, 1)
    lane = lane_i.astype(F32)
    logits = jnp.where(lane_i < n_experts, logits, -jnp.inf)
    v1 = logits.max(axis=-1, keepdims=True)
    i1 = jnp.where(logits == v1, lane, float(LANE)).min(axis=-1, keepdims=True)
    rest = jnp.where(lane == i1, -jnp.inf, logits)
    v2 = rest.max(axis=-1, keepdims=True)
    i2 = jnp.where(rest == v2, lane, float(LANE)).min(axis=-1, keepdims=True)
    e2 = jnp.exp(v2 - v1)
    g1 = 1.0 / (1.0 + e2)
    g2 = e2 * g1

    sel1 = lane == i1
    sel2 = lane == i2
    onehot = jnp.where(sel1 | sel2, 1.0, 0.0)
    tr = lax.broadcasted_iota(I32, (TB, TB), 0)
    tc = lax.broadcasted_iota(I32, (TB, TB), 1)
    before = _dot(jnp.where(tc < tr, 1.0, 0.0).astype(BF16), onehot.astype(BF16)) + cnt_ref[0:1, :]
    r1 = jnp.where(sel1, before, 0.0).sum(axis=-1, keepdims=True)
    r2 = jnp.where(sel2, before, 0.0).sum(axis=-1, keepdims=True)
    cnt_ref[...] = cnt_ref[...] + onehot.sum(axis=0, keepdims=True)

    info = jnp.where(lane_i == 0, i1, 0.0)
    info = jnp.where(lane_i == 1, i2, info)
    info = jnp.where(lane_i == 2, r1, info)
    info = jnp.where(lane_i == 3, r2, info)
    info = jnp.where(lane_i == 4, g1, info)
    info = jnp.where(lane_i == 5, g2, info)
    info_ref[...] = info
    rt_ref[...] = jnp.transpose(info)[0:SUBLANE, :].astype(I32)


def _route(a_lat, a_ctx, h, mod, g, layer, w_out, w_router, *, n_rows, seq_l, nb):
    t, d = h.shape
    n_experts = w_router.shape[-1]
    wr = jnp.concatenate(_split_bf16(jnp.zeros((d, LANE), F32).at[:, :n_experts].set(w_router)), axis=1)
    const = lambda i: (0, 0)
    row = lambda i: (i, 0)
    n = n_rows
    assert n_rows % TB == 0 and seq_l % TB == 0 and a_ctx.shape[0] % TB == 0
    n_tiles = n_rows // TB
    tpl = seq_l // TB
    nlat = nb * tpl
    return pl.pallas_call(
        functools.partial(_route_kernel, n_experts=n_experts, nlat=nlat),
        grid=(n_tiles,),
        in_specs=[pl.BlockSpec((TB, d), lambda i: (jnp.minimum(i, nlat - 1), 0)),
                  pl.BlockSpec((TB, d), lambda i: (jnp.maximum(i - nlat, 0), 0)),
                  pl.BlockSpec((TB, d), row),
                  pl.BlockSpec((None, N_MOD, d), lambda i: (jnp.minimum(i // tpl, nb), 0, 0)),
                  pl.BlockSpec((1, d), const),
                  pl.BlockSpec((None,) + w_out.shape[1:], lambda i: (layer, 0, 0), pipeline_mode=pl.Buffered(1)),
                  pl.BlockSpec(wr.shape, const)],
        out_specs=[pl.BlockSpec((TB, d), row),
                   pl.BlockSpec((TB, d), row),
                   pl.BlockSpec((TB, LANE), row),
                   pl.BlockSpec((None, SUBLANE, TB), lambda i: (i, 0, 0)),
                   pl.BlockSpec((SUBLANE, LANE), const)],
        out_shape=[jax.ShapeDtypeStruct((n, d), F32),
                   jax.ShapeDtypeStruct((n, d), F32),
                   jax.ShapeDtypeStruct((n, LANE), F32),
                   jax.ShapeDtypeStruct((n_tiles, SUBLANE, TB), I32),
                   jax.ShapeDtypeStruct((SUBLANE, LANE), F32)],
        scratch_shapes=[pltpu.VMEM(w_out.shape[1:], BF16)],
        compiler_params=_params(1),
        name="attn_out_router",
    )(a_lat, a_ctx, h, mod, g.reshape(1, d), w_out, wr)


def _row_copies(src_ref, dst_ref, idx_ref, sem, scatter, wait_here=True):
    def issue(jj, wait):
        for u in range(SUBLANE):
            for k in range(TOP_K):
                row = idx_ref[0, k * TB + jj * SUBLANE + u]
                if scatter:
                    cp = pltpu.make_async_copy(src_ref.at[jj, pl.ds(u, 1)], dst_ref.at[pl.ds(row, 1)], sem)
                else:
                    cp = pltpu.make_async_copy(src_ref.at[pl.ds(row, 1)], dst_ref.at[k, jj, pl.ds(u, 1)], sem)
                if wait:
                    cp.wait()
                else:
                    cp.start(priority=k % 2)

    def start_body(jj, c):
        issue(jj, False)
        return c

    def wait_body(jj, c):
        issue(jj, True)
        return c

    lax.fori_loop(0, TB // SUBLANE, start_body, 0)
    if wait_here:
        lax.fori_loop(0, TB // SUBLANE, wait_body, 0)


def _dispatch_kernel(pad_ref, dest_ref, f_ref, xs_ref, zbuf, fbuf, sem, zsem, fsem, *, n_fills, n_tiles):
    i = pl.program_id(0)
    tile_groups = TB // SUBLANE

    def stage(tile):
        s = lax.rem(tile, 3)
        return pltpu.make_async_copy(f_ref.at[pl.ds(tile * tile_groups, tile_groups)], fbuf.at[s], fsem.at[s])

    def wait_rows(tile):
        s = lax.rem(tile, 3)
        for _ in range(TOP_K):
            pltpu.make_async_copy(fbuf.at[s], fbuf.at[s], sem.at[s]).wait()

    @pl.when(i == 0)
    def _():
        stage(i).start()
        zbuf[...] = jnp.zeros_like(zbuf)

        def fill(e):
            return pltpu.make_async_copy(zbuf, xs_ref.at[pl.ds(pl.multiple_of(pad_ref[e] * TM, TM), TM)], zsem)

        for e in range(n_fills):
            @pl.when(pad_ref[e] >= 0)
            def _():
                fill(e).start()
        for e in range(n_fills):
            @pl.when(pad_ref[e] >= 0)
            def _():
                fill(e).wait()

    stage(i).wait()

    @pl.when(i + 1 < n_tiles)
    def _():
        stage(i + 1).start()

    slot = lax.rem(i, 3)
    _row_copies(fbuf.at[slot], xs_ref, dest_ref, sem.at[slot], scatter=True, wait_here=False)

    @pl.when(i > 0)
    def _():
        wait_rows(i - 1)

    @pl.when(i == n_tiles - 1)
    def _():
        wait_rows(i)


def _dispatch(f, dest, pad_at, *, n_tiles, n_rows):
    n_fills = pad_at.shape[0]
    n, d = f.shape
    grid_spec = pltpu.PrefetchScalarGridSpec(
        num_scalar_prefetch=1,
        grid=(n_tiles,),
        in_specs=[pl.BlockSpec((None, 1, TOP_K * TB), lambda i, pad: (i, 0, 0), memory_space=pltpu.SMEM),
                  pl.BlockSpec(memory_space=pl.ANY)],
        out_specs=pl.BlockSpec(memory_space=pl.ANY),
        scratch_shapes=[pltpu.VMEM((TM, d), F32), pltpu.VMEM((3, TB // SUBLANE, SUBLANE, d), F32),
                        pltpu.SemaphoreType.DMA((3,)), pltpu.SemaphoreType.DMA(()), pltpu.SemaphoreType.DMA((3,))],
    )
    return pl.pallas_call(
        functools.partial(_dispatch_kernel, n_fills=n_fills, n_tiles=n_tiles),
        grid_spec=grid_spec,
        out_shape=jax.ShapeDtypeStruct((n_rows, d), F32),
        compiler_params=_params(1),
        name="moe_dispatch",
    )(pad_at, dest, f.reshape(n // SUBLANE, SUBLANE, d))


def _experts_kernel(te_ref, tv_ref, nu_ref, x_ref, wg_ref, wu_ref, wd_ref, o_ref, xb_ref, wg_buf, wu_buf, wd_buf, wsem,
                    *, layer, n_chunks):
    i = pl.program_id(0)
    n_used = nu_ref[0]
    used = i < n_used
    valid = tv_ref[i]

    def copies(tile, j, slot):
        e = te_ref[tile]
        cols = pl.ds(pl.multiple_of(j * FC, FC), FC)
        return (pltpu.make_async_copy(wg_ref.at[layer, e, :, cols], wg_buf.at[slot], wsem.at[0, slot]),
                pltpu.make_async_copy(wu_ref.at[layer, e, :, cols], wu_buf.at[slot], wsem.at[1, slot]),
                pltpu.make_async_copy(wd_ref.at[layer, e, cols, :], wd_buf.at[slot], wsem.at[2, slot]))

    depth = W_BUFFERS - 1

    def start_ahead(c, ahead):
        t, j = lax.div(c + ahead, n_chunks), lax.rem(c + ahead, n_chunks)

        @pl.when(t < n_used)
        def _():
            for cp in copies(t, j, lax.rem(c + ahead, W_BUFFERS)):
                cp.start()

    @pl.when(i == 0)
    def _():
        for ahead in range(depth):
            start_ahead(0, ahead)

    o_ref[...] = jnp.zeros_like(o_ref)

    def compute(n_rows):
        xb_ref[0:n_rows, :] = x_ref[0:n_rows, :].astype(BF16)

        def chunk_body(j, carry):
            c = i * n_chunks + j
            slot = lax.rem(c, W_BUFFERS)
            for cp in copies(i, j, slot):
                cp.wait()
            start_ahead(c, depth)

            xb = xb_ref[0:n_rows, :]
            act = _silu(_dot(xb, wg_buf[slot].astype(BF16))) * _dot(xb, wu_buf[slot].astype(BF16))
            o_ref[0:n_rows, :] += _dot(act.astype(BF16), wd_buf[slot].astype(BF16))
            return carry

        lax.fori_loop(0, n_chunks, chunk_body, 0)

    for n_rows in range(SUB, TM + 1, SUB):
        @pl.when(used & (valid > n_rows - SUB) & (valid <= n_rows))
        def _():
            compute(n_rows)


def _experts(xs, tile_expert, tile_valid, n_used, layer, we_gate, we_up, we_down):
    p, d = xs.shape
    dfe = we_gate.shape[-1]
    assert dfe % FC == 0 and p % TM == 0 and TM % SUB == 0
    n_chunks = dfe // FC
    n_tiles = p // TM

    grid_spec = pltpu.PrefetchScalarGridSpec(
        num_scalar_prefetch=3,
        grid=(n_tiles,),
        in_specs=[pl.BlockSpec((TM, d), lambda i, te, tv, nu: (jnp.minimum(i, nu[0] - 1), 0)),
                  pl.BlockSpec(memory_space=pl.ANY),
                  pl.BlockSpec(memory_space=pl.ANY),
                  pl.BlockSpec(memory_space=pl.ANY)],
        out_specs=pl.BlockSpec((TM, d), lambda i, te, tv, nu: (i, 0)),
        scratch_shapes=[pltpu.VMEM((TM, d), BF16), pltpu.VMEM((W_BUFFERS, d, FC), F32),
                        pltpu.VMEM((W_BUFFERS, d, FC), F32), pltpu.VMEM((W_BUFFERS, FC, d), F32),
                        pltpu.SemaphoreType.DMA((3, W_BUFFERS))],
    )
    return pl.pallas_call(
        functools.partial(_experts_kernel, layer=layer, n_chunks=n_chunks),
        grid_spec=grid_spec,
        out_shape=jax.ShapeDtypeStruct((p, d), F32),
        compiler_params=_params(1),
        name="expert_swiglu",
    )(tile_expert, tile_valid, n_used, xs, we_gate, we_up, we_down)


def _combine_kernel(dest_ref, ys_ref, info_ref, h_ref, mod_ref, gf_ref, o_ref, ybuf, sem, *, final_norm, n_tiles):
    i = pl.program_id(0)
    slot = lax.rem(i, 2)

    @pl.when(i < n_tiles)
    def _():
        _row_copies(ys_ref, ybuf.at[slot], dest_ref, sem.at[slot], scatter=False, wait_here=False)

    @pl.when(i > 0)
    def _():
        done = ybuf.at[1 - slot]
        pltpu.make_async_copy(done, done, sem.at[1 - slot]).wait()
        info = info_ref[...]
        d = h_ref.shape[1]
        mix = info[:, 4:5] * done[0].reshape(TB, d) + info[:, 5:6] * done[1].reshape(TB, d)
        h = h_ref[...] + mod_ref[...][5:6] * mix
        if final_norm:
            h = (h * lax.rsqrt(jnp.mean(h * h, axis=-1, keepdims=True) + EPS)) * gf_ref[...]
        o_ref[...] = h


def _combine(ys, dest, info, h, mod, g_final, *, n_tiles, tpl, nb, final_norm):
    n, d = h.shape
    prev = lambda i: (jnp.maximum(i - 1, 0), 0)
    return pl.pallas_call(
        functools.partial(_combine_kernel, final_norm=final_norm, n_tiles=n_tiles),
        grid=(n_tiles + 1,),
        in_specs=[pl.BlockSpec((None, 1, TOP_K * TB), lambda i: (jnp.minimum(i, n_tiles - 1), 0, 0),
                               memory_space=pltpu.SMEM),
                  pl.BlockSpec(memory_space=pl.ANY),
                  pl.BlockSpec((TB, LANE), prev),
                  pl.BlockSpec((TB, d), prev),
                  pl.BlockSpec((None, N_MOD, d), lambda i: (jnp.minimum(jnp.maximum(i - 1, 0) // tpl, nb), 0, 0)),
                  pl.BlockSpec((1, d), lambda i: (0, 0))],
        out_specs=pl.BlockSpec((TB, d), prev),
        out_shape=jax.ShapeDtypeStruct((n_tiles * TB, d), F32),
        scratch_shapes=[pltpu.VMEM((2, TOP_K, TB // SUBLANE, SUBLANE, d), F32), pltpu.SemaphoreType.DMA((2,))],
        compiler_params=_params(1),
        name="moe_combine",
    )(dest, ys, info, h, mod, g_final.reshape(1, d))


def _moe(f, info, rt, counts, h, mod, g_final, layer, we_gate, we_up, we_down, *, n_tiles, tpl, nb, final_norm):
    n = f.shape[0]
    n_experts = we_gate.shape[1]
    n_xtiles = -(-TOP_K * n // TM) + n_experts
    n_rows = n_xtiles * TM

    cnt = counts[0, :n_experts].astype(I32)
    tiles_e = (cnt + TM - 1) // TM
    tile_end = jnp.cumsum(tiles_e)
    starts = (tile_end - tiles_e) * TM
    n_used = tile_end[-1:]
    tile_ids = jnp.arange(n_xtiles, dtype=I32)
    tile_expert = jnp.minimum(jnp.sum(tile_ids[:, None] >= tile_end[None, :], axis=1), n_experts - 1).astype(I32)
    tile_valid = jnp.clip(cnt[tile_expert] - (tile_ids * TM - starts[tile_expert]), 0, TM).astype(I32)
    choice = rt[:, 0:TOP_K, :]
    rank = rt[:, TOP_K:2 * TOP_K, :]
    start_of = sum(jnp.where(choice == e, starts[e], 0) for e in range(n_experts))
    dest = (start_of + rank).reshape(n_tiles, 1, TOP_K * TB)
    tail = n_used + jnp.arange(n_experts, dtype=I32)
    pad_at = jnp.concatenate([jnp.where(tiles_e > 0, tile_end - 1, -1),
                              jnp.where(tail < n_xtiles, tail, -1)]).astype(I32)

    xs = _dispatch(f, dest, pad_at, n_tiles=n_tiles, n_rows=n_rows)
    ys = _experts(xs, tile_expert, tile_valid, n_used, layer, we_gate, we_up, we_down)
    return _combine(ys, dest, info, h, mod, g_final, n_tiles=n_tiles, tpl=tpl, nb=nb, final_norm=final_norm)


def kernel(x, c, ctx, c_ctx, w_mod, b_mod, g_mix, g_ffn, g_final, w_in_ab, conv_w, pool_w, pool_scale,
           w_out_ab, w_ff_gate, w_ff_up, w_ff_down, w_qkv, rpb, w_out_na, w_router, we_gate, we_up, we_down):
    nb, seq_l, d = x.shape
    seq_c = ctx.shape[1]
    depth = w_mod.shape[0]
    heads = rpb.shape[1]
    na_rows = (rpb.shape[2] + 1) // 2
    na_cols = (rpb.shape[3] + 1) // 2
    assert seq_l % TB == 0 and seq_c % TB == 0 and nb + 1 <= SUBLANE and depth % 2 == 0
    tpl = seq_l // TB
    nlat = nb * tpl
    nall = nlat + nb * seq_c // TB

    cond = jnp.zeros((SUBLANE, d), F32).at[:nb].set(c).at[nb].set(c_ctx)
    mods = _adaln(cond, w_mod, b_mod).reshape(depth, SUBLANE, N_MOD, d)

    bf = lambda a: a.astype(BF16)
    h = out = None
    for i in range(depth):
        j = i // 2
        last = i == depth - 1
        mod = mods[i]
        if i % 2 == 0:
            srcs = (x.reshape(nb * seq_l, d), ctx.reshape(nb * seq_c, d), 0) if i == 0 else (h, h, nlat)
            h = _mixer(*srcs, mod, g_mix[i], j, w_in_ab, conv_w, pool_w, pool_scale[j], w_out_ab,
                       n_tiles=nall, nlat=nlat, nb=nb, seq_l=seq_l, seq_c=seq_c)
            h = _ffn(h, mod, g_ffn[i], bf(w_ff_gate[j]), bf(w_ff_up[j]), bf(w_ff_down[j]),
                     n_rows=nall * TB, seq_l=seq_l, nb=nb)
        else:
            n_tiles = nlat if last else nall
            q, k, v = _qkv(h, mod, g_mix[i], j, w_qkv, n_rows=nall * TB, seq_l=seq_l, nb=nb,
                           qscale=(d // heads) ** -0.5)
            o_lat, o_ctx = _natten(q, k, v, _bias_table(rpb[j], na_cols), nb=nb, seq_l=seq_l, seq_c=seq_c,
                                   heads=heads, na_rows=na_rows)
            h, f, info, rt, counts = _route(o_lat, o_ctx, h, mod, g_ffn[i], j, w_out_na, w_router[j],
                                            n_rows=n_tiles * TB, seq_l=seq_l, nb=nb)
            h = _moe(f, info, rt, counts, h, mod, g_final, j, we_gate, we_up, we_down,
                     n_tiles=n_tiles, tpl=tpl, nb=nb, final_norm=last)
            if last:
                out = h
    return out.reshape(nb, seq_l, d)
```

```python
import functools

import numpy as np
import jax
import jax.numpy as jnp
from jax import lax
from jax.experimental import pallas as pl
from jax.experimental.pallas import tpu as pltpu

F32 = jnp.float32
BF16 = jnp.bfloat16
I32 = jnp.int32

GRID_W = 64
POOL_WINDOWS = (2, 4, 8, 16)
N_MOD = 6
TOP_K = 2
EPS = 1e-6
NEG = -1e30

LANE = 128
SUBLANE = 8
TB = 256
TBD = 512
HALO = SUBLANE
TM = 1024
SUB = 256
FC = 512
W_BUFFERS = 4
NA_GROUP = 4
ADALN_COLS = 1536
FFN_COLS = 768
VMEM_LIMIT = 56 * 1024 * 1024


def _params(n_axes):
    return pltpu.CompilerParams(dimension_semantics=("arbitrary",) * n_axes,
                                vmem_limit_bytes=VMEM_LIMIT)


def _rms_mod(x, g, shift, scale):
    y = x * lax.rsqrt(jnp.mean(x * x, axis=-1, keepdims=True) + EPS)
    return (y * g) * (1.0 + scale) + shift


def _silu(x):
    return x * jax.nn.sigmoid(x)


def _dot(a, b):
    return jnp.dot(a, b, preferred_element_type=F32)


def _dot_nt(a, b):
    return lax.dot_general(a, b, (((1,), (1,)), ((), ())), preferred_element_type=F32)


def _adaln_kernel(cond_ref, w_ref, b_ref, o_ref):
    s = _silu(cond_ref[...]).astype(BF16)
    o_ref[...] = _dot(s, w_ref[...].astype(BF16)) + b_ref[...]


def _adaln(cond, w_mod, b_mod):
    depth, d, nd = w_mod.shape
    tn = ADALN_COLS
    assert nd % tn == 0
    return pl.pallas_call(
        _adaln_kernel,
        grid=(depth, nd // tn),
        in_specs=[pl.BlockSpec((SUBLANE, d), lambda l, n: (0, 0)),
                  pl.BlockSpec((None, d, tn), lambda l, n: (l, 0, n)),
                  pl.BlockSpec((None, 1, tn), lambda l, n: (l, 0, n))],
        out_specs=pl.BlockSpec((None, SUBLANE, tn), lambda l, n: (l, 0, n)),
        out_shape=jax.ShapeDtypeStruct((depth, SUBLANE, nd), F32),
        compiler_params=_params(2),
        name="adaln",
    )(cond, w_mod, b_mod.reshape(depth, 1, nd))


def _mixer_kernel(ap_ref, ac_ref, an_ref, bp_ref, bc_ref, bn_ref, mod_ref, g_ref, win_ref, cw_ref, pw_ref, ps_ref,
                  wout_ref, o_ref, xs_ref, z_ref, p_ref, winb_ref, woutb_ref,
                  *, nlat, tpl, tpc, seq_l, seq_c, d_conv, pool_group):
    i = pl.program_id(0)
    is_ctx = i >= nlat
    pos = jnp.where(is_ctx, lax.rem(i - nlat, tpc), lax.rem(i, tpl)) * TB
    seq_len = jnp.where(is_ctx, seq_c, seq_l)
    ext = TB + 2 * HALO

    @pl.when(i == 0)
    def _():
        winb_ref[...] = win_ref[...].astype(BF16)
        woutb_ref[...] = wout_ref[...].astype(BF16)

    hc = jnp.where(is_ctx, bc_ref[...], ac_ref[...])
    xs_ref[0:HALO, :] = jnp.where(is_ctx, bp_ref[...], ap_ref[...])
    xs_ref[HALO:HALO + TB, :] = hc
    xs_ref[HALO + TB:ext, :] = jnp.where(is_ctx, bn_ref[...], an_ref[...])
    mod = mod_ref[...]
    a = _rms_mod(xs_ref[...], g_ref[...], mod[0:1], mod[1:2]).astype(BF16)
    u = _dot(a, winb_ref[...])
    srow = lax.broadcasted_iota(I32, (ext, 1), 0) + (pos - HALO)
    u = jnp.where((srow >= 0) & (srow < seq_len), u, 0.0)

    z_ref[...] = u[:, d_conv:2 * d_conv] * u[:, 2 * d_conv:3 * d_conv]
    p_ref[...] = u[:, 3 * d_conv:]
    cw = cw_ref[...]
    conv = (z_ref[HALO - 1:HALO - 1 + TB, :] * cw[0:1] + z_ref[HALO:HALO + TB, :] * cw[1:2]
            + z_ref[HALO + 1:HALO + 1 + TB, :] * cw[2:3])
    pieces = [u[HALO:HALO + TB, 0:d_conv] * conv]

    spos = srow[HALO:HALO + TB]
    ps = ps_ref[...]
    for g, win in enumerate(POOL_WINDOWS):
        lo, hi = win // 2, win - 1 - win // 2
        cols = slice(g * pool_group, (g + 1) * pool_group)
        acc = p_ref[HALO - lo:HALO - lo + TB, cols]
        for dlt in range(-lo + 1, hi + 1):
            acc = acc + p_ref[HALO + dlt:HALO + dlt + TB, cols]
        cnt = jnp.minimum(spos + hi, seq_len - 1) - jnp.maximum(spos - lo, 0) + 1
        diff = acc / cnt.astype(F32) - p_ref[HALO:HALO + TB, cols]
        pieces.append(_dot(diff.astype(BF16), pw_ref[g].astype(BF16)) * ps[:, cols])
    cat = jnp.concatenate(pieces, axis=-1).astype(BF16)
    o_ref[...] = hc + mod[2:3] * _dot(cat, woutb_ref[...])


def _mixer(src_a, src_b, b_off, mod, g, layer, w_in, conv_w, pool_w, pool_scale, w_out,
           *, n_tiles, nlat, nb, seq_l, seq_c):
    d = src_a.shape[1]
    d_conv = conv_w.shape[-1]
    d_pool = pool_scale.shape[-1]
    pool_group = pool_w.shape[-1]
    assert pool_group % LANE == 0 and d_conv % LANE == 0 and len(POOL_WINDOWS) == pool_w.shape[1]
    tpl, tpc = seq_l // TB, seq_c // TB
    ext = TB + 2 * HALO
    hb = TB // HALO
    a_last = src_a.shape[0] // HALO - 1
    b_last = src_b.shape[0] // HALO - 1
    kern = functools.partial(_mixer_kernel, nlat=nlat, tpl=tpl, tpc=tpc, seq_l=seq_l, seq_c=seq_c,
                             d_conv=d_conv, pool_group=pool_group)
    a_tile = lambda i: jnp.minimum(i, nlat - 1)
    b_tile = lambda i: jnp.maximum(i - nlat, 0) + b_off
    once = pl.Buffered(1)
    return pl.pallas_call(
        kern,
        grid=(n_tiles,),
        in_specs=[pl.BlockSpec((HALO, d), lambda i: (jnp.maximum(a_tile(i) * hb - 1, 0), 0)),
                  pl.BlockSpec((TB, d), lambda i: (a_tile(i), 0)),
                  pl.BlockSpec((HALO, d), lambda i: (jnp.minimum((a_tile(i) + 1) * hb, a_last), 0)),
                  pl.BlockSpec((HALO, d), lambda i: (jnp.maximum(b_tile(i) * hb - 1, 0), 0)),
                  pl.BlockSpec((TB, d), lambda i: (b_tile(i), 0)),
                  pl.BlockSpec((HALO, d), lambda i: (jnp.minimum((b_tile(i) + 1) * hb, b_last), 0)),
                  pl.BlockSpec((None, N_MOD, d), lambda i: (jnp.minimum(i // tpl, nb), 0, 0)),
                  pl.BlockSpec((1, d), lambda i: (0, 0)),
                  pl.BlockSpec((None,) + w_in.shape[1:], lambda i: (layer, 0, 0), pipeline_mode=once),
                  pl.BlockSpec((None,) + conv_w.shape[1:], lambda i: (layer, 0, 0)),
                  pl.BlockSpec((None,) + pool_w.shape[1:], lambda i: (layer, 0, 0, 0)),
                  pl.BlockSpec((1, d_pool), lambda i: (0, 0)),
                  pl.BlockSpec((None,) + w_out.shape[1:], lambda i: (layer, 0, 0), pipeline_mode=once)],
        out_specs=pl.BlockSpec((TB, d), lambda i: (i, 0)),
        out_shape=jax.ShapeDtypeStruct((n_tiles * TB, d), F32),
        scratch_shapes=[pltpu.VMEM((ext, d), F32), pltpu.VMEM((ext, d_conv), F32), pltpu.VMEM((ext, d_pool), F32),
                        pltpu.VMEM(w_in.shape[1:], BF16), pltpu.VMEM(w_out.shape[1:], BF16)],
        compiler_params=_params(1),
        name="conv_pool_mixer",
    )(src_a, src_a, src_a, src_b, src_b, src_b, mod, g.reshape(1, d), w_in, conv_w, pool_w,
      pool_scale.reshape(1, d_pool), w_out)


def _ffn_kernel(h_ref, mod_ref, g_ref, wg_ref, wu_ref, wd_ref, o_ref, act_ref, *, chunks):
    mod = mod_ref[...]
    h = h_ref[...]
    f = _rms_mod(h, g_ref[...], mod[3:4], mod[4:5]).astype(BF16)
    for c0, c1 in chunks:
        gate = _dot(f, wg_ref[:, c0:c1])
        up = _dot(f, wu_ref[:, c0:c1])
        act_ref[:, c0:c1] = (_silu(gate) * up).astype(BF16)
    o_ref[...] = h + mod[5:6] * _dot(act_ref[...], wd_ref[...])


def _ffn(h, mod, g, wg, wu, wd, *, n_rows, seq_l, nb):
    t, d = h.shape
    dff = wg.shape[-1]
    step = FFN_COLS
    chunks = tuple((c, min(c + step, dff)) for c in range(0, dff, step))
    const = lambda i: (0, 0)
    once = pl.Buffered(1)
    assert n_rows % TBD == 0 and seq_l % TBD == 0
    tpl = seq_l // TBD
    return pl.pallas_call(
        functools.partial(_ffn_kernel, chunks=chunks),
        grid=(n_rows // TBD,),
        in_specs=[pl.BlockSpec((TBD, d), lambda i: (i, 0)),
                  pl.BlockSpec((None, N_MOD, d), lambda i: (jnp.minimum(i // tpl, nb), 0, 0)),
                  pl.BlockSpec((1, d), const),
                  pl.BlockSpec(wg.shape, const, pipeline_mode=once),
                  pl.BlockSpec(wu.shape, const, pipeline_mode=once),
                  pl.BlockSpec(wd.shape, const, pipeline_mode=once)],
        out_specs=pl.BlockSpec((TBD, d), lambda i: (i, 0)),
        out_shape=jax.ShapeDtypeStruct((n_rows, d), F32),
        scratch_shapes=[pltpu.VMEM((TBD, dff), BF16)],
        compiler_params=_params(1),
        name="dense_swiglu",
    )(h, mod, g.reshape(1, d), wg, wu, wd)


def _qkv_kernel(h_ref, mod_ref, g_ref, w_ref, q_ref, k_ref, v_ref, wb_ref, *, d, qscale):
    @pl.when(pl.program_id(0) == 0)
    def _():
        wb_ref[...] = w_ref[...].astype(BF16)

    mod = mod_ref[...]
    a = _rms_mod(h_ref[...], g_ref[...], mod[0:1], mod[1:2]).astype(BF16)
    q_ref[...] = (_dot(a, wb_ref[:, 0:d]) * qscale).astype(BF16)
    k_ref[...] = _dot(a, wb_ref[:, d:2 * d]).astype(BF16)
    v_ref[...] = _dot(a, wb_ref[:, 2 * d:3 * d]).astype(BF16)


def _qkv(h, mod, g, layer, w, *, n_rows, seq_l, nb, qscale):
    t, d = h.shape
    const = lambda i: (0, 0)
    out = jax.ShapeDtypeStruct((n_rows, d), BF16)
    assert n_rows % TBD == 0 and seq_l % TBD == 0
    tpl = seq_l // TBD
    return pl.pallas_call(
        functools.partial(_qkv_kernel, d=d, qscale=qscale),
        grid=(n_rows // TBD,),
        in_specs=[pl.BlockSpec((TBD, d), lambda i: (i, 0)),
                  pl.BlockSpec((None, N_MOD, d), lambda i: (jnp.minimum(i // tpl, nb), 0, 0)),
                  pl.BlockSpec((1, d), const),
                  pl.BlockSpec((None,) + w.shape[1:], lambda i: (layer, 0, 0), pipeline_mode=pl.Buffered(1))],
        out_specs=[pl.BlockSpec((TBD, d), lambda i: (i, 0))] * 3,
        out_shape=[out, out, out],
        scratch_shapes=[pltpu.VMEM(w.shape[1:], BF16)],
        compiler_params=_params(1),
        name="qkv_proj",
    )(h, mod, g.reshape(1, d), w)


def _softmax_pv(s_parts, v_parts):
    m = s_parts[0].max(axis=-1, keepdims=True)
    for s in s_parts[1:]:
        m = jnp.maximum(m, s.max(axis=-1, keepdims=True))
    den = 0.0
    out = 0.0
    for s, v in zip(s_parts, v_parts):
        p = jnp.exp(s - m)
        den = den + p.sum(axis=-1, keepdims=True)
        out = out + _dot(p.astype(BF16), v)
    return out / den


def _natten_kernel(q_ref, k_ref, v_ref, kc_ref, vc_ref, qc_ref, bias_ref, o_ref, oc_ref,
                   s_scr, p_scr, den_scr, *, rows, na_rows, head_dim):
    w = GRID_W
    band = na_rows * w
    lane = lax.broadcasted_iota(I32, (1, LANE), 1)
    head0 = lane < head_dim
    zero = jnp.zeros((), BF16)
    kc = kc_ref[...]
    vc = vc_ref[...]

    def stack(q):
        return jnp.concatenate([jnp.where(head0, q, zero), jnp.where(head0, zero, q)], axis=0)

    def unstack(o, n):
        return jnp.where(head0, o[0:n], o[n:2 * n])

    def offsets(r):
        start = jnp.clip(r - na_rows // 2, 0, rows - na_rows)
        return pl.multiple_of(r * w, w), pl.multiple_of(start * w, w), start - r + (na_rows - 1)

    def scores(r, slot):
        q0, k0, d0 = offsets(r)
        qq = stack(q_ref[pl.ds(q0, w), :])
        bias = jnp.concatenate([bias_ref[d0 + 2 * p] for p in range(band // LANE)], axis=-1)
        s_scr[slot, :, 0:band] = _dot_nt(qq, k_ref[pl.ds(k0, band), :]) + bias
        s_scr[slot, :, band:] = _dot_nt(qq, kc)

    def softmax(slot):
        s = s_scr[slot]
        p = jnp.exp(s - s.max(axis=-1, keepdims=True))
        den_scr[slot] = p.sum(axis=-1, keepdims=True)
        p_scr[slot] = p.astype(BF16)

    def values(r, slot):
        q0, k0, _ = offsets(r)
        o = _dot(p_scr[slot, :, 0:band], v_ref[pl.ds(k0, band), :]) + _dot(p_scr[slot, :, band:], vc)
        o_ref[pl.ds(q0, w), :] = unstack(o / den_scr[slot], w).astype(o_ref.dtype)

    n_groups = rows // NA_GROUP
    assert rows % NA_GROUP == 0 and n_groups % 2 == 0 and n_groups >= 4

    def step(u, parity, do_scores, do_softmax, do_values):
        for g in range(NA_GROUP):
            if do_softmax:
                softmax((1 - parity) * NA_GROUP + g)
        for g in range(NA_GROUP):
            if do_scores:
                scores(u * NA_GROUP + g, parity * NA_GROUP + g)
            if do_values:
                values((u - 2) * NA_GROUP + g, parity * NA_GROUP + g)

    step(0, 0, True, False, False)
    step(1, 1, True, True, False)

    def group_pair_body(u2, carry):
        step(2 * u2, 0, True, True, True)
        step(2 * u2 + 1, 1, True, True, True)
        return carry

    lax.fori_loop(1, n_groups // 2, group_pair_body, 0)
    step(n_groups, 0, False, True, True)
    step(n_groups + 1, 1, False, False, True)

    qc = qc_ref[...]
    nc = qc.shape[0]
    oc = _softmax_pv([_dot_nt(stack(qc), kc)], [vc])
    oc_ref[...] = unstack(oc, nc).astype(oc_ref.dtype)


def _bias_table(rpb, na_cols):
    heads, nr, nc = rpb.shape
    w = GRID_W
    jcol = np.arange(w)
    cstart = np.clip(jcol - na_cols // 2, 0, w - na_cols)
    kcol = np.arange(w)
    inside = (kcol[None, :] >= cstart[:, None]) & (kcol[None, :] < cstart[:, None] + na_cols)
    dc = kcol[None, :] - jcol[:, None] + (na_cols - 1)
    pick = jnp.asarray((dc[None] == np.arange(nc)[:, None, None]) & inside[None], F32)
    t2 = jnp.einsum("hdm,mqk->hdqk", rpb, pick, precision=lax.Precision.HIGHEST)
    t2 = jnp.where(inside[None, None], t2, NEG)
    t3 = jnp.concatenate([t2[:, :-1], t2[:, 1:]], axis=-1)
    t3 = t3.reshape(heads // 2, 2, nr - 1, w, 2 * w).transpose(0, 2, 1, 3, 4)
    return t3.reshape(heads // 2, nr - 1, 2 * w, 2 * w).astype(F32)


def _natten(q, k, v, bias, *, nb, seq_l, seq_c, heads, na_rows):
    t, d = q.shape
    head_dim = d // heads
    assert 2 * head_dim == LANE and 2 * GRID_W == LANE and seq_l % GRID_W == 0
    rows = seq_l // GRID_W
    assert (nb * seq_l) % seq_c == 0
    cblk = nb * seq_l // seq_c
    lat = lambda b, hp: (b, hp)
    ctx = lambda b, hp: (cblk + b, hp)
    return pl.pallas_call(
        functools.partial(_natten_kernel, rows=rows, na_rows=na_rows, head_dim=head_dim),
        grid=(nb, heads // 2),
        in_specs=[pl.BlockSpec((seq_l, LANE), lat),
                  pl.BlockSpec((seq_l, LANE), lat),
                  pl.BlockSpec((seq_l, LANE), lat),
                  pl.BlockSpec((seq_c, LANE), ctx),
                  pl.BlockSpec((seq_c, LANE), ctx),
                  pl.BlockSpec((seq_c, LANE), ctx),
                  pl.BlockSpec((None,) + bias.shape[1:], lambda b, hp: (hp, 0, 0, 0))],
        out_specs=[pl.BlockSpec((seq_l, LANE), lat),
                   pl.BlockSpec((seq_c, LANE), lambda b, hp: (b, hp))],
        out_shape=[jax.ShapeDtypeStruct((nb * seq_l, d), BF16),
                   jax.ShapeDtypeStruct((nb * seq_c, d), BF16)],
        scratch_shapes=[pltpu.VMEM((2 * NA_GROUP, 2 * GRID_W, na_rows * GRID_W + seq_c), F32),
                        pltpu.VMEM((2 * NA_GROUP, 2 * GRID_W, na_rows * GRID_W + seq_c), BF16),
                        pltpu.VMEM((2 * NA_GROUP, 2 * GRID_W, 1), F32)],
        compiler_params=_params(2),
        name="neighbourhood_attention",
    )(q, k, v, k, v, q, bias)


def _split_bf16(x):
    hi = x.astype(BF16)
    return hi, (x - hi.astype(F32)).astype(BF16)


def _route_kernel(al_ref, ac_ref, h_ref, mod_ref, g_ref, wo_ref, wr_ref, h_out_ref, f_ref, info_ref, rt_ref,
                  cnt_ref, wob_ref, *, n_experts, nlat):
    i = pl.program_id(0)

    @pl.when(i == 0)
    def _():
        cnt_ref[...] = jnp.zeros_like(cnt_ref)
        wob_ref[...] = wo_ref[...].astype(BF16)

    mod = mod_ref[...]
    attn = jnp.where(i < nlat, al_ref[...], ac_ref[...])
    h = h_ref[...] + mod[2:3] * _dot(attn, wob_ref[...])
    h_out_ref[...] = h
    f = _rms_mod(h, g_ref[...], mod[3:4], mod[4:5])
    f_ref[...] = f

    f_hi, f_lo = _split_bf16(f)
    z_hi = _dot(f_hi, wr_ref[...])
    z_lo = _dot(f_lo, wr_ref[...])
    logits = z_hi[:, 0:LANE] + (z_hi[:, LANE:] + z_lo[:, 0:LANE])
    lane_i = lax.broadcasted_iota(I32, logits.shape, 1)
    lane = lane_i.astype(F32)
    logits = jnp.where(lane_i < n_experts, logits, -jnp.inf)
    v1 = logits.max(axis=-1, keepdims=True)
    i1 = jnp.where(logits == v1, lane, float(LANE)).min(axis=-1, keepdims=True)
    rest = jnp.where(lane == i1, -jnp.inf, logits)
    v2 = rest.max(axis=-1, keepdims=True)
    i2 = jnp.where(rest == v2, lane, float(LANE)).min(axis=-1, keepdims=True)
    e2 = jnp.exp(v2 - v1)
    g1 = 1.0 / (1.0 + e2)
    g2 = e2 * g1

    sel1 = lane == i1
    sel2 = lane == i2
    onehot = jnp.where(sel1 | sel2, 1.0, 0.0)
    tr = lax.broadcasted_iota(I32, (TB, TB), 0)
    tc = lax.broadcasted_iota(I32, (TB, TB), 1)
    before = _dot(jnp.where(tc < tr, 1.0, 0.0).astype(BF16), onehot.astype(BF16)) + cnt_ref[0:1, :]
    r1 = jnp.where(sel1, before, 0.0).sum(axis=-1, keepdims=True)
    r2 = jnp.where(sel2, before, 0.0).sum(axis=-1, keepdims=True)
    cnt_ref[...] = cnt_ref[...] + onehot.sum(axis=0, keepdims=True)

    info = jnp.where(lane_i == 0, i1, 0.0)
    info = jnp.where(lane_i == 1, i2, info)
    info = jnp.where(lane_i == 2, r1, info)
    info = jnp.where(lane_i == 3, r2, info)
    info = jnp.where(lane_i == 4, g1, info)
    info = jnp.where(lane_i == 5, g2, info)
    info_ref[...] = info
    rt_ref[...] = jnp.transpose(info)[0:SUBLANE, :].astype(I32)


def _route(a_lat, a_ctx, h, mod, g, layer, w_out, w_router, *, n_rows, seq_l, nb):
    t, d = h.shape
    n_experts = w_router.shape[-1]
    wr = jnp.concatenate(_split_bf16(jnp.zeros((d, LANE), F32).at[:, :n_experts].set(w_router)), axis=1)
    const = lambda i: (0, 0)
    row = lambda i: (i, 0)
    n = n_rows
    assert n_rows % TB == 0 and seq_l % TB == 0 and a_ctx.shape[0] % TB == 0
    n_tiles = n_rows // TB
    tpl = seq_l // TB
    nlat = nb * tpl
    return pl.pallas_call(
        functools.partial(_route_kernel, n_experts=n_experts, nlat=nlat),
        grid=(n_tiles,),
        in_specs=[pl.BlockSpec((TB, d), lambda i: (jnp.minimum(i, nlat - 1), 0)),
                  pl.BlockSpec((TB, d), lambda i: (jnp.maximum(i - nlat, 0), 0)),
                  pl.BlockSpec((TB, d), row),
                  pl.BlockSpec((None, N_MOD, d), lambda i: (jnp.minimum(i // tpl, nb), 0, 0)),
                  pl.BlockSpec((1, d), const),
                  pl.BlockSpec((None,) + w_out.shape[1:], lambda i: (layer, 0, 0), pipeline_mode=pl.Buffered(1)),
                  pl.BlockSpec(wr.shape, const)],
        out_specs=[pl.BlockSpec((TB, d), row),
                   pl.BlockSpec((TB, d), row),
                   pl.BlockSpec((TB, LANE), row),
                   pl.BlockSpec((None, SUBLANE, TB), lambda i: (i, 0, 0)),
                   pl.BlockSpec((SUBLANE, LANE), const)],
        out_shape=[jax.ShapeDtypeStruct((n, d), F32),
                   jax.ShapeDtypeStruct((n, d), F32),
                   jax.ShapeDtypeStruct((n, LANE), F32),
                   jax.ShapeDtypeStruct((n_tiles, SUBLANE, TB), I32),
                   jax.ShapeDtypeStruct((SUBLANE, LANE), F32)],
        scratch_shapes=[pltpu.VMEM(w_out.shape[1:], BF16)],
        compiler_params=_params(1),
        name="attn_out_router",
    )(a_lat, a_ctx, h, mod, g.reshape(1, d), w_out, wr)


def _row_copies(src_ref, dst_ref, idx_ref, sem, scatter, wait_here=True):
    def issue(jj, wait):
        for u in range(SUBLANE):
            for k in range(TOP_K):
                row = idx_ref[0, k * TB + jj * SUBLANE + u]
                if scatter:
                    cp = pltpu.make_async_copy(src_ref.at[jj, pl.ds(u, 1)], dst_ref.at[pl.ds(row, 1)], sem)
                else:
                    cp = pltpu.make_async_copy(src_ref.at[pl.ds(row, 1)], dst_ref.at[k, jj, pl.ds(u, 1)], sem)
                if wait:
                    cp.wait()
                else:
                    cp.start(priority=k % 2)

    def start_body(jj, c):
        issue(jj, False)
        return c

    def wait_body(jj, c):
        issue(jj, True)
        return c

    lax.fori_loop(0, TB // SUBLANE, start_body, 0)
    if wait_here:
        lax.fori_loop(0, TB // SUBLANE, wait_body, 0)


def _dispatch_kernel(pad_ref, dest_ref, f_ref, xs_ref, zbuf, fbuf, sem, zsem, fsem, *, n_fills, n_tiles):
    i = pl.program_id(0)
    tile_groups = TB // SUBLANE

    def stage(tile):
        s = lax.rem(tile, 3)
        return pltpu.make_async_copy(f_ref.at[pl.ds(tile * tile_groups, tile_groups)], fbuf.at[s], fsem.at[s])

    def wait_rows(tile):
        s = lax.rem(tile, 3)
        for _ in range(TOP_K):
            pltpu.make_async_copy(fbuf.at[s], fbuf.at[s], sem.at[s]).wait()

    @pl.when(i == 0)
    def _():
        stage(i).start()
        zbuf[...] = jnp.zeros_like(zbuf)

        def fill(e):
            return pltpu.make_async_copy(zbuf, xs_ref.at[pl.ds(pl.multiple_of(pad_ref[e] * TM, TM), TM)], zsem)

        for e in range(n_fills):
            @pl.when(pad_ref[e] >= 0)
            def _():
                fill(e).start()
        for e in range(n_fills):
            @pl.when(pad_ref[e] >= 0)
            def _():
                fill(e).wait()

    stage(i).wait()

    @pl.when(i + 1 < n_tiles)
    def _():
        stage(i + 1).start()

    slot = lax.rem(i, 3)
    _row_copies(fbuf.at[slot], xs_ref, dest_ref, sem.at[slot], scatter=True, wait_here=False)

    @pl.when(i > 0)
    def _():
        wait_rows(i - 1)

    @pl.when(i == n_tiles - 1)
    def _():
        wait_rows(i)


def _dispatch(f, dest, pad_at, *, n_tiles, n_rows):
    n_fills = pad_at.shape[0]
    n, d = f.shape
    grid_spec = pltpu.PrefetchScalarGridSpec(
        num_scalar_prefetch=1,
        grid=(n_tiles,),
        in_specs=[pl.BlockSpec((None, 1, TOP_K * TB), lambda i, pad: (i, 0, 0), memory_space=pltpu.SMEM),
                  pl.BlockSpec(memory_space=pl.ANY)],
        out_specs=pl.BlockSpec(memory_space=pl.ANY),
        scratch_shapes=[pltpu.VMEM((TM, d), F32), pltpu.VMEM((3, TB // SUBLANE, SUBLANE, d), F32),
                        pltpu.SemaphoreType.DMA((3,)), pltpu.SemaphoreType.DMA(()), pltpu.SemaphoreType.DMA((3,))],
    )
    return pl.pallas_call(
        functools.partial(_dispatch_kernel, n_fills=n_fills, n_tiles=n_tiles),
        grid_spec=grid_spec,
        out_shape=jax.ShapeDtypeStruct((n_rows, d), F32),
        compiler_params=_params(1),
        name="moe_dispatch",
    )(pad_at, dest, f.reshape(n // SUBLANE, SUBLANE, d))


def _experts_kernel(te_ref, tv_ref, nu_ref, x_ref, wg_ref, wu_ref, wd_ref, o_ref, xb_ref, wg_buf, wu_buf, wd_buf, wsem,
                    *, layer, n_chunks):
    i = pl.program_id(0)
    n_used = nu_ref[0]
    used = i < n_used
    valid = tv_ref[i]

    def copies(tile, j, slot):
        e = te_ref[tile]
        cols = pl.ds(pl.multiple_of(j * FC, FC), FC)
        return (pltpu.make_async_copy(wg_ref.at[layer, e, :, cols], wg_buf.at[slot], wsem.at[0, slot]),
                pltpu.make_async_copy(wu_ref.at[layer, e, :, cols], wu_buf.at[slot], wsem.at[1, slot]),
                pltpu.make_async_copy(wd_ref.at[layer, e, cols, :], wd_buf.at[slot], wsem.at[2, slot]))

    depth = W_BUFFERS - 1

    def start_ahead(c, ahead):
        t, j = lax.div(c + ahead, n_chunks), lax.rem(c + ahead, n_chunks)

        @pl.when(t < n_used)
        def _():
            for cp in copies(t, j, lax.rem(c + ahead, W_BUFFERS)):
                cp.start()

    @pl.when(i == 0)
    def _():
        for ahead in range(depth):
            start_ahead(0, ahead)

    o_ref[...] = jnp.zeros_like(o_ref)

    def compute(n_rows):
        xb_ref[0:n_rows, :] = x_ref[0:n_rows, :].astype(BF16)

        def chunk_body(j, carry):
            c = i * n_chunks + j
            slot = lax.rem(c, W_BUFFERS)
            for cp in copies(i, j, slot):
                cp.wait()
            start_ahead(c, depth)

            xb = xb_ref[0:n_rows, :]
            act = _silu(_dot(xb, wg_buf[slot].astype(BF16))) * _dot(xb, wu_buf[slot].astype(BF16))
            o_ref[0:n_rows, :] += _dot(act.astype(BF16), wd_buf[slot].astype(BF16))
            return carry

        lax.fori_loop(0, n_chunks, chunk_body, 0)

    for n_rows in range(SUB, TM + 1, SUB):
        @pl.when(used & (valid > n_rows - SUB) & (valid <= n_rows))
        def _():
            compute(n_rows)


def _experts(xs, tile_expert, tile_valid, n_used, layer, we_gate, we_up, we_down):
    p, d = xs.shape
    dfe = we_gate.shape[-1]
    assert dfe % FC == 0 and p % TM == 0 and TM % SUB == 0
    n_chunks = dfe // FC
    n_tiles = p // TM

    grid_spec = pltpu.PrefetchScalarGridSpec(
        num_scalar_prefetch=3,
        grid=(n_tiles,),
        in_specs=[pl.BlockSpec((TM, d), lambda i, te, tv, nu: (jnp.minimum(i, nu[0] - 1), 0)),
                  pl.BlockSpec(memory_space=pl.ANY),
                  pl.BlockSpec(memory_space=pl.ANY),
                  pl.BlockSpec(memory_space=pl.ANY)],
        out_specs=pl.BlockSpec((TM, d), lambda i, te, tv, nu: (i, 0)),
        scratch_shapes=[pltpu.VMEM((TM, d), BF16), pltpu.VMEM((W_BUFFERS, d, FC), F32),
                        pltpu.VMEM((W_BUFFERS, d, FC), F32), pltpu.VMEM((W_BUFFERS, FC, d), F32),
                        pltpu.SemaphoreType.DMA((3, W_BUFFERS))],
    )
    return pl.pallas_call(
        functools.partial(_experts_kernel, layer=layer, n_chunks=n_chunks),
        grid_spec=grid_spec,
        out_shape=jax.ShapeDtypeStruct((p, d), F32),
        compiler_params=_params(1),
        name="expert_swiglu",
    )(tile_expert, tile_valid, n_used, xs, we_gate, we_up, we_down)


def _combine_kernel(dest_ref, ys_ref, info_ref, h_ref, mod_ref, gf_ref, o_ref, ybuf, sem, *, final_norm, n_tiles):
    i = pl.program_id(0)
    slot = lax.rem(i, 2)

    @pl.when(i < n_tiles)
    def _():
        _row_copies(ys_ref, ybuf.at[slot], dest_ref, sem.at[slot], scatter=False, wait_here=False)

    @pl.when(i > 0)
    def _():
        done = ybuf.at[1 - slot]
        pltpu.make_async_copy(done, done, sem.at[1 - slot]).wait()
        info = info_ref[...]
        d = h_ref.shape[1]
        mix = info[:, 4:5] * done[0].reshape(TB, d) + info[:, 5:6] * done[1].reshape(TB, d)
        h = h_ref[...] + mod_ref[...][5:6] * mix
        if final_norm:
            h = (h * lax.rsqrt(jnp.mean(h * h, axis=-1, keepdims=True) + EPS)) * gf_ref[...]
        o_ref[...] = h


def _combine(ys, dest, info, h, mod, g_final, *, n_tiles, tpl, nb, final_norm):
    n, d = h.shape
    prev = lambda i: (jnp.maximum(i - 1, 0), 0)
    return pl.pallas_call(
        functools.partial(_combine_kernel, final_norm=final_norm, n_tiles=n_tiles),
        grid=(n_tiles + 1,),
        in_specs=[pl.BlockSpec((None, 1, TOP_K * TB), lambda i: (jnp.minimum(i, n_tiles - 1), 0, 0),
                               memory_space=pltpu.SMEM),
                  pl.BlockSpec(memory_space=pl.ANY),
                  pl.BlockSpec((TB, LANE), prev),
                  pl.BlockSpec((TB, d), prev),
                  pl.BlockSpec((None, N_MOD, d), lambda i: (jnp.minimum(jnp.maximum(i - 1, 0) // tpl, nb), 0, 0)),
                  pl.BlockSpec((1, d), lambda i: (0, 0))],
        out_specs=pl.BlockSpec((TB, d), prev),
        out_shape=jax.ShapeDtypeStruct((n_tiles * TB, d), F32),
        scratch_shapes=[pltpu.VMEM((2, TOP_K, TB // SUBLANE, SUBLANE, d), F32), pltpu.SemaphoreType.DMA((2,))],
        compiler_params=_params(1),
        name="moe_combine",
    )(dest, ys, info, h, mod, g_final.reshape(1, d))


def _moe(f, info, rt, counts, h, mod, g_final, layer, we_gate, we_up, we_down, *, n_tiles, tpl, nb, final_norm):
    n = f.shape[0]
    n_experts = we_gate.shape[1]
    n_xtiles = -(-TOP_K * n // TM) + n_experts
    n_rows = n_xtiles * TM

    cnt = counts[0, :n_experts].astype(I32)
    tiles_e = (cnt + TM - 1) // TM
    tile_end = jnp.cumsum(tiles_e)
    starts = (tile_end - tiles_e) * TM
    n_used = tile_end[-1:]
    tile_ids = jnp.arange(n_xtiles, dtype=I32)
    tile_expert = jnp.minimum(jnp.sum(tile_ids[:, None] >= tile_end[None, :], axis=1), n_experts - 1).astype(I32)
    tile_valid = jnp.clip(cnt[tile_expert] - (tile_ids * TM - starts[tile_expert]), 0, TM).astype(I32)
    choice = rt[:, 0:TOP_K, :]
    rank = rt[:, TOP_K:2 * TOP_K, :]
    start_of = sum(jnp.where(choice == e, starts[e], 0) for e in range(n_experts))
    dest = (start_of + rank).reshape(n_tiles, 1, TOP_K * TB)
    tail = n_used + jnp.arange(n_experts, dtype=I32)
    pad_at = jnp.concatenate([jnp.where(tiles_e > 0, tile_end - 1, -1),
                              jnp.where(tail < n_xtiles, tail, -1)]).astype(I32)

    xs = _dispatch(f, dest, pad_at, n_tiles=n_tiles, n_rows=n_rows)
    ys = _experts(xs, tile_expert, tile_valid, n_used, layer, we_gate, we_up, we_down)
    return _combine(ys, dest, info, h, mod, g_final, n_tiles=n_tiles, tpl=tpl, nb=nb, final_norm=final_norm)


def kernel(x, c, ctx, c_ctx, w_mod, b_mod, g_mix, g_ffn, g_final, w_in_ab, conv_w, pool_w, pool_scale,
           w_out_ab, w_ff_gate, w_ff_up, w_ff_down, w_qkv, rpb, w_out_na, w_router, we_gate, we_up, we_down):
    nb, seq_l, d = x.shape
    seq_c = ctx.shape[1]
    depth = w_mod.shape[0]
    heads = rpb.shape[1]
    na_rows = (rpb.shape[2] + 1) // 2
    na_cols = (rpb.shape[3] + 1) // 2
    assert seq_l % TB == 0 and seq_c % TB == 0 and nb + 1 <= SUBLANE and depth % 2 == 0
    tpl = seq_l // TB
    nlat = nb * tpl
    nall = nlat + nb * seq_c // TB

    cond = jnp.zeros((SUBLANE, d), F32).at[:nb].set(c).at[nb].set(c_ctx)
    mods = _adaln(cond, w_mod, b_mod).reshape(depth, SUBLANE, N_MOD, d)

    bf = lambda a: a.astype(BF16)
    h = out = None
    for i in range(depth):
        j = i // 2
        last = i == depth - 1
        mod = mods[i]
        if i % 2 == 0:
            srcs = (x.reshape(nb * seq_l, d), ctx.reshape(nb * seq_c, d), 0) if i == 0 else (h, h, nlat)
            h = _mixer(*srcs, mod, g_mix[i], j, w_in_ab, conv_w, pool_w, pool_scale[j], w_out_ab,
                       n_tiles=nall, nlat=nlat, nb=nb, seq_l=seq_l, seq_c=seq_c)
            h = _ffn(h, mod, g_ffn[i], bf(w_ff_gate[j]), bf(w_ff_up[j]), bf(w_ff_down[j]),
                     n_rows=nall * TB, seq_l=seq_l, nb=nb)
        else:
            n_tiles = nlat if last else nall
            q, k, v = _qkv(h, mod, g_mix[i], j, w_qkv, n_rows=nall * TB, seq_l=seq_l, nb=nb,
                           qscale=(d // heads) ** -0.5)
            o_lat, o_ctx = _natten(q, k, v, _bias_table(rpb[j], na_cols), nb=nb, seq_l=seq_l, seq_c=seq_c,
                                   heads=heads, na_rows=na_rows)
            h, f, info, rt, counts = _route(o_lat, o_ctx, h, mod, g_ffn[i], j, w_out_na, w_router[j],
                                            n_rows=n_tiles * TB, seq_l=seq_l, nb=nb)
            h = _moe(f, info, rt, counts, h, mod, g_final, j, we_gate, we_up, we_down,
                     n_tiles=n_tiles, tpl=tpl, nb=nb, final_norm=last)
            if last:
                out = h
    return out.reshape(nb, seq_l, d)
```

```python
import functools

import numpy as np
import jax
import jax.numpy as jnp
from jax import lax
from jax.experimental import pallas as pl
from jax.experimental.pallas import tpu as pltpu

F32 = jnp.float32
BF16 = jnp.bfloat16
I32 = jnp.int32

GRID_W = 64
POOL_WINDOWS = (2, 4, 8, 16)
N_MOD = 6
TOP_K = 2
EPS = 1e-6
NEG = -1e30

LANE = 128
SUBLANE = 8
TB = 256
TBD = 512
HALO = SUBLANE
TM = 1024
SUB = 256
FC = 512
W_BUFFERS = 4
NA_GROUP = 4
ADALN_COLS = 1536
FFN_COLS = 768
VMEM_LIMIT = 56 * 1024 * 1024


def _params(n_axes):
    return pltpu.CompilerParams(dimension_semantics=("arbitrary",) * n_axes,
                                vmem_limit_bytes=VMEM_LIMIT)


def _rms_mod(x, g, shift, scale):
    y = x * lax.rsqrt(jnp.mean(x * x, axis=-1, keepdims=True) + EPS)
    return (y * g) * (1.0 + scale) + shift


def _silu(x):
    return x * jax.nn.sigmoid(x)


def _dot(a, b):
    return jnp.dot(a, b, preferred_element_type=F32)


def _dot_nt(a, b):
    return lax.dot_general(a, b, (((1,), (1,)), ((), ())), preferred_element_type=F32)


def _adaln_kernel(cond_ref, w_ref, b_ref, o_ref):
    s = _silu(cond_ref[...]).astype(BF16)
    o_ref[...] = _dot(s, w_ref[...].astype(BF16)) + b_ref[...]


def _adaln(cond, w_mod, b_mod):
    depth, d, nd = w_mod.shape
    tn = ADALN_COLS
    assert nd % tn == 0
    return pl.pallas_call(
        _adaln_kernel,
        grid=(depth, nd // tn),
        in_specs=[pl.BlockSpec((SUBLANE, d), lambda l, n: (0, 0)),
                  pl.BlockSpec((None, d, tn), lambda l, n: (l, 0, n)),
                  pl.BlockSpec((None, 1, tn), lambda l, n: (l, 0, n))],
        out_specs=pl.BlockSpec((None, SUBLANE, tn), lambda l, n: (l, 0, n)),
        out_shape=jax.ShapeDtypeStruct((depth, SUBLANE, nd), F32),
        compiler_params=_params(2),
        name="adaln",
    )(cond, w_mod, b_mod.reshape(depth, 1, nd))


def _mixer_kernel(ap_ref, ac_ref, an_ref, bp_ref, bc_ref, bn_ref, mod_ref, g_ref, win_ref, cw_ref, pw_ref, ps_ref,
                  wout_ref, o_ref, xs_ref, z_ref, p_ref, winb_ref, woutb_ref,
                  *, nlat, tpl, tpc, seq_l, seq_c, d_conv, pool_group):
    i = pl.program_id(0)
    is_ctx = i >= nlat
    pos = jnp.where(is_ctx, lax.rem(i - nlat, tpc), lax.rem(i, tpl)) * TB
    seq_len = jnp.where(is_ctx, seq_c, seq_l)
    ext = TB + 2 * HALO

    @pl.when(i == 0)
    def _():
        winb_ref[...] = win_ref[...].astype(BF16)
        woutb_ref[...] = wout_ref[...].astype(BF16)

    hc = jnp.where(is_ctx, bc_ref[...], ac_ref[...])
    xs_ref[0:HALO, :] = jnp.where(is_ctx, bp_ref[...], ap_ref[...])
    xs_ref[HALO:HALO + TB, :] = hc
    xs_ref[HALO + TB:ext, :] = jnp.where(is_ctx, bn_ref[...], an_ref[...])
    mod = mod_ref[...]
    a = _rms_mod(xs_ref[...], g_ref[...], mod[0:1], mod[1:2]).astype(BF16)
    u = _dot(a, winb_ref[...])
    srow = lax.broadcasted_iota(I32, (ext, 1), 0) + (pos - HALO)
    u = jnp.where((srow >= 0) & (srow < seq_len), u, 0.0)

    z_ref[...] = u[:, d_conv:2 * d_conv] * u[:, 2 * d_conv:3 * d_conv]
    p_ref[...] = u[:, 3 * d_conv:]
    cw = cw_ref[...]
    conv = (z_ref[HALO - 1:HALO - 1 + TB, :] * cw[0:1] + z_ref[HALO:HALO + TB, :] * cw[1:2]
            + z_ref[HALO + 1:HALO + 1 + TB, :] * cw[2:3])
    pieces = [u[HALO:HALO + TB, 0:d_conv] * conv]

    spos = srow[HALO:HALO + TB]
    ps = ps_ref[...]
    for g, win in enumerate(POOL_WINDOWS):
        lo, hi = win // 2, win - 1 - win // 2
        cols = slice(g * pool_group, (g + 1) * pool_group)
        acc = p_ref[HALO - lo:HALO - lo + TB, cols]
        for dlt in range(-lo + 1, hi + 1):
            acc = acc + p_ref[HALO + dlt:HALO + dlt + TB, cols]
        cnt = jnp.minimum(spos + hi, seq_len - 1) - jnp.maximum(spos - lo, 0) + 1
        diff = acc / cnt.astype(F32) - p_ref[HALO:HALO + TB, cols]
        pieces.append(_dot(diff.astype(BF16), pw_ref[g].astype(BF16)) * ps[:, cols])
    cat = jnp.concatenate(pieces, axis=-1).astype(BF16)
    o_ref[...] = hc + mod[2:3] * _dot(cat, woutb_ref[...])


def _mixer(src_a, src_b, b_off, mod, g, layer, w_in, conv_w, pool_w, pool_scale, w_out,
           *, n_tiles, nlat, nb, seq_l, seq_c):
    d = src_a.shape[1]
    d_conv = conv_w.shape[-1]
    d_pool = pool_scale.shape[-1]
    pool_group = pool_w.shape[-1]
    assert pool_group % LANE == 0 and d_conv % LANE == 0 and len(POOL_WINDOWS) == pool_w.shape[1]
    tpl, tpc = seq_l // TB, seq_c // TB
    ext = TB + 2 * HALO
    hb = TB // HALO
    a_last = src_a.shape[0] // HALO - 1
    b_last = src_b.shape[0] // HALO - 1
    kern = functools.partial(_mixer_kernel, nlat=nlat, tpl=tpl, tpc=tpc, seq_l=seq_l, seq_c=seq_c,
                             d_conv=d_conv, pool_group=pool_group)
    a_tile = lambda i: jnp.minimum(i, nlat - 1)
    b_tile = lambda i: jnp.maximum(i - nlat, 0) + b_off
    once = pl.Buffered(1)
    return pl.pallas_call(
        kern,
        grid=(n_tiles,),
        in_specs=[pl.BlockSpec((HALO, d), lambda i: (jnp.maximum(a_tile(i) * hb - 1, 0), 0)),
                  pl.BlockSpec((TB, d), lambda i: (a_tile(i), 0)),
                  pl.BlockSpec((HALO, d), lambda i: (jnp.minimum((a_tile(i) + 1) * hb, a_last), 0)),
                  pl.BlockSpec((HALO, d), lambda i: (jnp.maximum(b_tile(i) * hb - 1, 0), 0)),
                  pl.BlockSpec((TB, d), lambda i: (b_tile(i), 0)),
                  pl.BlockSpec((HALO, d), lambda i: (jnp.minimum((b_tile(i) + 1) * hb, b_last), 0)),
                  pl.BlockSpec((None, N_MOD, d), lambda i: (jnp.minimum(i // tpl, nb), 0, 0)),
                  pl.BlockSpec((1, d), lambda i: (0, 0)),
                  pl.BlockSpec((None,) + w_in.shape[1:], lambda i: (layer, 0, 0), pipeline_mode=once),
                  pl.BlockSpec((None,) + conv_w.shape[1:], lambda i: (layer, 0, 0)),
                  pl.BlockSpec((None,) + pool_w.shape[1:], lambda i: (layer, 0, 0, 0)),
                  pl.BlockSpec((1, d_pool), lambda i: (0, 0)),
                  pl.BlockSpec((None,) + w_out.shape[1:], lambda i: (layer, 0, 0), pipeline_mode=once)],
        out_specs=pl.BlockSpec((TB, d), lambda i: (i, 0)),
        out_shape=jax.ShapeDtypeStruct((n_tiles * TB, d), F32),
        scratch_shapes=[pltpu.VMEM((ext, d), F32), pltpu.VMEM((ext, d_conv), F32), pltpu.VMEM((ext, d_pool), F32),
                        pltpu.VMEM(w_in.shape[1:], BF16), pltpu.VMEM(w_out.shape[1:], BF16)],
        compiler_params=_params(1),
        name="conv_pool_mixer",
    )(src_a, src_a, src_a, src_b, src_b, src_b, mod, g.reshape(1, d), w_in, conv_w, pool_w,
      pool_scale.reshape(1, d_pool), w_out)


def _ffn_kernel(h_ref, mod_ref, g_ref, wg_ref, wu_ref, wd_ref, o_ref, act_ref, *, chunks):
    mod = mod_ref[...]
    h = h_ref[...]
    f = _rms_mod(h, g_ref[...], mod[3:4], mod[4:5]).astype(BF16)
    for c0, c1 in chunks:
        gate = _dot(f, wg_ref[:, c0:c1])
        up = _dot(f, wu_ref[:, c0:c1])
        act_ref[:, c0:c1] = (_silu(gate) * up).astype(BF16)
    o_ref[...] = h + mod[5:6] * _dot(act_ref[...], wd_ref[...])


def _ffn(h, mod, g, wg, wu, wd, *, n_rows, seq_l, nb):
    t, d = h.shape
    dff = wg.shape[-1]
    step = FFN_COLS
    chunks = tuple((c, min(c + step, dff)) for c in range(0, dff, step))
    const = lambda i: (0, 0)
    once = pl.Buffered(1)
    assert n_rows % TBD == 0 and seq_l % TBD == 0
    tpl = seq_l // TBD
    return pl.pallas_call(
        functools.partial(_ffn_kernel, chunks=chunks),
        grid=(n_rows // TBD,),
        in_specs=[pl.BlockSpec((TBD, d), lambda i: (i, 0)),
                  pl.BlockSpec((None, N_MOD, d), lambda i: (jnp.minimum(i // tpl, nb), 0, 0)),
                  pl.BlockSpec((1, d), const),
                  pl.BlockSpec(wg.shape, const, pipeline_mode=once),
                  pl.BlockSpec(wu.shape, const, pipeline_mode=once),
                  pl.BlockSpec(wd.shape, const, pipeline_mode=once)],
        out_specs=pl.BlockSpec((TBD, d), lambda i: (i, 0)),
        out_shape=jax.ShapeDtypeStruct((n_rows, d), F32),
        scratch_shapes=[pltpu.VMEM((TBD, dff), BF16)],
        compiler_params=_params(1),
        name="dense_swiglu",
    )(h, mod, g.reshape(1, d), wg, wu, wd)


def _qkv_kernel(h_ref, mod_ref, g_ref, w_ref, q_ref, k_ref, v_ref, wb_ref, *, d, qscale):
    @pl.when(pl.program_id(0) == 0)
    def _():
        wb_ref[...] = w_ref[...].astype(BF16)

    mod = mod_ref[...]
    a = _rms_mod(h_ref[...], g_ref[...], mod[0:1], mod[1:2]).astype(BF16)
    q_ref[...] = (_dot(a, wb_ref[:, 0:d]) * qscale).astype(BF16)
    k_ref[...] = _dot(a, wb_ref[:, d:2 * d]).astype(BF16)
    v_ref[...] = _dot(a, wb_ref[:, 2 * d:3 * d]).astype(BF16)


def _qkv(h, mod, g, layer, w, *, n_rows, seq_l, nb, qscale):
    t, d = h.shape
    const = lambda i: (0, 0)
    out = jax.ShapeDtypeStruct((n_rows, d), BF16)
    assert n_rows % TBD == 0 and seq_l % TBD == 0
    tpl = seq_l // TBD
    return pl.pallas_call(
        functools.partial(_qkv_kernel, d=d, qscale=qscale),
        grid=(n_rows // TBD,),
        in_specs=[pl.BlockSpec((TBD, d), lambda i: (i, 0)),
                  pl.BlockSpec((None, N_MOD, d), lambda i: (jnp.minimum(i // tpl, nb), 0, 0)),
                  pl.BlockSpec((1, d), const),
                  pl.BlockSpec((None,) + w.shape[1:], lambda i: (layer, 0, 0), pipeline_mode=pl.Buffered(1))],
        out_specs=[pl.BlockSpec((TBD, d), lambda i: (i, 0))] * 3,
        out_shape=[out, out, out],
        scratch_shapes=[pltpu.VMEM(w.shape[1:], BF16)],
        compiler_params=_params(1),
        name="qkv_proj",
    )(h, mod, g.reshape(1, d), w)


def _softmax_pv(s_parts, v_parts):
    m = s_parts[0].max(axis=-1, keepdims=True)
    for s in s_parts[1:]:
        m = jnp.maximum(m, s.max(axis=-1, keepdims=True))
    den = 0.0
    out = 0.0
    for s, v in zip(s_parts, v_parts):
        p = jnp.exp(s - m)
        den = den + p.sum(axis=-1, keepdims=True)
        out = out + _dot(p.astype(BF16), v)
    return out / den


def _natten_kernel(q_ref, k_ref, v_ref, kc_ref, vc_ref, qc_ref, bias_ref, o_ref, oc_ref,
                   s_scr, p_scr, den_scr, *, rows, na_rows, head_dim):
    w = GRID_W
    band = na_rows * w
    lane = lax.broadcasted_iota(I32, (1, LANE), 1)
    head0 = lane < head_dim
    zero = jnp.zeros((), BF16)
    kc = kc_ref[...]
    vc = vc_ref[...]

    def stack(q):
        return jnp.concatenate([jnp.where(head0, q, zero), jnp.where(head0, zero, q)], axis=0)

    def unstack(o, n):
        return jnp.where(head0, o[0:n], o[n:2 * n])

    def offsets(r):
        start = jnp.clip(r - na_rows // 2, 0, rows - na_rows)
        return pl.multiple_of(r * w, w), pl.multiple_of(start * w, w), start - r + (na_rows - 1)

    def scores(r, slot):
        q0, k0, d0 = offsets(r)
        qq = stack(q_ref[pl.ds(q0, w), :])
        bias = jnp.concatenate([bias_ref[d0 + 2 * p] for p in range(band // LANE)], axis=-1)
        s_scr[slot, 0:2 * w, 0:band] = _dot_nt(qq, k_ref[pl.ds(k0, band), :]) + bias
        s_scr[slot, 0:2 * w, band:] = _dot_nt(qq, kc)

    def softmax(slot):
        s = s_scr[slot, 0:2 * w, :]
        p = jnp.exp(s - s.max(axis=-1, keepdims=True))
        den_scr[slot] = p.sum(axis=-1, keepdims=True)
        p_scr[slot, 0:2 * w, :] = p.astype(BF16)

    def values(r, slot):
        q0, k0, _ = offsets(r)
        o = (_dot(p_scr[slot, 0:2 * w, 0:band], v_ref[pl.ds(k0, band), :])
             + _dot(p_scr[slot, 0:2 * w, band:], vc))
        o_ref[pl.ds(q0, w), :] = unstack(o / den_scr[slot], w).astype(o_ref.dtype)

    n_groups = rows // NA_GROUP
    assert rows % NA_GROUP == 0 and n_groups % 2 == 0 and n_groups >= 4

    def step(u, parity, do_scores, do_softmax, do_values):
        for g in range(NA_GROUP):
            if do_softmax:
                softmax((1 - parity) * NA_GROUP + g)
        for g in range(NA_GROUP):
            if do_scores:
                scores(u * NA_GROUP + g, parity * NA_GROUP + g)
            if do_values:
                values((u - 2) * NA_GROUP + g, parity * NA_GROUP + g)

    step(0, 0, True, False, False)
    step(1, 1, True, True, False)

    def group_pair_body(u2, carry):
        step(2 * u2, 0, True, True, True)
        step(2 * u2 + 1, 1, True, True, True)
        return carry

    lax.fori_loop(1, n_groups // 2, group_pair_body, 0)
    step(n_groups, 0, False, True, True)
    step(n_groups + 1, 1, False, False, True)

    qc = qc_ref[...]
    nc = qc.shape[0]
    oc = _softmax_pv([_dot_nt(stack(qc), kc)], [vc])
    oc_ref[...] = unstack(oc, nc).astype(oc_ref.dtype)


def _bias_table(rpb, na_cols):
    heads, nr, nc = rpb.shape
    w = GRID_W
    jcol = np.arange(w)
    cstart = np.clip(jcol - na_cols // 2, 0, w - na_cols)
    kcol = np.arange(w)
    inside = (kcol[None, :] >= cstart[:, None]) & (kcol[None, :] < cstart[:, None] + na_cols)
    dc = kcol[None, :] - jcol[:, None] + (na_cols - 1)
    pick = jnp.asarray((dc[None] == np.arange(nc)[:, None, None]) & inside[None], F32)
    t2 = jnp.einsum("hdm,mqk->hdqk", rpb, pick, precision=lax.Precision.HIGHEST)
    t2 = jnp.where(inside[None, None], t2, NEG)
    t3 = jnp.concatenate([t2[:, :-1], t2[:, 1:]], axis=-1)
    t3 = t3.reshape(heads // 2, 2, nr - 1, w, 2 * w).transpose(0, 2, 1, 3, 4)
    return t3.reshape(heads // 2, nr - 1, 2 * w, 2 * w).astype(F32)


def _natten(q, k, v, bias, *, nb, seq_l, seq_c, heads, na_rows):
    t, d = q.shape
    head_dim = d // heads
    assert 2 * head_dim == LANE and 2 * GRID_W == LANE and seq_l % GRID_W == 0
    rows = seq_l // GRID_W
    assert (nb * seq_l) % seq_c == 0
    cblk = nb * seq_l // seq_c
    lat = lambda b, hp: (b, hp)
    ctx = lambda b, hp: (cblk + b, hp)
    return pl.pallas_call(
        functools.partial(_natten_kernel, rows=rows, na_rows=na_rows, head_dim=head_dim),
        grid=(nb, heads // 2),
        in_specs=[pl.BlockSpec((seq_l, LANE), lat),
                  pl.BlockSpec((seq_l, LANE), lat),
                  pl.BlockSpec((seq_l, LANE), lat),
                  pl.BlockSpec((seq_c, LANE), ctx),
                  pl.BlockSpec((seq_c, LANE), ctx),
                  pl.BlockSpec((seq_c, LANE), ctx),
                  pl.BlockSpec((None,) + bias.shape[1:], lambda b, hp: (hp, 0, 0, 0))],
        out_specs=[pl.BlockSpec((seq_l, LANE), lat),
                   pl.BlockSpec((seq_c, LANE), lambda b, hp: (b, hp))],
        out_shape=[jax.ShapeDtypeStruct((nb * seq_l, d), BF16),
                   jax.ShapeDtypeStruct((nb * seq_c, d), BF16)],
        scratch_shapes=[pltpu.VMEM((2 * NA_GROUP, 2 * GRID_W + SUBLANE, na_rows * GRID_W + seq_c), F32),
                        pltpu.VMEM((2 * NA_GROUP, 2 * GRID_W + 2 * SUBLANE, na_rows * GRID_W + seq_c), BF16),
                        pltpu.VMEM((2 * NA_GROUP, 2 * GRID_W, 1), F32)],
        compiler_params=_params(2),
        name="neighbourhood_attention",
    )(q, k, v, k, v, q, bias)


def _split_bf16(x):
    hi = x.astype(BF16)
    return hi, (x - hi.astype(F32)).astype(BF16)


def _route_kernel(al_ref, ac_ref, h_ref, mod_ref, g_ref, wo_ref, wr_ref, h_out_ref, f_ref, info_ref, rt_ref,
                  cnt_ref, wob_ref, *, n_experts, nlat):
    i = pl.program_id(0)

    @pl.when(i == 0)
    def _():
        cnt_ref[...] = jnp.zeros_like(cnt_ref)
        wob_ref[...] = wo_ref[...].astype(BF16)

    mod = mod_ref[...]
    attn = jnp.where(i < nlat, al_ref[...], ac_ref[...])
    h = h_ref[...] + mod[2:3] * _dot(attn, wob_ref[...])
    h_out_ref[...] = h
    f = _rms_mod(h, g_ref[...], mod[3:4], mod[4:5])
    f_ref[...] = f

    f_hi, f_lo = _split_bf16(f)
    z_hi = _dot(f_hi, wr_ref[...])
    z_lo = _dot(f_lo, wr_ref[...])
    logits = z_hi[:, 0:LANE] + (z_hi[:, LANE:] + z_lo[:, 0:LANE])
    lane_i = lax.broadcasted_iota(I32, logits.shape, 1)
    lane = lane_i.astype(F32)
    logits = jnp.where(lane_i < n_experts, logits, -jnp.inf)
    v1 = logits.max(axis=-1, keepdims=True)
    i1 = jnp.where(logits == v1, lane, float(LANE)).min(axis=-1, keepdims=True)
    rest = jnp.where(lane == i1, -jnp.inf, logits)
    v2 = rest.max(axis=-1, keepdims=True)
    i2 = jnp.where(rest == v2, lane, float(LANE)).min(axis=-1, keepdims=True)
    e2 = jnp.exp(v2 - v1)
    g1 = 1.0 / (1.0 + e2)
    g2 = e2 * g1

    sel1 = lane == i1
    sel2 = lane == i2
    onehot = jnp.where(sel1 | sel2, 1.0, 0.0)
    tr = lax.broadcasted_iota(I32, (TB, TB), 0)
    tc = lax.broadcasted_iota(I32, (TB, TB), 1)
    before = _dot(jnp.where(tc < tr, 1.0, 0.0).astype(BF16), onehot.astype(BF16)) + cnt_ref[0:1, :]
    r1 = jnp.where(sel1, before, 0.0).sum(axis=-1, keepdims=True)
    r2 = jnp.where(sel2, before, 0.0).sum(axis=-1, keepdims=True)
    cnt_ref[...] = cnt_ref[...] + onehot.sum(axis=0, keepdims=True)

    info = jnp.where(lane_i == 0, i1, 0.0)
    info = jnp.where(lane_i == 1, i2, info)
    info = jnp.where(lane_i == 2, r1, info)
    info = jnp.where(lane_i == 3, r2, info)
    info = jnp.where(lane_i == 4, g1, info)
    info = jnp.where(lane_i == 5, g2, info)
    info_ref[...] = info
    rt_ref[...] = jnp.transpose(info)[0:SUBLANE, :].astype(I32)


def _route(a_lat, a_ctx, h, mod, g, layer, w_out, w_router, *, n_rows, seq_l, nb):
    t, d = h.shape
    n_experts = w_router.shape[-1]
    wr = jnp.concatenate(_split_bf16(jnp.zeros((d, LANE), F32).at[:, :n_experts].set(w_router)), axis=1)
    const = lambda i: (0, 0)
    row = lambda i: (i, 0)
    n = n_rows
    assert n_rows % TB == 0 and seq_l % TB == 0 and a_ctx.shape[0] % TB == 0
    n_tiles = n_rows // TB
    tpl = seq_l // TB
    nlat = nb * tpl
    return pl.pallas_call(
        functools.partial(_route_kernel, n_experts=n_experts, nlat=nlat),
        grid=(n_tiles,),
        in_specs=[pl.BlockSpec((TB, d), lambda i: (jnp.minimum(i, nlat - 1), 0)),
                  pl.BlockSpec((TB, d), lambda i: (jnp.maximum(i - nlat, 0), 0)),
                  pl.BlockSpec((TB, d), row),
                  pl.BlockSpec((None, N_MOD, d), lambda i: (jnp.minimum(i // tpl, nb), 0, 0)),
                  pl.BlockSpec((1, d), const),
                  pl.BlockSpec((None,) + w_out.shape[1:], lambda i: (layer, 0, 0), pipeline_mode=pl.Buffered(1)),
                  pl.BlockSpec(wr.shape, const)],
        out_specs=[pl.BlockSpec((TB, d), row),
                   pl.BlockSpec((TB, d), row),
                   pl.BlockSpec((TB, LANE), row),
                   pl.BlockSpec((None, SUBLANE, TB), lambda i: (i, 0, 0)),
                   pl.BlockSpec((SUBLANE, LANE), const)],
        out_shape=[jax.ShapeDtypeStruct((n, d), F32),
                   jax.ShapeDtypeStruct((n, d), F32),
                   jax.ShapeDtypeStruct((n, LANE), F32),
                   jax.ShapeDtypeStruct((n_tiles, SUBLANE, TB), I32),
                   jax.ShapeDtypeStruct((SUBLANE, LANE), F32)],
        scratch_shapes=[pltpu.VMEM(w_out.shape[1:], BF16)],
        compiler_params=_params(1),
        name="attn_out_router",
    )(a_lat, a_ctx, h, mod, g.reshape(1, d), w_out, wr)


def _row_copies(src_ref, dst_ref, idx_ref, sem, scatter, wait_here=True):
    def issue(jj, wait):
        for u in range(SUBLANE):
            for k in range(TOP_K):
                row = idx_ref[0, k * TB + jj * SUBLANE + u]
                if scatter:
                    cp = pltpu.make_async_copy(src_ref.at[jj, pl.ds(u, 1)], dst_ref.at[pl.ds(row, 1)], sem)
                else:
                    cp = pltpu.make_async_copy(src_ref.at[pl.ds(row, 1)], dst_ref.at[k, jj, pl.ds(u, 1)], sem)
                if wait:
                    cp.wait()
                else:
                    cp.start(priority=k % 2)

    def start_body(jj, c):
        issue(jj, False)
        return c

    def wait_body(jj, c):
        issue(jj, True)
        return c

    lax.fori_loop(0, TB // SUBLANE, start_body, 0)
    if wait_here:
        lax.fori_loop(0, TB // SUBLANE, wait_body, 0)


def _dispatch_kernel(pad_ref, dest_ref, f_ref, xs_ref, zbuf, fbuf, sem, zsem, fsem, *, n_fills, n_tiles):
    i = pl.program_id(0)
    tile_groups = TB // SUBLANE

    def stage(tile):
        s = lax.rem(tile, 3)
        return pltpu.make_async_copy(f_ref.at[pl.ds(tile * tile_groups, tile_groups)], fbuf.at[s], fsem.at[s])

    def wait_rows(tile):
        s = lax.rem(tile, 3)
        for _ in range(TOP_K):
            pltpu.make_async_copy(fbuf.at[s], fbuf.at[s], sem.at[s]).wait()

    @pl.when(i == 0)
    def _():
        stage(i).start()
        zbuf[...] = jnp.zeros_like(zbuf)

        def fill(e):
            return pltpu.make_async_copy(zbuf, xs_ref.at[pl.ds(pl.multiple_of(pad_ref[e] * TM, TM), TM)], zsem)

        for e in range(n_fills):
            @pl.when(pad_ref[e] >= 0)
            def _():
                fill(e).start()
        for e in range(n_fills):
            @pl.when(pad_ref[e] >= 0)
            def _():
                fill(e).wait()

    stage(i).wait()

    @pl.when(i + 1 < n_tiles)
    def _():
        stage(i + 1).start()

    slot = lax.rem(i, 3)
    _row_copies(fbuf.at[slot], xs_ref, dest_ref, sem.at[slot], scatter=True, wait_here=False)

    @pl.when(i > 0)
    def _():
        wait_rows(i - 1)

    @pl.when(i == n_tiles - 1)
    def _():
        wait_rows(i)


def _dispatch(f, dest, pad_at, *, n_tiles, n_rows):
    n_fills = pad_at.shape[0]
    n, d = f.shape
    grid_spec = pltpu.PrefetchScalarGridSpec(
        num_scalar_prefetch=1,
        grid=(n_tiles,),
        in_specs=[pl.BlockSpec((None, 1, TOP_K * TB), lambda i, pad: (i, 0, 0), memory_space=pltpu.SMEM),
                  pl.BlockSpec(memory_space=pl.ANY)],
        out_specs=pl.BlockSpec(memory_space=pl.ANY),
        scratch_shapes=[pltpu.VMEM((TM, d), F32), pltpu.VMEM((3, TB // SUBLANE, SUBLANE, d), F32),
                        pltpu.SemaphoreType.DMA((3,)), pltpu.SemaphoreType.DMA(()), pltpu.SemaphoreType.DMA((3,))],
    )
    return pl.pallas_call(
        functools.partial(_dispatch_kernel, n_fills=n_fills, n_tiles=n_tiles),
        grid_spec=grid_spec,
        out_shape=jax.ShapeDtypeStruct((n_rows, d), F32),
        compiler_params=_params(1),
        name="moe_dispatch",
    )(pad_at, dest, f.reshape(n // SUBLANE, SUBLANE, d))


def _experts_kernel(te_ref, tv_ref, nu_ref, x_ref, wg_ref, wu_ref, wd_ref, o_ref, xb_ref, wg_buf, wu_buf, wd_buf, wsem,
                    *, layer, n_chunks):
    i = pl.program_id(0)
    n_used = nu_ref[0]
    used = i < n_used
    valid = tv_ref[i]

    def copies(tile, j, slot):
        e = te_ref[tile]
        cols = pl.ds(pl.multiple_of(j * FC, FC), FC)
        return (pltpu.make_async_copy(wg_ref.at[layer, e, :, cols], wg_buf.at[slot], wsem.at[0, slot]),
                pltpu.make_async_copy(wu_ref.at[layer, e, :, cols], wu_buf.at[slot], wsem.at[1, slot]),
                pltpu.make_async_copy(wd_ref.at[layer, e, cols, :], wd_buf.at[slot], wsem.at[2, slot]))

    depth = W_BUFFERS - 1

    def start_ahead(c, ahead):
        t, j = lax.div(c + ahead, n_chunks), lax.rem(c + ahead, n_chunks)

        @pl.when(t < n_used)
        def _():
            for cp in copies(t, j, lax.rem(c + ahead, W_BUFFERS)):
                cp.start()

    @pl.when(i == 0)
    def _():
        for ahead in range(depth):
            start_ahead(0, ahead)

    o_ref[...] = jnp.zeros_like(o_ref)

    def compute(n_rows):
        xb_ref[0:n_rows, :] = x_ref[0:n_rows, :].astype(BF16)

        def chunk_body(j, carry):
            c = i * n_chunks + j
            slot = lax.rem(c, W_BUFFERS)
            for cp in copies(i, j, slot):
                cp.wait()
            start_ahead(c, depth)

            xb = xb_ref[0:n_rows, :]
            act = _silu(_dot(xb, wg_buf[slot].astype(BF16))) * _dot(xb, wu_buf[slot].astype(BF16))
            o_ref[0:n_rows, :] += _dot(act.astype(BF16), wd_buf[slot].astype(BF16))
            return carry

        lax.fori_loop(0, n_chunks, chunk_body, 0)

    for n_rows in range(SUB, TM + 1, SUB):
        @pl.when(used & (valid > n_rows - SUB) & (valid <= n_rows))
        def _():
            compute(n_rows)


def _experts(xs, tile_expert, tile_valid, n_used, layer, we_gate, we_up, we_down):
    p, d = xs.shape
    dfe = we_gate.shape[-1]
    assert dfe % FC == 0 and p % TM == 0 and TM % SUB == 0
    n_chunks = dfe // FC
    n_tiles = p // TM

    grid_spec = pltpu.PrefetchScalarGridSpec(
        num_scalar_prefetch=3,
        grid=(n_tiles,),
        in_specs=[pl.BlockSpec((TM, d), lambda i, te, tv, nu: (jnp.minimum(i, nu[0] - 1), 0)),
                  pl.BlockSpec(memory_space=pl.ANY),
                  pl.BlockSpec(memory_space=pl.ANY),
                  pl.BlockSpec(memory_space=pl.ANY)],
        out_specs=pl.BlockSpec((TM, d), lambda i, te, tv, nu: (i, 0)),
        scratch_shapes=[pltpu.VMEM((TM, d), BF16), pltpu.VMEM((W_BUFFERS, d, FC), F32),
                        pltpu.VMEM((W_BUFFERS, d, FC), F32), pltpu.VMEM((W_BUFFERS, FC, d), F32),
                        pltpu.SemaphoreType.DMA((3, W_BUFFERS))],
    )
    return pl.pallas_call(
        functools.partial(_experts_kernel, layer=layer, n_chunks=n_chunks),
        grid_spec=grid_spec,
        out_shape=jax.ShapeDtypeStruct((p, d), F32),
        compiler_params=_params(1),
        name="expert_swiglu",
    )(tile_expert, tile_valid, n_used, xs, we_gate, we_up, we_down)


def _combine_kernel(dest_ref, ys_ref, info_ref, h_ref, mod_ref, gf_ref, o_ref, ybuf, sem, *, final_norm, n_tiles):
    i = pl.program_id(0)
    slot = lax.rem(i, 2)

    @pl.when(i < n_tiles)
    def _():
        _row_copies(ys_ref, ybuf.at[slot], dest_ref, sem.at[slot], scatter=False, wait_here=False)

    @pl.when(i > 0)
    def _():
        done = ybuf.at[1 - slot]
        pltpu.make_async_copy(done, done, sem.at[1 - slot]).wait()
        info = info_ref[...]
        d = h_ref.shape[1]
        mix = info[:, 4:5] * done[0].reshape(TB, d) + info[:, 5:6] * done[1].reshape(TB, d)
        h = h_ref[...] + mod_ref[...][5:6] * mix
        if final_norm:
            h = (h * lax.rsqrt(jnp.mean(h * h, axis=-1, keepdims=True) + EPS)) * gf_ref[...]
        o_ref[...] = h


def _combine(ys, dest, info, h, mod, g_final, *, n_tiles, tpl, nb, final_norm):
    n, d = h.shape
    prev = lambda i: (jnp.maximum(i - 1, 0), 0)
    return pl.pallas_call(
        functools.partial(_combine_kernel, final_norm=final_norm, n_tiles=n_tiles),
        grid=(n_tiles + 1,),
        in_specs=[pl.BlockSpec((None, 1, TOP_K * TB), lambda i: (jnp.minimum(i, n_tiles - 1), 0, 0),
                               memory_space=pltpu.SMEM),
                  pl.BlockSpec(memory_space=pl.ANY),
                  pl.BlockSpec((TB, LANE), prev),
                  pl.BlockSpec((TB, d), prev),
                  pl.BlockSpec((None, N_MOD, d), lambda i: (jnp.minimum(jnp.maximum(i - 1, 0) // tpl, nb), 0, 0)),
                  pl.BlockSpec((1, d), lambda i: (0, 0))],
        out_specs=pl.BlockSpec((TB, d), prev),
        out_shape=jax.ShapeDtypeStruct((n_tiles * TB, d), F32),
        scratch_shapes=[pltpu.VMEM((2, TOP_K, TB // SUBLANE, SUBLANE, d), F32), pltpu.SemaphoreType.DMA((2,))],
        compiler_params=_params(1),
        name="moe_combine",
    )(dest, ys, info, h, mod, g_final.reshape(1, d))


def _moe(f, info, rt, counts, h, mod, g_final, layer, we_gate, we_up, we_down, *, n_tiles, tpl, nb, final_norm):
    n = f.shape[0]
    n_experts = we_gate.shape[1]
    n_xtiles = -(-TOP_K * n // TM) + n_experts
    n_rows = n_xtiles * TM

    cnt = counts[0, :n_experts].astype(I32)
    tiles_e = (cnt + TM - 1) // TM
    tile_end = jnp.cumsum(tiles_e)
    starts = (tile_end - tiles_e) * TM
    n_used = tile_end[-1:]
    tile_ids = jnp.arange(n_xtiles, dtype=I32)
    tile_expert = jnp.minimum(jnp.sum(tile_ids[:, None] >= tile_end[None, :], axis=1), n_experts - 1).astype(I32)
    tile_valid = jnp.clip(cnt[tile_expert] - (tile_ids * TM - starts[tile_expert]), 0, TM).astype(I32)
    choice = rt[:, 0:TOP_K, :]
    rank = rt[:, TOP_K:2 * TOP_K, :]
    start_of = sum(jnp.where(choice == e, starts[e], 0) for e in range(n_experts))
    dest = (start_of + rank).reshape(n_tiles, 1, TOP_K * TB)
    tail = n_used + jnp.arange(n_experts, dtype=I32)
    pad_at = jnp.concatenate([jnp.where(tiles_e > 0, tile_end - 1, -1),
                              jnp.where(tail < n_xtiles, tail, -1)]).astype(I32)

    xs = _dispatch(f, dest, pad_at, n_tiles=n_tiles, n_rows=n_rows)
    ys = _experts(xs, tile_expert, tile_valid, n_used, layer, we_gate, we_up, we_down)
    return _combine(ys, dest, info, h, mod, g_final, n_tiles=n_tiles, tpl=tpl, nb=nb, final_norm=final_norm)


def kernel(x, c, ctx, c_ctx, w_mod, b_mod, g_mix, g_ffn, g_final, w_in_ab, conv_w, pool_w, pool_scale,
           w_out_ab, w_ff_gate, w_ff_up, w_ff_down, w_qkv, rpb, w_out_na, w_router, we_gate, we_up, we_down):
    nb, seq_l, d = x.shape
    seq_c = ctx.shape[1]
    depth = w_mod.shape[0]
    heads = rpb.shape[1]
    na_rows = (rpb.shape[2] + 1) // 2
    na_cols = (rpb.shape[3] + 1) // 2
    assert seq_l % TB == 0 and seq_c % TB == 0 and nb + 1 <= SUBLANE and depth % 2 == 0
    tpl = seq_l // TB
    nlat = nb * tpl
    nall = nlat + nb * seq_c // TB

    cond = jnp.zeros((SUBLANE, d), F32).at[:nb].set(c).at[nb].set(c_ctx)
    mods = _adaln(cond, w_mod, b_mod).reshape(depth, SUBLANE, N_MOD, d)

    bf = lambda a: a.astype(BF16)
    h = out = None
    for i in range(depth):
        j = i // 2
        last = i == depth - 1
        mod = mods[i]
        if i % 2 == 0:
            srcs = (x.reshape(nb * seq_l, d), ctx.reshape(nb * seq_c, d), 0) if i == 0 else (h, h, nlat)
            h = _mixer(*srcs, mod, g_mix[i], j, w_in_ab, conv_w, pool_w, pool_scale[j], w_out_ab,
                       n_tiles=nall, nlat=nlat, nb=nb, seq_l=seq_l, seq_c=seq_c)
            h = _ffn(h, mod, g_ffn[i], bf(w_ff_gate[j]), bf(w_ff_up[j]), bf(w_ff_down[j]),
                     n_rows=nall * TB, seq_l=seq_l, nb=nb)
        else:
            n_tiles = nlat if last else nall
            q, k, v = _qkv(h, mod, g_mix[i], j, w_qkv, n_rows=nall * TB, seq_l=seq_l, nb=nb,
                           qscale=(d // heads) ** -0.5)
            o_lat, o_ctx = _natten(q, k, v, _bias_table(rpb[j], na_cols), nb=nb, seq_l=seq_l, seq_c=seq_c,
                                   heads=heads, na_rows=na_rows)
            h, f, info, rt, counts = _route(o_lat, o_ctx, h, mod, g_ffn[i], j, w_out_na, w_router[j],
                                            n_rows=n_tiles * TB, seq_l=seq_l, nb=nb)
            h = _moe(f, info, rt, counts, h, mod, g_final, j, we_gate, we_up, we_down,
                     n_tiles=n_tiles, tpl=tpl, nb=nb, final_norm=last)
            if last:
                out = h
    return out.reshape(nb, seq_l, d)
```

```python
import functools

import numpy as np
import jax
import jax.numpy as jnp
from jax import lax
from jax.experimental import pallas as pl
from jax.experimental.pallas import tpu as pltpu

F32 = jnp.float32
BF16 = jnp.bfloat16
I32 = jnp.int32

GRID_W = 64
POOL_WINDOWS = (2, 4, 8, 16)
N_MOD = 6
TOP_K = 2
EPS = 1e-6
NEG = -1e30

LANE = 128
SUBLANE = 8
TB = 256
TBD = 512
TBR = 512
HALO = SUBLANE
TM = 1024
SUB = 256
FC = 512
W_BUFFERS = 4
NA_GROUP = 4
ADALN_COLS = 1536
FFN_COLS = 768
VMEM_LIMIT = 56 * 1024 * 1024


def _params(n_axes):
    return pltpu.CompilerParams(dimension_semantics=("arbitrary",) * n_axes,
                                vmem_limit_bytes=VMEM_LIMIT)


def _rms_mod(x, g, shift, scale):
    y = x * lax.rsqrt(jnp.mean(x * x, axis=-1, keepdims=True) + EPS)
    return (y * g) * (1.0 + scale) + shift


def _silu(x):
    return x * jax.nn.sigmoid(x)


def _dot(a, b):
    return jnp.dot(a, b, preferred_element_type=F32)


def _dot_nt(a, b):
    return lax.dot_general(a, b, (((1,), (1,)), ((), ())), preferred_element_type=F32)


def _adaln_kernel(cond_ref, w_ref, b_ref, o_ref):
    s = _silu(cond_ref[...]).astype(BF16)
    o_ref[...] = _dot(s, w_ref[...].astype(BF16)) + b_ref[...]


def _adaln(cond, w_mod, b_mod):
    depth, d, nd = w_mod.shape
    tn = ADALN_COLS
    assert nd % tn == 0
    return pl.pallas_call(
        _adaln_kernel,
        grid=(depth, nd // tn),
        in_specs=[pl.BlockSpec((SUBLANE, d), lambda l, n: (0, 0)),
                  pl.BlockSpec((None, d, tn), lambda l, n: (l, 0, n)),
                  pl.BlockSpec((None, 1, tn), lambda l, n: (l, 0, n))],
        out_specs=pl.BlockSpec((None, SUBLANE, tn), lambda l, n: (l, 0, n)),
        out_shape=jax.ShapeDtypeStruct((depth, SUBLANE, nd), F32),
        compiler_params=_params(2),
        name="adaln",
    )(cond, w_mod, b_mod.reshape(depth, 1, nd))


def _mixer_kernel(ap_ref, ac_ref, an_ref, bp_ref, bc_ref, bn_ref, mod_ref, g_ref, win_ref, cw_ref, pw_ref, ps_ref,
                  wout_ref, o_ref, xs_ref, z_ref, p_ref, winb_ref, woutb_ref,
                  *, nlat, tpl, tpc, seq_l, seq_c, d_conv, pool_group):
    i = pl.program_id(0)
    is_ctx = i >= nlat
    pos = jnp.where(is_ctx, lax.rem(i - nlat, tpc), lax.rem(i, tpl)) * TB
    seq_len = jnp.where(is_ctx, seq_c, seq_l)
    ext = TB + 2 * HALO

    @pl.when(i == 0)
    def _():
        winb_ref[...] = win_ref[...].astype(BF16)
        woutb_ref[...] = wout_ref[...].astype(BF16)

    hc = jnp.where(is_ctx, bc_ref[...], ac_ref[...])
    xs_ref[0:HALO, :] = jnp.where(is_ctx, bp_ref[...], ap_ref[...])
    xs_ref[HALO:HALO + TB, :] = hc
    xs_ref[HALO + TB:ext, :] = jnp.where(is_ctx, bn_ref[...], an_ref[...])
    mod = mod_ref[...]
    a = _rms_mod(xs_ref[...], g_ref[...], mod[0:1], mod[1:2]).astype(BF16)
    u = _dot(a, winb_ref[...])
    srow = lax.broadcasted_iota(I32, (ext, 1), 0) + (pos - HALO)
    u = jnp.where((srow >= 0) & (srow < seq_len), u, 0.0)

    z_ref[...] = u[:, d_conv:2 * d_conv] * u[:, 2 * d_conv:3 * d_conv]
    p_ref[...] = u[:, 3 * d_conv:]
    cw = cw_ref[...]
    conv = (z_ref[HALO - 1:HALO - 1 + TB, :] * cw[0:1] + z_ref[HALO:HALO + TB, :] * cw[1:2]
            + z_ref[HALO + 1:HALO + 1 + TB, :] * cw[2:3])
    pieces = [u[HALO:HALO + TB, 0:d_conv] * conv]

    spos = srow[HALO:HALO + TB]
    ps = ps_ref[...]
    for g, win in enumerate(POOL_WINDOWS):
        lo, hi = win // 2, win - 1 - win // 2
        cols = slice(g * pool_group, (g + 1) * pool_group)
        acc = p_ref[HALO - lo:HALO - lo + TB, cols]
        for dlt in range(-lo + 1, hi + 1):
            acc = acc + p_ref[HALO + dlt:HALO + dlt + TB, cols]
        cnt = jnp.minimum(spos + hi, seq_len - 1) - jnp.maximum(spos - lo, 0) + 1
        diff = acc / cnt.astype(F32) - p_ref[HALO:HALO + TB, cols]
        pieces.append(_dot(diff.astype(BF16), pw_ref[g].astype(BF16)) * ps[:, cols])
    cat = jnp.concatenate(pieces, axis=-1).astype(BF16)
    o_ref[...] = hc + mod[2:3] * _dot(cat, woutb_ref[...])


def _mixer(src_a, src_b, b_off, mod, g, layer, w_in, conv_w, pool_w, pool_scale, w_out,
           *, n_tiles, nlat, nb, seq_l, seq_c):
    d = src_a.shape[1]
    d_conv = conv_w.shape[-1]
    d_pool = pool_scale.shape[-1]
    pool_group = pool_w.shape[-1]
    assert pool_group % LANE == 0 and d_conv % LANE == 0 and len(POOL_WINDOWS) == pool_w.shape[1]
    tpl, tpc = seq_l // TB, seq_c // TB
    ext = TB + 2 * HALO
    hb = TB // HALO
    a_last = src_a.shape[0] // HALO - 1
    b_last = src_b.shape[0] // HALO - 1
    kern = functools.partial(_mixer_kernel, nlat=nlat, tpl=tpl, tpc=tpc, seq_l=seq_l, seq_c=seq_c,
                             d_conv=d_conv, pool_group=pool_group)
    a_tile = lambda i: jnp.minimum(i, nlat - 1)
    b_tile = lambda i: jnp.maximum(i - nlat, 0) + b_off
    once = pl.Buffered(1)
    return pl.pallas_call(
        kern,
        grid=(n_tiles,),
        in_specs=[pl.BlockSpec((HALO, d), lambda i: (jnp.maximum(a_tile(i) * hb - 1, 0), 0)),
                  pl.BlockSpec((TB, d), lambda i: (a_tile(i), 0)),
                  pl.BlockSpec((HALO, d), lambda i: (jnp.minimum((a_tile(i) + 1) * hb, a_last), 0)),
                  pl.BlockSpec((HALO, d), lambda i: (jnp.maximum(b_tile(i) * hb - 1, 0), 0)),
                  pl.BlockSpec((TB, d), lambda i: (b_tile(i), 0)),
                  pl.BlockSpec((HALO, d), lambda i: (jnp.minimum((b_tile(i) + 1) * hb, b_last), 0)),
                  pl.BlockSpec((None, N_MOD, d), lambda i: (jnp.minimum(i // tpl, nb), 0, 0)),
                  pl.BlockSpec((1, d), lambda i: (0, 0)),
                  pl.BlockSpec((None,) + w_in.shape[1:], lambda i: (layer, 0, 0), pipeline_mode=once),
                  pl.BlockSpec((None,) + conv_w.shape[1:], lambda i: (layer, 0, 0)),
                  pl.BlockSpec((None,) + pool_w.shape[1:], lambda i: (layer, 0, 0, 0)),
                  pl.BlockSpec((1, d_pool), lambda i: (0, 0)),
                  pl.BlockSpec((None,) + w_out.shape[1:], lambda i: (layer, 0, 0), pipeline_mode=once)],
        out_specs=pl.BlockSpec((TB, d), lambda i: (i, 0)),
        out_shape=jax.ShapeDtypeStruct((n_tiles * TB, d), F32),
        scratch_shapes=[pltpu.VMEM((ext, d), F32), pltpu.VMEM((ext, d_conv), F32), pltpu.VMEM((ext, d_pool), F32),
                        pltpu.VMEM(w_in.shape[1:], BF16), pltpu.VMEM(w_out.shape[1:], BF16)],
        compiler_params=_params(1),
        name="conv_pool_mixer",
    )(src_a, src_a, src_a, src_b, src_b, src_b, mod, g.reshape(1, d), w_in, conv_w, pool_w,
      pool_scale.reshape(1, d_pool), w_out)


def _ffn_kernel(h_ref, mod_ref, g_ref, wg_ref, wu_ref, wd_ref, o_ref, act_ref, *, chunks):
    mod = mod_ref[...]
    h = h_ref[...]
    f = _rms_mod(h, g_ref[...], mod[3:4], mod[4:5]).astype(BF16)
    for c0, c1 in chunks:
        gate = _dot(f, wg_ref[:, c0:c1])
        up = _dot(f, wu_ref[:, c0:c1])
        act_ref[:, c0:c1] = (_silu(gate) * up).astype(BF16)
    o_ref[...] = h + mod[5:6] * _dot(act_ref[...], wd_ref[...])


def _ffn(h, mod, g, wg, wu, wd, *, n_rows, seq_l, nb):
    t, d = h.shape
    dff = wg.shape[-1]
    step = FFN_COLS
    chunks = tuple((c, min(c + step, dff)) for c in range(0, dff, step))
    const = lambda i: (0, 0)
    once = pl.Buffered(1)
    assert n_rows % TBD == 0 and seq_l % TBD == 0
    tpl = seq_l // TBD
    return pl.pallas_call(
        functools.partial(_ffn_kernel, chunks=chunks),
        grid=(n_rows // TBD,),
        in_specs=[pl.BlockSpec((TBD, d), lambda i: (i, 0)),
                  pl.BlockSpec((None, N_MOD, d), lambda i: (jnp.minimum(i // tpl, nb), 0, 0)),
                  pl.BlockSpec((1, d), const),
                  pl.BlockSpec(wg.shape, const, pipeline_mode=once),
                  pl.BlockSpec(wu.shape, const, pipeline_mode=once),
                  pl.BlockSpec(wd.shape, const, pipeline_mode=once)],
        out_specs=pl.BlockSpec((TBD, d), lambda i: (i, 0)),
        out_shape=jax.ShapeDtypeStruct((n_rows, d), F32),
        scratch_shapes=[pltpu.VMEM((TBD, dff), BF16)],
        compiler_params=_params(1),
        name="dense_swiglu",
    )(h, mod, g.reshape(1, d), wg, wu, wd)


def _qkv_kernel(h_ref, mod_ref, g_ref, w_ref, q_ref, k_ref, v_ref, wb_ref, *, d, qscale):
    @pl.when(pl.program_id(0) == 0)
    def _():
        wb_ref[...] = w_ref[...].astype(BF16)

    mod = mod_ref[...]
    a = _rms_mod(h_ref[...], g_ref[...], mod[0:1], mod[1:2]).astype(BF16)
    q_ref[...] = (_dot(a, wb_ref[:, 0:d]) * qscale).astype(BF16)
    k_ref[...] = _dot(a, wb_ref[:, d:2 * d]).astype(BF16)
    v_ref[...] = _dot(a, wb_ref[:, 2 * d:3 * d]).astype(BF16)


def _qkv(h, mod, g, layer, w, *, n_rows, seq_l, nb, qscale):
    t, d = h.shape
    const = lambda i: (0, 0)
    out = jax.ShapeDtypeStruct((n_rows, d), BF16)
    assert n_rows % TBD == 0 and seq_l % TBD == 0
    tpl = seq_l // TBD
    return pl.pallas_call(
        functools.partial(_qkv_kernel, d=d, qscale=qscale),
        grid=(n_rows // TBD,),
        in_specs=[pl.BlockSpec((TBD, d), lambda i: (i, 0)),
                  pl.BlockSpec((None, N_MOD, d), lambda i: (jnp.minimum(i // tpl, nb), 0, 0)),
                  pl.BlockSpec((1, d), const),
                  pl.BlockSpec((None,) + w.shape[1:], lambda i: (layer, 0, 0), pipeline_mode=pl.Buffered(1))],
        out_specs=[pl.BlockSpec((TBD, d), lambda i: (i, 0))] * 3,
        out_shape=[out, out, out],
        scratch_shapes=[pltpu.VMEM(w.shape[1:], BF16)],
        compiler_params=_params(1),
        name="qkv_proj",
    )(h, mod, g.reshape(1, d), w)


def _softmax_pv(s_parts, v_parts):
    m = s_parts[0].max(axis=-1, keepdims=True)
    for s in s_parts[1:]:
        m = jnp.maximum(m, s.max(axis=-1, keepdims=True))
    den = 0.0
    out = 0.0
    for s, v in zip(s_parts, v_parts):
        p = jnp.exp(s - m)
        den = den + p.sum(axis=-1, keepdims=True)
        out = out + _dot(p.astype(BF16), v)
    return out / den


def _natten_kernel(q_ref, k_ref, v_ref, kc_ref, vc_ref, qc_ref, bias_ref, o_ref, oc_ref,
                   s_scr, p_scr, den_scr, *, rows, na_rows, head_dim):
    w = GRID_W
    band = na_rows * w
    lane = lax.broadcasted_iota(I32, (1, LANE), 1)
    head0 = lane < head_dim
    zero = jnp.zeros((), BF16)
    kc = kc_ref[...]
    vc = vc_ref[...]

    def stack(q):
        return jnp.concatenate([jnp.where(head0, q, zero), jnp.where(head0, zero, q)], axis=0)

    def unstack(o, n):
        return jnp.where(head0, o[0:n], o[n:2 * n])

    def offsets(r):
        start = jnp.clip(r - na_rows // 2, 0, rows - na_rows)
        return pl.multiple_of(r * w, w), pl.multiple_of(start * w, w), start - r + (na_rows - 1)

    def scores(r, slot):
        q0, k0, d0 = offsets(r)
        qq = stack(q_ref[pl.ds(q0, w), :])
        bias = jnp.concatenate([bias_ref[d0 + 2 * p] for p in range(band // LANE)], axis=-1)
        s_scr[slot, :, 0:band] = _dot_nt(qq, k_ref[pl.ds(k0, band), :]) + bias
        s_scr[slot, :, band:] = _dot_nt(qq, kc)

    def softmax(slot):
        s = s_scr[slot]
        p = jnp.exp(s - s.max(axis=-1, keepdims=True))
        den_scr[slot] = p.sum(axis=-1, keepdims=True)
        p_scr[slot] = p.astype(BF16)

    def values(r, slot):
        q0, k0, _ = offsets(r)
        o = _dot(p_scr[slot, :, 0:band], v_ref[pl.ds(k0, band), :]) + _dot(p_scr[slot, :, band:], vc)
        o_ref[pl.ds(q0, w), :] = unstack(o / den_scr[slot], w).astype(o_ref.dtype)

    n_groups = rows // NA_GROUP
    assert rows % NA_GROUP == 0 and n_groups % 2 == 0 and n_groups >= 4

    def step(u, parity, do_scores, do_softmax, do_values):
        for g in range(NA_GROUP):
            if do_softmax:
                softmax((1 - parity) * NA_GROUP + g)
        for g in range(NA_GROUP):
            if do_scores:
                scores(u * NA_GROUP + g, parity * NA_GROUP + g)
            if do_values:
                values((u - 2) * NA_GROUP + g, parity * NA_GROUP + g)

    step(0, 0, True, False, False)
    step(1, 1, True, True, False)

    def group_pair_body(u2, carry):
        step(2 * u2, 0, True, True, True)
        step(2 * u2 + 1, 1, True, True, True)
        return carry

    lax.fori_loop(1, n_groups // 2, group_pair_body, 0)
    step(n_groups, 0, False, True, True)
    step(n_groups + 1, 1, False, False, True)

    qc = qc_ref[...]
    nc = qc.shape[0]
    oc = _softmax_pv([_dot_nt(stack(qc), kc)], [vc])
    oc_ref[...] = unstack(oc, nc).astype(oc_ref.dtype)


def _bias_table(rpb, na_cols):
    heads, nr, nc = rpb.shape
    w = GRID_W
    jcol = np.arange(w)
    cstart = np.clip(jcol - na_cols // 2, 0, w - na_cols)
    kcol = np.arange(w)
    inside = (kcol[None, :] >= cstart[:, None]) & (kcol[None, :] < cstart[:, None] + na_cols)
    dc = kcol[None, :] - jcol[:, None] + (na_cols - 1)
    pick = jnp.asarray((dc[None] == np.arange(nc)[:, None, None]) & inside[None], F32)
    t2 = jnp.einsum("hdm,mqk->hdqk", rpb, pick, precision=lax.Precision.HIGHEST)
    t2 = jnp.where(inside[None, None], t2, NEG)
    t3 = jnp.concatenate([t2[:, :-1], t2[:, 1:]], axis=-1)
    t3 = t3.reshape(heads // 2, 2, nr - 1, w, 2 * w).transpose(0, 2, 1, 3, 4)
    return t3.reshape(heads // 2, nr - 1, 2 * w, 2 * w).astype(F32)


def _natten(q, k, v, bias, *, nb, seq_l, seq_c, heads, na_rows):
    t, d = q.shape
    head_dim = d // heads
    assert 2 * head_dim == LANE and 2 * GRID_W == LANE and seq_l % GRID_W == 0
    rows = seq_l // GRID_W
    assert (nb * seq_l) % seq_c == 0
    cblk = nb * seq_l // seq_c
    lat = lambda b, hp: (b, hp)
    ctx = lambda b, hp: (cblk + b, hp)
    return pl.pallas_call(
        functools.partial(_natten_kernel, rows=rows, na_rows=na_rows, head_dim=head_dim),
        grid=(nb, heads // 2),
        in_specs=[pl.BlockSpec((seq_l, LANE), lat),
                  pl.BlockSpec((seq_l, LANE), lat),
                  pl.BlockSpec((seq_l, LANE), lat),
                  pl.BlockSpec((seq_c, LANE), ctx),
                  pl.BlockSpec((seq_c, LANE), ctx),
                  pl.BlockSpec((seq_c, LANE), ctx),
                  pl.BlockSpec((None,) + bias.shape[1:], lambda b, hp: (hp, 0, 0, 0))],
        out_specs=[pl.BlockSpec((seq_l, LANE), lat),
                   pl.BlockSpec((seq_c, LANE), lambda b, hp: (b, hp))],
        out_shape=[jax.ShapeDtypeStruct((nb * seq_l, d), BF16),
                   jax.ShapeDtypeStruct((nb * seq_c, d), BF16)],
        scratch_shapes=[pltpu.VMEM((2 * NA_GROUP, 2 * GRID_W, na_rows * GRID_W + seq_c), F32),
                        pltpu.VMEM((2 * NA_GROUP, 2 * GRID_W, na_rows * GRID_W + seq_c), BF16),
                        pltpu.VMEM((2 * NA_GROUP, 2 * GRID_W, 1), F32)],
        compiler_params=_params(2),
        name="neighbourhood_attention",
    )(q, k, v, k, v, q, bias)


def _split_bf16(x):
    hi = x.astype(BF16)
    return hi, (x - hi.astype(F32)).astype(BF16)


def _route_kernel(al_ref, ac_ref, h_ref, mod_ref, g_ref, wo_ref, wr_ref, h_out_ref, f_ref, info_ref, rt_ref,
                  cnt_ref, wob_ref, *, n_experts, nlat):
    i = pl.program_id(0)

    @pl.when(i == 0)
    def _():
        cnt_ref[...] = jnp.zeros_like(cnt_ref)
        wob_ref[...] = wo_ref[...].astype(BF16)

    mod = mod_ref[...]
    attn = jnp.where(i < nlat, al_ref[...], ac_ref[...])
    h = h_ref[...] + mod[2:3] * _dot(attn, wob_ref[...])
    h_out_ref[...] = h
    f = _rms_mod(h, g_ref[...], mod[3:4], mod[4:5])
    f_ref[...] = f

    f_hi, f_lo = _split_bf16(f)
    z_hi = _dot(f_hi, wr_ref[...])
    z_lo = _dot(f_lo, wr_ref[...])
    logits = z_hi[:, 0:LANE] + (z_hi[:, LANE:] + z_lo[:, 0:LANE])
    lane_i = lax.broadcasted_iota(I32, logits.shape, 1)
    lane = lane_i.astype(F32)
    logits = jnp.where(lane_i < n_experts, logits, -jnp.inf)
    v1 = logits.max(axis=-1, keepdims=True)
    i1 = jnp.where(logits == v1, lane, float(LANE)).min(axis=-1, keepdims=True)
    rest = jnp.where(lane == i1, -jnp.inf, logits)
    v2 = rest.max(axis=-1, keepdims=True)
    i2 = jnp.where(rest == v2, lane, float(LANE)).min(axis=-1, keepdims=True)
    e2 = jnp.exp(v2 - v1)
    g1 = 1.0 / (1.0 + e2)
    g2 = e2 * g1

    sel1 = lane == i1
    sel2 = lane == i2
    onehot = jnp.where(sel1 | sel2, 1.0, 0.0)
    tr = lax.broadcasted_iota(I32, (TB, TB), 0)
    tc = lax.broadcasted_iota(I32, (TB, TB), 1)
    before = _dot(jnp.where(tc < tr, 1.0, 0.0).astype(BF16), onehot.astype(BF16)) + cnt_ref[0:1, :]
    r1 = jnp.where(sel1, before, 0.0).sum(axis=-1, keepdims=True)
    r2 = jnp.where(sel2, before, 0.0).sum(axis=-1, keepdims=True)
    cnt_ref[...] = cnt_ref[...] + onehot.sum(axis=0, keepdims=True)

    info = jnp.where(lane_i == 0, i1, 0.0)
    info = jnp.where(lane_i == 1, i2, info)
    info = jnp.where(lane_i == 2, r1, info)
    info = jnp.where(lane_i == 3, r2, info)
    info = jnp.where(lane_i == 4, g1, info)
    info = jnp.where(lane_i == 5, g2, info)
    info_ref[...] = info
    rt_ref[...] = jnp.transpose(info)[0:SUBLANE, :].astype(I32)


def _route(a_lat, a_ctx, h, mod, g, layer, w_out, w_router, *, n_rows, seq_l, nb):
    t, d = h.shape
    n_experts = w_router.shape[-1]
    wr = jnp.concatenate(_split_bf16(jnp.zeros((d, LANE), F32).at[:, :n_experts].set(w_router)), axis=1)
    const = lambda i: (0, 0)
    row = lambda i: (i, 0)
    n = n_rows
    assert n_rows % TB == 0 and seq_l % TB == 0 and a_ctx.shape[0] % TB == 0
    n_tiles = n_rows // TB
    tpl = seq_l // TB
    nlat = nb * tpl
    return pl.pallas_call(
        functools.partial(_route_kernel, n_experts=n_experts, nlat=nlat),
        grid=(n_tiles,),
        in_specs=[pl.BlockSpec((TB, d), lambda i: (jnp.minimum(i, nlat - 1), 0)),
                  pl.BlockSpec((TB, d), lambda i: (jnp.maximum(i - nlat, 0), 0)),
                  pl.BlockSpec((TB, d), row),
                  pl.BlockSpec((None, N_MOD, d), lambda i: (jnp.minimum(i // tpl, nb), 0, 0)),
                  pl.BlockSpec((1, d), const),
                  pl.BlockSpec((None,) + w_out.shape[1:], lambda i: (layer, 0, 0), pipeline_mode=pl.Buffered(1)),
                  pl.BlockSpec(wr.shape, const)],
        out_specs=[pl.BlockSpec((TB, d), row),
                   pl.BlockSpec((TB, d), row),
                   pl.BlockSpec((TB, LANE), row),
                   pl.BlockSpec((None, SUBLANE, TB), lambda i: (i, 0, 0)),
                   pl.BlockSpec((SUBLANE, LANE), const)],
        out_shape=[jax.ShapeDtypeStruct((n, d), F32),
                   jax.ShapeDtypeStruct((n, d), F32),
                   jax.ShapeDtypeStruct((n, LANE), F32),
                   jax.ShapeDtypeStruct((n_tiles, SUBLANE, TB), I32),
                   jax.ShapeDtypeStruct((SUBLANE, LANE), F32)],
        scratch_shapes=[pltpu.VMEM(w_out.shape[1:], BF16)],
        compiler_params=_params(1),
        name="attn_out_router",
    )(a_lat, a_ctx, h, mod, g.reshape(1, d), w_out, wr)


def _row_copies(src_ref, dst_ref, idx_ref, sem, scatter, wait_here=True):
    def issue(jj, wait):
        for u in range(SUBLANE):
            for k in range(TOP_K):
                row = idx_ref[0, k * TBR + jj * SUBLANE + u]
                if scatter:
                    cp = pltpu.make_async_copy(src_ref.at[jj, pl.ds(u, 1)], dst_ref.at[pl.ds(row, 1)], sem)
                else:
                    cp = pltpu.make_async_copy(src_ref.at[pl.ds(row, 1)], dst_ref.at[k, jj, pl.ds(u, 1)], sem)
                if wait:
                    cp.wait()
                else:
                    cp.start(priority=k % 2)

    def start_body(jj, c):
        issue(jj, False)
        return c

    def wait_body(jj, c):
        issue(jj, True)
        return c

    lax.fori_loop(0, TBR // SUBLANE, start_body, 0)
    if wait_here:
        lax.fori_loop(0, TBR // SUBLANE, wait_body, 0)


def _dispatch_kernel(pad_ref, dest_ref, f_ref, xs_ref, zbuf, fbuf, sem, zsem, fsem, *, n_fills, n_tiles):
    i = pl.program_id(0)
    tile_groups = TBR // SUBLANE

    def stage(tile):
        s = lax.rem(tile, 3)
        return pltpu.make_async_copy(f_ref.at[pl.ds(tile * tile_groups, tile_groups)], fbuf.at[s], fsem.at[s])

    def wait_rows(tile):
        s = lax.rem(tile, 3)
        for _ in range(TOP_K):
            pltpu.make_async_copy(fbuf.at[s], fbuf.at[s], sem.at[s]).wait()

    @pl.when(i == 0)
    def _():
        stage(i).start()
        zbuf[...] = jnp.zeros_like(zbuf)

        def fill(e):
            return pltpu.make_async_copy(zbuf, xs_ref.at[pl.ds(pl.multiple_of(pad_ref[e] * TM, TM), TM)], zsem)

        for e in range(n_fills):
            @pl.when(pad_ref[e] >= 0)
            def _():
                fill(e).start()
        for e in range(n_fills):
            @pl.when(pad_ref[e] >= 0)
            def _():
                fill(e).wait()

    stage(i).wait()

    @pl.when(i + 1 < n_tiles)
    def _():
        stage(i + 1).start()

    slot = lax.rem(i, 3)
    _row_copies(fbuf.at[slot], xs_ref, dest_ref, sem.at[slot], scatter=True, wait_here=False)

    @pl.when(i > 0)
    def _():
        wait_rows(i - 1)

    @pl.when(i == n_tiles - 1)
    def _():
        wait_rows(i)


def _dispatch(f, dest, pad_at, *, n_tiles, n_rows):
    n_fills = pad_at.shape[0]
    n, d = f.shape
    grid_spec = pltpu.PrefetchScalarGridSpec(
        num_scalar_prefetch=1,
        grid=(n_tiles,),
        in_specs=[pl.BlockSpec((None, 1, TOP_K * TBR), lambda i, pad: (i, 0, 0), memory_space=pltpu.SMEM),
                  pl.BlockSpec(memory_space=pl.ANY)],
        out_specs=pl.BlockSpec(memory_space=pl.ANY),
        scratch_shapes=[pltpu.VMEM((TM, d), F32), pltpu.VMEM((3, TBR // SUBLANE, SUBLANE, d), F32),
                        pltpu.SemaphoreType.DMA((3,)), pltpu.SemaphoreType.DMA(()), pltpu.SemaphoreType.DMA((3,))],
    )
    return pl.pallas_call(
        functools.partial(_dispatch_kernel, n_fills=n_fills, n_tiles=n_tiles),
        grid_spec=grid_spec,
        out_shape=jax.ShapeDtypeStruct((n_rows, d), F32),
        compiler_params=_params(1),
        name="moe_dispatch",
    )(pad_at, dest, f.reshape(n // SUBLANE, SUBLANE, d))


def _experts_kernel(te_ref, tv_ref, nu_ref, x_ref, wg_ref, wu_ref, wd_ref, o_ref, xb_ref, wg_buf, wu_buf, wd_buf, wsem,
                    *, layer, n_chunks):
    i = pl.program_id(0)
    n_used = nu_ref[0]
    used = i < n_used
    valid = tv_ref[i]

    def copies(tile, j, slot):
        e = te_ref[tile]
        cols = pl.ds(pl.multiple_of(j * FC, FC), FC)
        return (pltpu.make_async_copy(wg_ref.at[layer, e, :, cols], wg_buf.at[slot], wsem.at[0, slot]),
                pltpu.make_async_copy(wu_ref.at[layer, e, :, cols], wu_buf.at[slot], wsem.at[1, slot]),
                pltpu.make_async_copy(wd_ref.at[layer, e, cols, :], wd_buf.at[slot], wsem.at[2, slot]))

    depth = W_BUFFERS - 1

    def start_ahead(c, ahead):
        t, j = lax.div(c + ahead, n_chunks), lax.rem(c + ahead, n_chunks)

        @pl.when(t < n_used)
        def _():
            for cp in copies(t, j, lax.rem(c + ahead, W_BUFFERS)):
                cp.start()

    @pl.when(i == 0)
    def _():
        for ahead in range(depth):
            start_ahead(0, ahead)

    o_ref[...] = jnp.zeros_like(o_ref)

    def compute(n_rows):
        xb_ref[0:n_rows, :] = x_ref[0:n_rows, :].astype(BF16)

        def chunk_body(j, carry):
            c = i * n_chunks + j
            slot = lax.rem(c, W_BUFFERS)
            for cp in copies(i, j, slot):
                cp.wait()
            start_ahead(c, depth)

            xb = xb_ref[0:n_rows, :]
            act = _silu(_dot(xb, wg_buf[slot].astype(BF16))) * _dot(xb, wu_buf[slot].astype(BF16))
            o_ref[0:n_rows, :] += _dot(act.astype(BF16), wd_buf[slot].astype(BF16))
            return carry

        lax.fori_loop(0, n_chunks, chunk_body, 0)

    for n_rows in range(SUB, TM + 1, SUB):
        @pl.when(used & (valid > n_rows - SUB) & (valid <= n_rows))
        def _():
            compute(n_rows)


def _experts(xs, tile_expert, tile_valid, n_used, layer, we_gate, we_up, we_down):
    p, d = xs.shape
    dfe = we_gate.shape[-1]
    assert dfe % FC == 0 and p % TM == 0 and TM % SUB == 0
    n_chunks = dfe // FC
    n_tiles = p // TM

    grid_spec = pltpu.PrefetchScalarGridSpec(
        num_scalar_prefetch=3,
        grid=(n_tiles,),
        in_specs=[pl.BlockSpec((TM, d), lambda i, te, tv, nu: (jnp.minimum(i, nu[0] - 1), 0)),
                  pl.BlockSpec(memory_space=pl.ANY),
                  pl.BlockSpec(memory_space=pl.ANY),
                  pl.BlockSpec(memory_space=pl.ANY)],
        out_specs=pl.BlockSpec((TM, d), lambda i, te, tv, nu: (i, 0)),
        scratch_shapes=[pltpu.VMEM((TM, d), BF16), pltpu.VMEM((W_BUFFERS, d, FC), F32),
                        pltpu.VMEM((W_BUFFERS, d, FC), F32), pltpu.VMEM((W_BUFFERS, FC, d), F32),
                        pltpu.SemaphoreType.DMA((3, W_BUFFERS))],
    )
    return pl.pallas_call(
        functools.partial(_experts_kernel, layer=layer, n_chunks=n_chunks),
        grid_spec=grid_spec,
        out_shape=jax.ShapeDtypeStruct((p, d), F32),
        compiler_params=_params(1),
        name="expert_swiglu",
    )(tile_expert, tile_valid, n_used, xs, we_gate, we_up, we_down)


def _combine_kernel(dest_ref, ys_ref, info_ref, h_ref, mod_ref, gf_ref, o_ref, ybuf, sem, *, final_norm, n_tiles):
    i = pl.program_id(0)
    slot = lax.rem(i, 2)

    @pl.when(i < n_tiles)
    def _():
        _row_copies(ys_ref, ybuf.at[slot], dest_ref, sem.at[slot], scatter=False, wait_here=False)

    @pl.when(i > 0)
    def _():
        done = ybuf.at[1 - slot]
        pltpu.make_async_copy(done, done, sem.at[1 - slot]).wait()
        info = info_ref[...]
        d = h_ref.shape[1]
        mix = info[:, 4:5] * done[0].reshape(TBR, d) + info[:, 5:6] * done[1].reshape(TBR, d)
        h = h_ref[...] + mod_ref[...][5:6] * mix
        if final_norm:
            h = (h * lax.rsqrt(jnp.mean(h * h, axis=-1, keepdims=True) + EPS)) * gf_ref[...]
        o_ref[...] = h


def _combine(ys, dest, info, h, mod, g_final, *, n_tiles, tpl, nb, final_norm):
    n, d = h.shape
    prev = lambda i: (jnp.maximum(i - 1, 0), 0)
    return pl.pallas_call(
        functools.partial(_combine_kernel, final_norm=final_norm, n_tiles=n_tiles),
        grid=(n_tiles + 1,),
        in_specs=[pl.BlockSpec((None, 1, TOP_K * TBR), lambda i: (jnp.minimum(i, n_tiles - 1), 0, 0),
                               memory_space=pltpu.SMEM),
                  pl.BlockSpec(memory_space=pl.ANY),
                  pl.BlockSpec((TBR, LANE), prev),
                  pl.BlockSpec((TBR, d), prev),
                  pl.BlockSpec((None, N_MOD, d), lambda i: (jnp.minimum(jnp.maximum(i - 1, 0) // tpl, nb), 0, 0)),
                  pl.BlockSpec((1, d), lambda i: (0, 0))],
        out_specs=pl.BlockSpec((TBR, d), prev),
        out_shape=jax.ShapeDtypeStruct((n_tiles * TBR, d), F32),
        scratch_shapes=[pltpu.VMEM((2, TOP_K, TBR // SUBLANE, SUBLANE, d), F32), pltpu.SemaphoreType.DMA((2,))],
        compiler_params=_params(1),
        name="moe_combine",
    )(dest, ys, info, h, mod, g_final.reshape(1, d))


def _moe(f, info, rt, counts, h, mod, g_final, layer, we_gate, we_up, we_down, *, seq_l, nb, final_norm):
    n = f.shape[0]
    assert n % TBR == 0 and seq_l % TBR == 0 and TBR % TB == 0
    n_tiles, tpl = n // TBR, seq_l // TBR
    n_experts = we_gate.shape[1]
    n_xtiles = -(-TOP_K * n // TM) + n_experts
    n_rows = n_xtiles * TM

    cnt = counts[0, :n_experts].astype(I32)
    tiles_e = (cnt + TM - 1) // TM
    tile_end = jnp.cumsum(tiles_e)
    starts = (tile_end - tiles_e) * TM
    n_used = tile_end[-1:]
    tile_ids = jnp.arange(n_xtiles, dtype=I32)
    tile_expert = jnp.minimum(jnp.sum(tile_ids[:, None] >= tile_end[None, :], axis=1), n_experts - 1).astype(I32)
    tile_valid = jnp.clip(cnt[tile_expert] - (tile_ids * TM - starts[tile_expert]), 0, TM).astype(I32)
    choice = rt[:, 0:TOP_K, :]
    rank = rt[:, TOP_K:2 * TOP_K, :]
    start_of = sum(jnp.where(choice == e, starts[e], 0) for e in range(n_experts))
    dest = (start_of + rank).reshape(n_tiles, TBR // TB, TOP_K, TB).transpose(0, 2, 1, 3)
    dest = dest.reshape(n_tiles, 1, TOP_K * TBR)
    tail = n_used + jnp.arange(n_experts, dtype=I32)
    pad_at = jnp.concatenate([jnp.where(tiles_e > 0, tile_end - 1, -1),
                              jnp.where(tail < n_xtiles, tail, -1)]).astype(I32)

    xs = _dispatch(f, dest, pad_at, n_tiles=n_tiles, n_rows=n_rows)
    ys = _experts(xs, tile_expert, tile_valid, n_used, layer, we_gate, we_up, we_down)
    return _combine(ys, dest, info, h, mod, g_final, n_tiles=n_tiles, tpl=tpl, nb=nb, final_norm=final_norm)


def kernel(x, c, ctx, c_ctx, w_mod, b_mod, g_mix, g_ffn, g_final, w_in_ab, conv_w, pool_w, pool_scale,
           w_out_ab, w_ff_gate, w_ff_up, w_ff_down, w_qkv, rpb, w_out_na, w_router, we_gate, we_up, we_down):
    nb, seq_l, d = x.shape
    seq_c = ctx.shape[1]
    depth = w_mod.shape[0]
    heads = rpb.shape[1]
    na_rows = (rpb.shape[2] + 1) // 2
    na_cols = (rpb.shape[3] + 1) // 2
    assert seq_l % TB == 0 and seq_c % TB == 0 and nb + 1 <= SUBLANE and depth % 2 == 0
    tpl = seq_l // TB
    nlat = nb * tpl
    nall = nlat + nb * seq_c // TB

    cond = jnp.zeros((SUBLANE, d), F32).at[:nb].set(c).at[nb].set(c_ctx)
    mods = _adaln(cond, w_mod, b_mod).reshape(depth, SUBLANE, N_MOD, d)

    bf = lambda a: a.astype(BF16)
    h = out = None
    for i in range(depth):
        j = i // 2
        last = i == depth - 1
        mod = mods[i]
        if i % 2 == 0:
            srcs = (x.reshape(nb * seq_l, d), ctx.reshape(nb * seq_c, d), 0) if i == 0 else (h, h, nlat)
            h = _mixer(*srcs, mod, g_mix[i], j, w_in_ab, conv_w, pool_w, pool_scale[j], w_out_ab,
                       n_tiles=nall, nlat=nlat, nb=nb, seq_l=seq_l, seq_c=seq_c)
            h = _ffn(h, mod, g_ffn[i], bf(w_ff_gate[j]), bf(w_ff_up[j]), bf(w_ff_down[j]),
                     n_rows=nall * TB, seq_l=seq_l, nb=nb)
        else:
            n_tiles = nlat if last else nall
            q, k, v = _qkv(h, mod, g_mix[i], j, w_qkv, n_rows=nall * TB, seq_l=seq_l, nb=nb,
                           qscale=(d // heads) ** -0.5)
            o_lat, o_ctx = _natten(q, k, v, _bias_table(rpb[j], na_cols), nb=nb, seq_l=seq_l, seq_c=seq_c,
                                   heads=heads, na_rows=na_rows)
            h, f, info, rt, counts = _route(o_lat, o_ctx, h, mod, g_ffn[i], j, w_out_na, w_router[j],
                                            n_rows=n_tiles * TB, seq_l=seq_l, nb=nb)
            h = _moe(f, info, rt, counts, h, mod, g_final, j, we_gate, we_up, we_down,
                     seq_l=seq_l, nb=nb, final_norm=last)
            if last:
                out = h
    return out.reshape(nb, seq_l, d)
```

```python
import functools

import numpy as np
import jax
import jax.numpy as jnp
from jax import lax
from jax.experimental import pallas as pl
from jax.experimental.pallas import tpu as pltpu

F32 = jnp.float32
BF16 = jnp.bfloat16
I32 = jnp.int32

GRID_W = 64
POOL_WINDOWS = (2, 4, 8, 16)
N_MOD = 6
TOP_K = 2
EPS = 1e-6
NEG = -1e30

LANE = 128
SUBLANE = 8
TB = 256
TBD = 512
TBR = 512
HALO = SUBLANE
TM = 1024
SUB = 256
FC = 512
W_BUFFERS = 4
NA_GROUP = 4
ADALN_COLS = 1536
FFN_COLS = 768
VMEM_LIMIT = 56 * 1024 * 1024


def _params(n_axes):
    return pltpu.CompilerParams(dimension_semantics=("arbitrary",) * n_axes,
                                vmem_limit_bytes=VMEM_LIMIT)


def _rms_mod(x, g, shift, scale):
    y = x * lax.rsqrt(jnp.mean(x * x, axis=-1, keepdims=True) + EPS)
    return (y * g) * (1.0 + scale) + shift


def _silu(x):
    return x * jax.nn.sigmoid(x)


def _dot(a, b):
    return jnp.dot(a, b, preferred_element_type=F32)


def _dot_nt(a, b):
    return lax.dot_general(a, b, (((1,), (1,)), ((), ())), preferred_element_type=F32)


def _adaln_kernel(cond_ref, w_ref, b_ref, o_ref):
    s = _silu(cond_ref[...]).astype(BF16)
    o_ref[...] = _dot(s, w_ref[...].astype(BF16)) + b_ref[...]


def _adaln(cond, w_mod, b_mod):
    depth, d, nd = w_mod.shape
    tn = ADALN_COLS
    assert nd % tn == 0
    return pl.pallas_call(
        _adaln_kernel,
        grid=(depth, nd // tn),
        in_specs=[pl.BlockSpec((SUBLANE, d), lambda l, n: (0, 0)),
                  pl.BlockSpec((None, d, tn), lambda l, n: (l, 0, n)),
                  pl.BlockSpec((None, 1, tn), lambda l, n: (l, 0, n))],
        out_specs=pl.BlockSpec((None, SUBLANE, tn), lambda l, n: (l, 0, n)),
        out_shape=jax.ShapeDtypeStruct((depth, SUBLANE, nd), F32),
        compiler_params=_params(2),
        name="adaln",
    )(cond, w_mod, b_mod.reshape(depth, 1, nd))


def _mixer_kernel(ap_ref, ac_ref, an_ref, bp_ref, bc_ref, bn_ref, mod_ref, g_ref, win_ref, cw_ref, pw_ref, ps_ref,
                  wout_ref, o_ref, xs_ref, z_ref, p_ref, winb_ref, woutb_ref,
                  *, nlat, tpl, tpc, seq_l, seq_c, d_conv, pool_group):
    i = pl.program_id(0)
    is_ctx = i >= nlat
    pos = jnp.where(is_ctx, lax.rem(i - nlat, tpc), lax.rem(i, tpl)) * TB
    seq_len = jnp.where(is_ctx, seq_c, seq_l)
    ext = TB + 2 * HALO

    @pl.when(i == 0)
    def _():
        winb_ref[...] = win_ref[...].astype(BF16)
        woutb_ref[...] = wout_ref[...].astype(BF16)

    hc = jnp.where(is_ctx, bc_ref[...], ac_ref[...])
    xs_ref[0:HALO, :] = jnp.where(is_ctx, bp_ref[...], ap_ref[...])
    xs_ref[HALO:HALO + TB, :] = hc
    xs_ref[HALO + TB:ext, :] = jnp.where(is_ctx, bn_ref[...], an_ref[...])
    mod = mod_ref[...]
    a = _rms_mod(xs_ref[...], g_ref[...], mod[0:1], mod[1:2]).astype(BF16)
    u = _dot(a, winb_ref[...])
    srow = lax.broadcasted_iota(I32, (ext, 1), 0) + (pos - HALO)
    u = jnp.where((srow >= 0) & (srow < seq_len), u, 0.0)

    z_ref[...] = u[:, d_conv:2 * d_conv] * u[:, 2 * d_conv:3 * d_conv]
    p_ref[...] = u[:, 3 * d_conv:]
    cw = cw_ref[...]
    conv = (z_ref[HALO - 1:HALO - 1 + TB, :] * cw[0:1] + z_ref[HALO:HALO + TB, :] * cw[1:2]
            + z_ref[HALO + 1:HALO + 1 + TB, :] * cw[2:3])
    pieces = [u[HALO:HALO + TB, 0:d_conv] * conv]

    spos = srow[HALO:HALO + TB]
    ps = ps_ref[...]
    for g, win in enumerate(POOL_WINDOWS):
        lo, hi = win // 2, win - 1 - win // 2
        cols = slice(g * pool_group, (g + 1) * pool_group)
        acc = p_ref[HALO - lo:HALO - lo + TB, cols]
        for dlt in range(-lo + 1, hi + 1):
            acc = acc + p_ref[HALO + dlt:HALO + dlt + TB, cols]
        cnt = jnp.minimum(spos + hi, seq_len - 1) - jnp.maximum(spos - lo, 0) + 1
        diff = acc / cnt.astype(F32) - p_ref[HALO:HALO + TB, cols]
        pieces.append(_dot(diff.astype(BF16), pw_ref[g].astype(BF16)) * ps[:, cols])
    cat = jnp.concatenate(pieces, axis=-1).astype(BF16)
    o_ref[...] = hc + mod[2:3] * _dot(cat, woutb_ref[...])


def _mixer(src_a, src_b, b_off, mod, g, layer, w_in, conv_w, pool_w, pool_scale, w_out,
           *, n_tiles, nlat, nb, seq_l, seq_c):
    d = src_a.shape[1]
    d_conv = conv_w.shape[-1]
    d_pool = pool_scale.shape[-1]
    pool_group = pool_w.shape[-1]
    assert pool_group % LANE == 0 and d_conv % LANE == 0 and len(POOL_WINDOWS) == pool_w.shape[1]
    tpl, tpc = seq_l // TB, seq_c // TB
    ext = TB + 2 * HALO
    hb = TB // HALO
    a_last = src_a.shape[0] // HALO - 1
    b_last = src_b.shape[0] // HALO - 1
    kern = functools.partial(_mixer_kernel, nlat=nlat, tpl=tpl, tpc=tpc, seq_l=seq_l, seq_c=seq_c,
                             d_conv=d_conv, pool_group=pool_group)
    a_tile = lambda i: jnp.minimum(i, nlat - 1)
    b_tile = lambda i: jnp.maximum(i - nlat, 0) + b_off
    once = pl.Buffered(1)
    return pl.pallas_call(
        kern,
        grid=(n_tiles,),
        in_specs=[pl.BlockSpec((HALO, d), lambda i: (jnp.maximum(a_tile(i) * hb - 1, 0), 0)),
                  pl.BlockSpec((TB, d), lambda i: (a_tile(i), 0)),
                  pl.BlockSpec((HALO, d), lambda i: (jnp.minimum((a_tile(i) + 1) * hb, a_last), 0)),
                  pl.BlockSpec((HALO, d), lambda i: (jnp.maximum(b_tile(i) * hb - 1, 0), 0)),
                  pl.BlockSpec((TB, d), lambda i: (b_tile(i), 0)),
                  pl.BlockSpec((HALO, d), lambda i: (jnp.minimum((b_tile(i) + 1) * hb, b_last), 0)),
                  pl.BlockSpec((None, N_MOD, d), lambda i: (jnp.minimum(i // tpl, nb), 0, 0)),
                  pl.BlockSpec((1, d), lambda i: (0, 0)),
                  pl.BlockSpec((None,) + w_in.shape[1:], lambda i: (layer, 0, 0), pipeline_mode=once),
                  pl.BlockSpec((None,) + conv_w.shape[1:], lambda i: (layer, 0, 0)),
                  pl.BlockSpec((None,) + pool_w.shape[1:], lambda i: (layer, 0, 0, 0)),
                  pl.BlockSpec((1, d_pool), lambda i: (0, 0)),
                  pl.BlockSpec((None,) + w_out.shape[1:], lambda i: (layer, 0, 0), pipeline_mode=once)],
        out_specs=pl.BlockSpec((TB, d), lambda i: (i, 0)),
        out_shape=jax.ShapeDtypeStruct((n_tiles * TB, d), F32),
        scratch_shapes=[pltpu.VMEM((ext, d), F32), pltpu.VMEM((ext, d_conv), F32), pltpu.VMEM((ext, d_pool), F32),
                        pltpu.VMEM(w_in.shape[1:], BF16), pltpu.VMEM(w_out.shape[1:], BF16)],
        compiler_params=_params(1),
        name="conv_pool_mixer",
    )(src_a, src_a, src_a, src_b, src_b, src_b, mod, g.reshape(1, d), w_in, conv_w, pool_w,
      pool_scale.reshape(1, d_pool), w_out)


def _ffn_kernel(h_ref, mod_ref, g_ref, wg_ref, wu_ref, wd_ref, o_ref, act_ref, *, chunks):
    mod = mod_ref[...]
    h = h_ref[...]
    f = _rms_mod(h, g_ref[...], mod[3:4], mod[4:5]).astype(BF16)
    for c0, c1 in chunks:
        gate = _dot(f, wg_ref[:, c0:c1])
        up = _dot(f, wu_ref[:, c0:c1])
        act_ref[:, c0:c1] = (_silu(gate) * up).astype(BF16)
    o_ref[...] = h + mod[5:6] * _dot(act_ref[...], wd_ref[...])


def _ffn(h, mod, g, wg, wu, wd, *, n_rows, seq_l, nb):
    t, d = h.shape
    dff = wg.shape[-1]
    step = FFN_COLS
    chunks = tuple((c, min(c + step, dff)) for c in range(0, dff, step))
    const = lambda i: (0, 0)
    once = pl.Buffered(1)
    assert n_rows % TBD == 0 and seq_l % TBD == 0
    tpl = seq_l // TBD
    return pl.pallas_call(
        functools.partial(_ffn_kernel, chunks=chunks),
        grid=(n_rows // TBD,),
        in_specs=[pl.BlockSpec((TBD, d), lambda i: (i, 0)),
                  pl.BlockSpec((None, N_MOD, d), lambda i: (jnp.minimum(i // tpl, nb), 0, 0)),
                  pl.BlockSpec((1, d), const),
                  pl.BlockSpec(wg.shape, const, pipeline_mode=once),
                  pl.BlockSpec(wu.shape, const, pipeline_mode=once),
                  pl.BlockSpec(wd.shape, const, pipeline_mode=once)],
        out_specs=pl.BlockSpec((TBD, d), lambda i: (i, 0)),
        out_shape=jax.ShapeDtypeStruct((n_rows, d), F32),
        scratch_shapes=[pltpu.VMEM((TBD, dff), BF16)],
        compiler_params=_params(1),
        name="dense_swiglu",
    )(h, mod, g.reshape(1, d), wg, wu, wd)


def _qkv_kernel(h_ref, mod_ref, g_ref, w_ref, q_ref, k_ref, v_ref, wb_ref, *, d, qscale):
    @pl.when(pl.program_id(0) == 0)
    def _():
        wb_ref[...] = w_ref[...].astype(BF16)

    mod = mod_ref[...]
    a = _rms_mod(h_ref[...], g_ref[...], mod[0:1], mod[1:2]).astype(BF16)
    q_ref[...] = (_dot(a, wb_ref[:, 0:d]) * qscale).astype(BF16)
    k_ref[...] = _dot(a, wb_ref[:, d:2 * d]).astype(BF16)
    v_ref[...] = _dot(a, wb_ref[:, 2 * d:3 * d]).astype(BF16)


def _qkv(h, mod, g, layer, w, *, n_rows, seq_l, nb, qscale):
    t, d = h.shape
    const = lambda i: (0, 0)
    out = jax.ShapeDtypeStruct((n_rows, d), BF16)
    assert n_rows % TBD == 0 and seq_l % TBD == 0
    tpl = seq_l // TBD
    return pl.pallas_call(
        functools.partial(_qkv_kernel, d=d, qscale=qscale),
        grid=(n_rows // TBD,),
        in_specs=[pl.BlockSpec((TBD, d), lambda i: (i, 0)),
                  pl.BlockSpec((None, N_MOD, d), lambda i: (jnp.minimum(i // tpl, nb), 0, 0)),
                  pl.BlockSpec((1, d), const),
                  pl.BlockSpec((None,) + w.shape[1:], lambda i: (layer, 0, 0), pipeline_mode=pl.Buffered(1))],
        out_specs=[pl.BlockSpec((TBD, d), lambda i: (i, 0))] * 3,
        out_shape=[out, out, out],
        scratch_shapes=[pltpu.VMEM(w.shape[1:], BF16)],
        compiler_params=_params(1),
        name="qkv_proj",
    )(h, mod, g.reshape(1, d), w)


def _softmax_pv(s_parts, v_parts):
    m = s_parts[0].max(axis=-1, keepdims=True)
    for s in s_parts[1:]:
        m = jnp.maximum(m, s.max(axis=-1, keepdims=True))
    den = 0.0
    out = 0.0
    for s, v in zip(s_parts, v_parts):
        p = jnp.exp(s - m)
        den = den + p.sum(axis=-1, keepdims=True)
        out = out + _dot(p.astype(BF16), v)
    return out / den


def _natten_kernel(q_ref, k_ref, v_ref, kc_ref, vc_ref, qc_ref, bias_ref, o_ref, oc_ref,
                   s_scr, p_scr, den_scr, *, rows, na_rows, head_dim):
    w = GRID_W
    band = na_rows * w
    lane = lax.broadcasted_iota(I32, (1, LANE), 1)
    head0 = lane < head_dim
    zero = jnp.zeros((), BF16)
    kc = kc_ref[...]
    vc = vc_ref[...]

    def stack(q):
        return jnp.concatenate([jnp.where(head0, q, zero), jnp.where(head0, zero, q)], axis=0)

    def unstack(o, n):
        return jnp.where(head0, o[0:n], o[n:2 * n])

    def offsets(r):
        start = jnp.clip(r - na_rows // 2, 0, rows - na_rows)
        return pl.multiple_of(r * w, w), pl.multiple_of(start * w, w), start - r + (na_rows - 1)

    def scores(r, slot):
        q0, k0, d0 = offsets(r)
        qq = stack(q_ref[pl.ds(q0, w), :])
        bias = jnp.concatenate([bias_ref[d0 + 2 * p] for p in range(band // LANE)], axis=-1)
        s_scr[slot, :, 0:band] = _dot_nt(qq, k_ref[pl.ds(k0, band), :]) + bias
        s_scr[slot, :, band:] = _dot_nt(qq, kc)

    def softmax(slot):
        s = s_scr[slot]
        p = jnp.exp(s - s.max(axis=-1, keepdims=True))
        den_scr[slot] = p.sum(axis=-1, keepdims=True)
        p_scr[slot] = p.astype(BF16)

    def values(r, slot):
        q0, k0, _ = offsets(r)
        o = _dot(p_scr[slot, :, 0:band], v_ref[pl.ds(k0, band), :]) + _dot(p_scr[slot, :, band:], vc)
        o_ref[pl.ds(q0, w), :] = unstack(o / den_scr[slot], w).astype(o_ref.dtype)

    n_groups = rows // NA_GROUP
    assert rows % NA_GROUP == 0 and n_groups % 2 == 0 and n_groups >= 4

    def step(u, parity, do_scores, do_softmax, do_values):
        for g in range(NA_GROUP):
            if do_softmax:
                softmax((1 - parity) * NA_GROUP + g)
        for g in range(NA_GROUP):
            if do_scores:
                scores(u * NA_GROUP + g, parity * NA_GROUP + g)
            if do_values:
                values((u - 2) * NA_GROUP + g, parity * NA_GROUP + g)

    step(0, 0, True, False, False)
    step(1, 1, True, True, False)

    def group_pair_body(u2, carry):
        step(2 * u2, 0, True, True, True)
        step(2 * u2 + 1, 1, True, True, True)
        return carry

    lax.fori_loop(1, n_groups // 2, group_pair_body, 0)
    step(n_groups, 0, False, True, True)
    step(n_groups + 1, 1, False, False, True)

    qc = qc_ref[...]
    nc = qc.shape[0]
    oc = _softmax_pv([_dot_nt(stack(qc), kc)], [vc])
    oc_ref[...] = unstack(oc, nc).astype(oc_ref.dtype)


def _bias_table(rpb, na_cols):
    heads, nr, nc = rpb.shape
    w = GRID_W
    jcol = np.arange(w)
    cstart = np.clip(jcol - na_cols // 2, 0, w - na_cols)
    kcol = np.arange(w)
    inside = (kcol[None, :] >= cstart[:, None]) & (kcol[None, :] < cstart[:, None] + na_cols)
    dc = kcol[None, :] - jcol[:, None] + (na_cols - 1)
    pick = jnp.asarray((dc[None] == np.arange(nc)[:, None, None]) & inside[None], F32)
    t2 = jnp.einsum("hdm,mqk->hdqk", rpb, pick, precision=lax.Precision.HIGHEST)
    t2 = jnp.where(inside[None, None], t2, NEG)
    t3 = jnp.concatenate([t2[:, :-1], t2[:, 1:]], axis=-1)
    t3 = t3.reshape(heads // 2, 2, nr - 1, w, 2 * w).transpose(0, 2, 1, 3, 4)
    return t3.reshape(heads // 2, nr - 1, 2 * w, 2 * w).astype(F32)


def _natten(q, k, v, bias, *, nb, seq_l, seq_c, heads, na_rows):
    t, d = q.shape
    head_dim = d // heads
    assert 2 * head_dim == LANE and 2 * GRID_W == LANE and seq_l % GRID_W == 0
    rows = seq_l // GRID_W
    assert (nb * seq_l) % seq_c == 0
    cblk = nb * seq_l // seq_c
    lat = lambda b, hp: (b, hp)
    ctx = lambda b, hp: (cblk + b, hp)
    return pl.pallas_call(
        functools.partial(_natten_kernel, rows=rows, na_rows=na_rows, head_dim=head_dim),
        grid=(nb, heads // 2),
        in_specs=[pl.BlockSpec((seq_l, LANE), lat),
                  pl.BlockSpec((seq_l, LANE), lat),
                  pl.BlockSpec((seq_l, LANE), lat),
                  pl.BlockSpec((seq_c, LANE), ctx),
                  pl.BlockSpec((seq_c, LANE), ctx),
                  pl.BlockSpec((seq_c, LANE), ctx),
                  pl.BlockSpec((None,) + bias.shape[1:], lambda b, hp: (hp, 0, 0, 0))],
        out_specs=[pl.BlockSpec((seq_l, LANE), lat),
                   pl.BlockSpec((seq_c, LANE), lambda b, hp: (b, hp))],
        out_shape=[jax.ShapeDtypeStruct((nb * seq_l, d), BF16),
                   jax.ShapeDtypeStruct((nb * seq_c, d), BF16)],
        scratch_shapes=[pltpu.VMEM((2 * NA_GROUP, 2 * GRID_W, na_rows * GRID_W + seq_c), F32),
                        pltpu.VMEM((2 * NA_GROUP, 2 * GRID_W, na_rows * GRID_W + seq_c), BF16),
                        pltpu.VMEM((2 * NA_GROUP, 2 * GRID_W, 1), F32)],
        compiler_params=_params(2),
        name="neighbourhood_attention",
    )(q, k, v, k, v, q, bias)


def _split_bf16(x):
    hi = x.astype(BF16)
    return hi, (x - hi.astype(F32)).astype(BF16)


def _route_kernel(al_ref, ac_ref, h_ref, mod_ref, g_ref, wo_ref, wr_ref, h_out_ref, f_ref, info_ref, rt_ref,
                  cnt_ref, wob_ref, *, n_experts, nlat):
    i = pl.program_id(0)

    @pl.when(i == 0)
    def _():
        cnt_ref[...] = jnp.zeros_like(cnt_ref)
        wob_ref[...] = wo_ref[...].astype(BF16)

    mod = mod_ref[...]
    attn = jnp.where(i < nlat, al_ref[...], ac_ref[...])
    h = h_ref[...] + mod[2:3] * _dot(attn, wob_ref[...])
    h_out_ref[...] = h
    f = _rms_mod(h, g_ref[...], mod[3:4], mod[4:5])
    f_ref[...] = f

    f_hi, f_lo = _split_bf16(f)
    z_hi = _dot(f_hi, wr_ref[...])
    z_lo = _dot(f_lo, wr_ref[...])
    logits = z_hi[:, 0:LANE] + (z_hi[:, LANE:] + z_lo[:, 0:LANE])
    lane_i = lax.broadcasted_iota(I32, logits.shape, 1)
    lane = lane_i.astype(F32)
    logits = jnp.where(lane_i < n_experts, logits, -jnp.inf)
    v1 = logits.max(axis=-1, keepdims=True)
    i1 = jnp.where(logits == v1, lane, float(LANE)).min(axis=-1, keepdims=True)
    rest = jnp.where(lane == i1, -jnp.inf, logits)
    v2 = rest.max(axis=-1, keepdims=True)
    i2 = jnp.where(rest == v2, lane, float(LANE)).min(axis=-1, keepdims=True)
    e2 = jnp.exp(v2 - v1)
    g1 = 1.0 / (1.0 + e2)
    g2 = e2 * g1

    sel1 = lane == i1
    sel2 = lane == i2
    onehot = jnp.where(sel1 | sel2, 1.0, 0.0)
    tr = lax.broadcasted_iota(I32, (TB, TB), 0)
    tc = lax.broadcasted_iota(I32, (TB, TB), 1)
    before = _dot(jnp.where(tc < tr, 1.0, 0.0).astype(BF16), onehot.astype(BF16)) + cnt_ref[0:1, :]
    r1 = jnp.where(sel1, before, 0.0).sum(axis=-1, keepdims=True)
    r2 = jnp.where(sel2, before, 0.0).sum(axis=-1, keepdims=True)
    cnt_ref[...] = cnt_ref[...] + onehot.sum(axis=0, keepdims=True)

    info = jnp.where(lane_i == 0, i1, 0.0)
    info = jnp.where(lane_i == 1, i2, info)
    info = jnp.where(lane_i == 2, r1, info)
    info = jnp.where(lane_i == 3, r2, info)
    info = jnp.where(lane_i == 4, g1, info)
    info = jnp.where(lane_i == 5, g2, info)
    info_ref[...] = info
    rt_ref[...] = jnp.transpose(info)[0:SUBLANE, :].astype(I32)


def _route(a_lat, a_ctx, h, mod, g, layer, w_out, w_router, *, n_rows, seq_l, nb):
    t, d = h.shape
    n_experts = w_router.shape[-1]
    wr = jnp.concatenate(_split_bf16(jnp.zeros((d, LANE), F32).at[:, :n_experts].set(w_router)), axis=1)
    const = lambda i: (0, 0)
    row = lambda i: (i, 0)
    n = n_rows
    assert n_rows % TB == 0 and seq_l % TB == 0 and a_ctx.shape[0] % TB == 0
    n_tiles = n_rows // TB
    tpl = seq_l // TB
    nlat = nb * tpl
    return pl.pallas_call(
        functools.partial(_route_kernel, n_experts=n_experts, nlat=nlat),
        grid=(n_tiles,),
        in_specs=[pl.BlockSpec((TB, d), lambda i: (jnp.minimum(i, nlat - 1), 0)),
                  pl.BlockSpec((TB, d), lambda i: (jnp.maximum(i - nlat, 0), 0)),
                  pl.BlockSpec((TB, d), row),
                  pl.BlockSpec((None, N_MOD, d), lambda i: (jnp.minimum(i // tpl, nb), 0, 0)),
                  pl.BlockSpec((1, d), const),
                  pl.BlockSpec((None,) + w_out.shape[1:], lambda i: (layer, 0, 0), pipeline_mode=pl.Buffered(1)),
                  pl.BlockSpec(wr.shape, const)],
        out_specs=[pl.BlockSpec((TB, d), row),
                   pl.BlockSpec((TB, d), row),
                   pl.BlockSpec((TB, LANE), row),
                   pl.BlockSpec((None, SUBLANE, TB), lambda i: (i, 0, 0)),
                   pl.BlockSpec((SUBLANE, LANE), const)],
        out_shape=[jax.ShapeDtypeStruct((n, d), F32),
                   jax.ShapeDtypeStruct((n, d), F32),
                   jax.ShapeDtypeStruct((n, LANE), F32),
                   jax.ShapeDtypeStruct((n_tiles, SUBLANE, TB), I32),
                   jax.ShapeDtypeStruct((SUBLANE, LANE), F32)],
        scratch_shapes=[pltpu.VMEM(w_out.shape[1:], BF16)],
        compiler_params=_params(1),
        name="attn_out_router",
    )(a_lat, a_ctx, h, mod, g.reshape(1, d), w_out, wr)


def _row_copies(src_ref, dst_ref, idx_ref, sem, scatter, wait_here=True):
    def issue(jj, wait):
        for u in range(SUBLANE):
            for k in range(TOP_K):
                row = idx_ref[0, k * TBR + jj * SUBLANE + u]
                if scatter:
                    cp = pltpu.make_async_copy(src_ref.at[jj, pl.ds(u, 1)], dst_ref.at[pl.ds(row, 1)], sem)
                else:
                    cp = pltpu.make_async_copy(src_ref.at[pl.ds(row, 1)], dst_ref.at[k, jj, pl.ds(u, 1)], sem)
                if wait:
                    cp.wait()
                else:
                    cp.start(priority=k % 2)

    def start_body(jj, c):
        issue(jj, False)
        return c

    def wait_body(jj, c):
        issue(jj, True)
        return c

    lax.fori_loop(0, TBR // SUBLANE, start_body, 0)
    if wait_here:
        lax.fori_loop(0, TBR // SUBLANE, wait_body, 0)


def _dispatch_kernel(pad_ref, dest_ref, f_ref, xs_ref, zbuf, fbuf, sem, zsem, fsem, *, n_fills, n_tiles):
    i = pl.program_id(0)
    tile_groups = TBR // SUBLANE

    def stage(tile):
        s = lax.rem(tile, 3)
        return pltpu.make_async_copy(f_ref.at[pl.ds(tile * tile_groups, tile_groups)], fbuf.at[s], fsem.at[s])

    def wait_rows(tile):
        s = lax.rem(tile, 3)
        for _ in range(TOP_K):
            pltpu.make_async_copy(fbuf.at[s], fbuf.at[s], sem.at[s]).wait()

    @pl.when(i == 0)
    def _():
        stage(i).start()
        zbuf[...] = jnp.zeros_like(zbuf)

        def fill(e):
            return pltpu.make_async_copy(zbuf, xs_ref.at[pl.ds(pl.multiple_of(pad_ref[e] * TM, TM), TM)], zsem)

        for e in range(n_fills):
            @pl.when(pad_ref[e] >= 0)
            def _():
                fill(e).start()
        for e in range(n_fills):
            @pl.when(pad_ref[e] >= 0)
            def _():
                fill(e).wait()

    stage(i).wait()

    @pl.when(i + 1 < n_tiles)
    def _():
        stage(i + 1).start()

    slot = lax.rem(i, 3)
    _row_copies(fbuf.at[slot], xs_ref, dest_ref, sem.at[slot], scatter=True, wait_here=False)

    @pl.when(i > 0)
    def _():
        wait_rows(i - 1)

    @pl.when(i == n_tiles - 1)
    def _():
        wait_rows(i)


def _dispatch(f, dest, pad_at, *, n_tiles, n_rows):
    n_fills = pad_at.shape[0]
    n, d = f.shape
    grid_spec = pltpu.PrefetchScalarGridSpec(
        num_scalar_prefetch=1,
        grid=(n_tiles,),
        in_specs=[pl.BlockSpec((None, 1, TOP_K * TBR), lambda i, pad: (i, 0, 0), memory_space=pltpu.SMEM),
                  pl.BlockSpec(memory_space=pl.ANY)],
        out_specs=pl.BlockSpec(memory_space=pl.ANY),
        scratch_shapes=[pltpu.VMEM((TM, d), F32), pltpu.VMEM((3, TBR // SUBLANE, SUBLANE, d), F32),
                        pltpu.SemaphoreType.DMA((3,)), pltpu.SemaphoreType.DMA(()), pltpu.SemaphoreType.DMA((3,))],
    )
    return pl.pallas_call(
        functools.partial(_dispatch_kernel, n_fills=n_fills, n_tiles=n_tiles),
        grid_spec=grid_spec,
        out_shape=jax.ShapeDtypeStruct((n_rows, d), F32),
        compiler_params=_params(1),
        name="moe_dispatch",
    )(pad_at, dest, f.reshape(n // SUBLANE, SUBLANE, d))


def _experts_kernel(te_ref, tv_ref, nu_ref, x_ref, wg_ref, wu_ref, wd_ref, o_ref, xb_ref, wg_buf, wu_buf, wd_buf, wsem,
                    *, layer, n_chunks):
    i = pl.program_id(0)
    n_used = nu_ref[0]
    used = i < n_used
    valid = tv_ref[i]

    def copies(tile, j, slot):
        e = te_ref[tile]
        cols = pl.ds(pl.multiple_of(j * FC, FC), FC)
        return (pltpu.make_async_copy(wg_ref.at[layer, e, :, cols], wg_buf.at[slot], wsem.at[0, slot]),
                pltpu.make_async_copy(wu_ref.at[layer, e, :, cols], wu_buf.at[slot], wsem.at[1, slot]),
                pltpu.make_async_copy(wd_ref.at[layer, e, cols, :], wd_buf.at[slot], wsem.at[2, slot]))

    depth = W_BUFFERS - 1

    def start_ahead(c, ahead):
        t, j = lax.div(c + ahead, n_chunks), lax.rem(c + ahead, n_chunks)

        @pl.when(t < n_used)
        def _():
            for cp in copies(t, j, lax.rem(c + ahead, W_BUFFERS)):
                cp.start()

    @pl.when(i == 0)
    def _():
        for ahead in range(depth):
            start_ahead(0, ahead)

    @pl.when(jnp.logical_not(used))
    def _():
        o_ref[...] = jnp.zeros_like(o_ref)

    def compute(n_rows):
        xb_ref[0:n_rows, :] = x_ref[0:n_rows, :].astype(BF16)
        if n_rows < TM:
            o_ref[n_rows:TM, :] = jnp.zeros((TM - n_rows, o_ref.shape[1]), o_ref.dtype)

        def chunk(j, first):
            c = i * n_chunks + j
            slot = lax.rem(c, W_BUFFERS)
            for cp in copies(i, j, slot):
                cp.wait()
            start_ahead(c, depth)

            xb = xb_ref[0:n_rows, :]
            act = _silu(_dot(xb, wg_buf[slot].astype(BF16))) * _dot(xb, wu_buf[slot].astype(BF16))
            part = _dot(act.astype(BF16), wd_buf[slot].astype(BF16))
            if first:
                o_ref[0:n_rows, :] = part
            else:
                o_ref[0:n_rows, :] += part

        chunk(0, True)

        def chunk_body(j, carry):
            chunk(j, False)
            return carry

        lax.fori_loop(1, n_chunks, chunk_body, 0)

    for n_rows in range(SUB, TM + 1, SUB):
        @pl.when(used & (valid > n_rows - SUB) & (valid <= n_rows))
        def _():
            compute(n_rows)


def _experts(xs, tile_expert, tile_valid, n_used, layer, we_gate, we_up, we_down):
    p, d = xs.shape
    dfe = we_gate.shape[-1]
    assert dfe % FC == 0 and p % TM == 0 and TM % SUB == 0
    n_chunks = dfe // FC
    n_tiles = p // TM

    grid_spec = pltpu.PrefetchScalarGridSpec(
        num_scalar_prefetch=3,
        grid=(n_tiles,),
        in_specs=[pl.BlockSpec((TM, d), lambda i, te, tv, nu: (jnp.minimum(i, nu[0] - 1), 0)),
                  pl.BlockSpec(memory_space=pl.ANY),
                  pl.BlockSpec(memory_space=pl.ANY),
                  pl.BlockSpec(memory_space=pl.ANY)],
        out_specs=pl.BlockSpec((TM, d), lambda i, te, tv, nu: (i, 0)),
        scratch_shapes=[pltpu.VMEM((TM, d), BF16), pltpu.VMEM((W_BUFFERS, d, FC), F32),
                        pltpu.VMEM((W_BUFFERS, d, FC), F32), pltpu.VMEM((W_BUFFERS, FC, d), F32),
                        pltpu.SemaphoreType.DMA((3, W_BUFFERS))],
    )
    return pl.pallas_call(
        functools.partial(_experts_kernel, layer=layer, n_chunks=n_chunks),
        grid_spec=grid_spec,
        out_shape=jax.ShapeDtypeStruct((p, d), F32),
        compiler_params=_params(1),
        name="expert_swiglu",
    )(tile_expert, tile_valid, n_used, xs, we_gate, we_up, we_down)


def _combine_kernel(dest_ref, ys_ref, info_ref, h_ref, mod_ref, gf_ref, o_ref, ybuf, sem, *, final_norm, n_tiles):
    i = pl.program_id(0)
    slot = lax.rem(i, 2)

    @pl.when(i < n_tiles)
    def _():
        _row_copies(ys_ref, ybuf.at[slot], dest_ref, sem.at[slot], scatter=False, wait_here=False)

    @pl.when(i > 0)
    def _():
        done = ybuf.at[1 - slot]
        pltpu.make_async_copy(done, done, sem.at[1 - slot]).wait()
        info = info_ref[...]
        d = h_ref.shape[1]
        mix = info[:, 4:5] * done[0].reshape(TBR, d) + info[:, 5:6] * done[1].reshape(TBR, d)
        h = h_ref[...] + mod_ref[...][5:6] * mix
        if final_norm:
            h = (h * lax.rsqrt(jnp.mean(h * h, axis=-1, keepdims=True) + EPS)) * gf_ref[...]
        o_ref[...] = h


def _combine(ys, dest, info, h, mod, g_final, *, n_tiles, tpl, nb, final_norm):
    n, d = h.shape
    prev = lambda i: (jnp.maximum(i - 1, 0), 0)
    return pl.pallas_call(
        functools.partial(_combine_kernel, final_norm=final_norm, n_tiles=n_tiles),
        grid=(n_tiles + 1,),
        in_specs=[pl.BlockSpec((None, 1, TOP_K * TBR), lambda i: (jnp.minimum(i, n_tiles - 1), 0, 0),
                               memory_space=pltpu.SMEM),
                  pl.BlockSpec(memory_space=pl.ANY),
                  pl.BlockSpec((TBR, LANE), prev),
                  pl.BlockSpec((TBR, d), prev),
                  pl.BlockSpec((None, N_MOD, d), lambda i: (jnp.minimum(jnp.maximum(i - 1, 0) // tpl, nb), 0, 0)),
                  pl.BlockSpec((1, d), lambda i: (0, 0))],
        out_specs=pl.BlockSpec((TBR, d), prev),
        out_shape=jax.ShapeDtypeStruct((n_tiles * TBR, d), F32),
        scratch_shapes=[pltpu.VMEM((2, TOP_K, TBR // SUBLANE, SUBLANE, d), F32), pltpu.SemaphoreType.DMA((2,))],
        compiler_params=_params(1),
        name="moe_combine",
    )(dest, ys, info, h, mod, g_final.reshape(1, d))


def _moe(f, info, rt, counts, h, mod, g_final, layer, we_gate, we_up, we_down, *, seq_l, nb, final_norm):
    n = f.shape[0]
    assert n % TBR == 0 and seq_l % TBR == 0 and TBR % TB == 0
    n_tiles, tpl = n // TBR, seq_l // TBR
    n_experts = we_gate.shape[1]
    n_xtiles = -(-TOP_K * n // TM) + n_experts
    n_rows = n_xtiles * TM

    cnt = counts[0, :n_experts].astype(I32)
    tiles_e = (cnt + TM - 1) // TM
    tile_end = jnp.cumsum(tiles_e)
    starts = (tile_end - tiles_e) * TM
    n_used = tile_end[-1:]
    tile_ids = jnp.arange(n_xtiles, dtype=I32)
    tile_expert = jnp.minimum(jnp.sum(tile_ids[:, None] >= tile_end[None, :], axis=1), n_experts - 1).astype(I32)
    tile_valid = jnp.clip(cnt[tile_expert] - (tile_ids * TM - starts[tile_expert]), 0, TM).astype(I32)
    choice = rt[:, 0:TOP_K, :]
    rank = rt[:, TOP_K:2 * TOP_K, :]
    start_of = sum(jnp.where(choice == e, starts[e], 0) for e in range(n_experts))
    dest = (start_of + rank).reshape(n_tiles, TBR // TB, TOP_K, TB).transpose(0, 2, 1, 3)
    dest = dest.reshape(n_tiles, 1, TOP_K * TBR)
    tail = n_used + jnp.arange(n_experts, dtype=I32)
    pad_at = jnp.concatenate([jnp.where(tiles_e > 0, tile_end - 1, -1),
                              jnp.where(tail < n_xtiles, tail, -1)]).astype(I32)

    xs = _dispatch(f, dest, pad_at, n_tiles=n_tiles, n_rows=n_rows)
    ys = _experts(xs, tile_expert, tile_valid, n_used, layer, we_gate, we_up, we_down)
    return _combine(ys, dest, info, h, mod, g_final, n_tiles=n_tiles, tpl=tpl, nb=nb, final_norm=final_norm)


def kernel(x, c, ctx, c_ctx, w_mod, b_mod, g_mix, g_ffn, g_final, w_in_ab, conv_w, pool_w, pool_scale,
           w_out_ab, w_ff_gate, w_ff_up, w_ff_down, w_qkv, rpb, w_out_na, w_router, we_gate, we_up, we_down):
    nb, seq_l, d = x.shape
    seq_c = ctx.shape[1]
    depth = w_mod.shape[0]
    heads = rpb.shape[1]
    na_rows = (rpb.shape[2] + 1) // 2
    na_cols = (rpb.shape[3] + 1) // 2
    assert seq_l % TB == 0 and seq_c % TB == 0 and nb + 1 <= SUBLANE and depth % 2 == 0
    tpl = seq_l // TB
    nlat = nb * tpl
    nall = nlat + nb * seq_c // TB

    cond = jnp.zeros((SUBLANE, d), F32).at[:nb].set(c).at[nb].set(c_ctx)
    mods = _adaln(cond, w_mod, b_mod).reshape(depth, SUBLANE, N_MOD, d)

    bf = lambda a: a.astype(BF16)
    h = out = None
    for i in range(depth):
        j = i // 2
        last = i == depth - 1
        mod = mods[i]
        if i % 2 == 0:
            srcs = (x.reshape(nb * seq_l, d), ctx.reshape(nb * seq_c, d), 0) if i == 0 else (h, h, nlat)
            h = _mixer(*srcs, mod, g_mix[i], j, w_in_ab, conv_w, pool_w, pool_scale[j], w_out_ab,
                       n_tiles=nall, nlat=nlat, nb=nb, seq_l=seq_l, seq_c=seq_c)
            h = _ffn(h, mod, g_ffn[i], bf(w_ff_gate[j]), bf(w_ff_up[j]), bf(w_ff_down[j]),
                     n_rows=nall * TB, seq_l=seq_l, nb=nb)
        else:
            n_tiles = nlat if last else nall
            q, k, v = _qkv(h, mod, g_mix[i], j, w_qkv, n_rows=nall * TB, seq_l=seq_l, nb=nb,
                           qscale=(d // heads) ** -0.5)
            o_lat, o_ctx = _natten(q, k, v, _bias_table(rpb[j], na_cols), nb=nb, seq_l=seq_l, seq_c=seq_c,
                                   heads=heads, na_rows=na_rows)
            h, f, info, rt, counts = _route(o_lat, o_ctx, h, mod, g_ffn[i], j, w_out_na, w_router[j],
                                            n_rows=n_tiles * TB, seq_l=seq_l, nb=nb)
            h = _moe(f, info, rt, counts, h, mod, g_final, j, we_gate, we_up, we_down,
                     seq_l=seq_l, nb=nb, final_norm=last)
            if last:
                out = h
    return out.reshape(nb, seq_l, d)
```

```python
import functools

import numpy as np
import jax
import jax.numpy as jnp
from jax import lax
from jax.experimental import pallas as pl
from jax.experimental.pallas import tpu as pltpu

F32 = jnp.float32
BF16 = jnp.bfloat16
I32 = jnp.int32

GRID_W = 64
POOL_WINDOWS = (2, 4, 8, 16)
N_MOD = 6
TOP_K = 2
EPS = 1e-6
NEG = -1e30

LANE = 128
SUBLANE = 8
TB = 256
TBD = 512
TBR = 512
HALO = SUBLANE
TM = 1024
SUB = 256
FC = 512
W_BUFFERS = 4
NA_GROUP = 4
ADALN_COLS = 1536
FFN_COLS = 768
VMEM_LIMIT = 56 * 1024 * 1024


def _params(n_axes):
    return pltpu.CompilerParams(dimension_semantics=("arbitrary",) * n_axes,
                                vmem_limit_bytes=VMEM_LIMIT)


def _rms_mod(x, g, shift, scale):
    y = x * lax.rsqrt(jnp.mean(x * x, axis=-1, keepdims=True) + EPS)
    return (y * g) * (1.0 + scale) + shift


def _silu(x):
    return x * jax.nn.sigmoid(x)


def _dot(a, b):
    return jnp.dot(a, b, preferred_element_type=F32)


def _dot_nt(a, b):
    return lax.dot_general(a, b, (((1,), (1,)), ((), ())), preferred_element_type=F32)


def _adaln_kernel(cond_ref, w_ref, b_ref, o_ref):
    s = _silu(cond_ref[...]).astype(BF16)
    o_ref[...] = _dot(s, w_ref[...].astype(BF16)) + b_ref[...]


def _adaln(cond, w_mod, b_mod):
    depth, d, nd = w_mod.shape
    tn = ADALN_COLS
    assert nd % tn == 0
    return pl.pallas_call(
        _adaln_kernel,
        grid=(depth, nd // tn),
        in_specs=[pl.BlockSpec((SUBLANE, d), lambda l, n: (0, 0)),
                  pl.BlockSpec((None, d, tn), lambda l, n: (l, 0, n)),
                  pl.BlockSpec((None, 1, tn), lambda l, n: (l, 0, n))],
        out_specs=pl.BlockSpec((None, SUBLANE, tn), lambda l, n: (l, 0, n)),
        out_shape=jax.ShapeDtypeStruct((depth, SUBLANE, nd), F32),
        compiler_params=_params(2),
        name="adaln",
    )(cond, w_mod, b_mod.reshape(depth, 1, nd))


def _mixer_kernel(ap_ref, ac_ref, an_ref, bp_ref, bc_ref, bn_ref, mod_ref, g_ref, win_ref, cw_ref, pw_ref, ps_ref,
                  wout_ref, o_ref, xs_ref, z_ref, p_ref, winb_ref, woutb_ref,
                  *, nlat, tpl, tpc, seq_l, seq_c, d_conv, pool_group):
    i = pl.program_id(0)
    is_ctx = i >= nlat
    pos = jnp.where(is_ctx, lax.rem(i - nlat, tpc), lax.rem(i, tpl)) * TB
    seq_len = jnp.where(is_ctx, seq_c, seq_l)
    ext = TB + 2 * HALO

    @pl.when(i == 0)
    def _():
        winb_ref[...] = win_ref[...].astype(BF16)
        woutb_ref[...] = wout_ref[...].astype(BF16)

    hc = jnp.where(is_ctx, bc_ref[...], ac_ref[...])
    xs_ref[0:HALO, :] = jnp.where(is_ctx, bp_ref[...], ap_ref[...])
    xs_ref[HALO:HALO + TB, :] = hc
    xs_ref[HALO + TB:ext, :] = jnp.where(is_ctx, bn_ref[...], an_ref[...])
    mod = mod_ref[...]
    a = _rms_mod(xs_ref[...], g_ref[...], mod[0:1], mod[1:2]).astype(BF16)
    u = _dot(a, winb_ref[...])
    srow = lax.broadcasted_iota(I32, (ext, 1), 0) + (pos - HALO)
    u = jnp.where((srow >= 0) & (srow < seq_len), u, 0.0)

    z_ref[...] = u[:, d_conv:2 * d_conv] * u[:, 2 * d_conv:3 * d_conv]
    p_ref[...] = u[:, 3 * d_conv:]
    cw = cw_ref[...]
    conv = (z_ref[HALO - 1:HALO - 1 + TB, :] * cw[0:1] + z_ref[HALO:HALO + TB, :] * cw[1:2]
            + z_ref[HALO + 1:HALO + 1 + TB, :] * cw[2:3])
    pieces = [u[HALO:HALO + TB, 0:d_conv] * conv]

    spos = srow[HALO:HALO + TB]
    ps = ps_ref[...]
    for g, win in enumerate(POOL_WINDOWS):
        lo, hi = win // 2, win - 1 - win // 2
        cols = slice(g * pool_group, (g + 1) * pool_group)
        acc = p_ref[HALO - lo:HALO - lo + TB, cols]
        for dlt in range(-lo + 1, hi + 1):
            acc = acc + p_ref[HALO + dlt:HALO + dlt + TB, cols]
        cnt = jnp.minimum(spos + hi, seq_len - 1) - jnp.maximum(spos - lo, 0) + 1
        diff = acc / cnt.astype(F32) - p_ref[HALO:HALO + TB, cols]
        pieces.append(_dot(diff.astype(BF16), pw_ref[g].astype(BF16)) * ps[:, cols])
    cat = jnp.concatenate(pieces, axis=-1).astype(BF16)
    o_ref[...] = hc + mod[2:3] * _dot(cat, woutb_ref[...])


def _mixer(src_a, src_b, b_off, mod, g, layer, w_in, conv_w, pool_w, pool_scale, w_out,
           *, n_tiles, nlat, nb, seq_l, seq_c):
    d = src_a.shape[1]
    d_conv = conv_w.shape[-1]
    d_pool = pool_scale.shape[-1]
    pool_group = pool_w.shape[-1]
    assert pool_group % LANE == 0 and d_conv % LANE == 0 and len(POOL_WINDOWS) == pool_w.shape[1]
    tpl, tpc = seq_l // TB, seq_c // TB
    ext = TB + 2 * HALO
    hb = TB // HALO
    a_last = src_a.shape[0] // HALO - 1
    b_last = src_b.shape[0] // HALO - 1
    kern = functools.partial(_mixer_kernel, nlat=nlat, tpl=tpl, tpc=tpc, seq_l=seq_l, seq_c=seq_c,
                             d_conv=d_conv, pool_group=pool_group)
    a_tile = lambda i: jnp.minimum(i, nlat - 1)
    b_tile = lambda i: jnp.maximum(i - nlat, 0) + b_off
    once = pl.Buffered(1)
    return pl.pallas_call(
        kern,
        grid=(n_tiles,),
        in_specs=[pl.BlockSpec((HALO, d), lambda i: (jnp.maximum(a_tile(i) * hb - 1, 0), 0)),
                  pl.BlockSpec((TB, d), lambda i: (a_tile(i), 0)),
                  pl.BlockSpec((HALO, d), lambda i: (jnp.minimum((a_tile(i) + 1) * hb, a_last), 0)),
                  pl.BlockSpec((HALO, d), lambda i: (jnp.maximum(b_tile(i) * hb - 1, 0), 0)),
                  pl.BlockSpec((TB, d), lambda i: (b_tile(i), 0)),
                  pl.BlockSpec((HALO, d), lambda i: (jnp.minimum((b_tile(i) + 1) * hb, b_last), 0)),
                  pl.BlockSpec((None, N_MOD, d), lambda i: (jnp.minimum(i // tpl, nb), 0, 0)),
                  pl.BlockSpec((1, d), lambda i: (0, 0)),
                  pl.BlockSpec((None,) + w_in.shape[1:], lambda i: (layer, 0, 0), pipeline_mode=once),
                  pl.BlockSpec((None,) + conv_w.shape[1:], lambda i: (layer, 0, 0)),
                  pl.BlockSpec((None,) + pool_w.shape[1:], lambda i: (layer, 0, 0, 0)),
                  pl.BlockSpec((1, d_pool), lambda i: (0, 0)),
                  pl.BlockSpec((None,) + w_out.shape[1:], lambda i: (layer, 0, 0), pipeline_mode=once)],
        out_specs=pl.BlockSpec((TB, d), lambda i: (i, 0)),
        out_shape=jax.ShapeDtypeStruct((n_tiles * TB, d), F32),
        scratch_shapes=[pltpu.VMEM((ext, d), F32), pltpu.VMEM((ext, d_conv), F32), pltpu.VMEM((ext, d_pool), F32),
                        pltpu.VMEM(w_in.shape[1:], BF16), pltpu.VMEM(w_out.shape[1:], BF16)],
        compiler_params=_params(1),
        name="conv_pool_mixer",
    )(src_a, src_a, src_a, src_b, src_b, src_b, mod, g.reshape(1, d), w_in, conv_w, pool_w,
      pool_scale.reshape(1, d_pool), w_out)


def _ffn_kernel(h_ref, mod_ref, g_ref, wg_ref, wu_ref, wd_ref, o_ref, act_ref, *, chunks):
    mod = mod_ref[...]
    h = h_ref[...]
    f = _rms_mod(h, g_ref[...], mod[3:4], mod[4:5]).astype(BF16)
    for c0, c1 in chunks:
        gate = _dot(f, wg_ref[:, c0:c1])
        up = _dot(f, wu_ref[:, c0:c1])
        act_ref[:, c0:c1] = (_silu(gate) * up).astype(BF16)
    o_ref[...] = h + mod[5:6] * _dot(act_ref[...], wd_ref[...])


def _ffn(h, mod, g, wg, wu, wd, *, n_rows, seq_l, nb):
    t, d = h.shape
    dff = wg.shape[-1]
    step = FFN_COLS
    chunks = tuple((c, min(c + step, dff)) for c in range(0, dff, step))
    const = lambda i: (0, 0)
    once = pl.Buffered(1)
    assert n_rows % TBD == 0 and seq_l % TBD == 0
    tpl = seq_l // TBD
    return pl.pallas_call(
        functools.partial(_ffn_kernel, chunks=chunks),
        grid=(n_rows // TBD,),
        in_specs=[pl.BlockSpec((TBD, d), lambda i: (i, 0)),
                  pl.BlockSpec((None, N_MOD, d), lambda i: (jnp.minimum(i // tpl, nb), 0, 0)),
                  pl.BlockSpec((1, d), const),
                  pl.BlockSpec(wg.shape, const, pipeline_mode=once),
                  pl.BlockSpec(wu.shape, const, pipeline_mode=once),
                  pl.BlockSpec(wd.shape, const, pipeline_mode=once)],
        out_specs=pl.BlockSpec((TBD, d), lambda i: (i, 0)),
        out_shape=jax.ShapeDtypeStruct((n_rows, d), F32),
        scratch_shapes=[pltpu.VMEM((TBD, dff), BF16)],
        compiler_params=_params(1),
        name="dense_swiglu",
    )(h, mod, g.reshape(1, d), wg, wu, wd)


def _qkv_kernel(h_ref, mod_ref, g_ref, w_ref, q_ref, k_ref, v_ref, wb_ref, *, d, qscale):
    @pl.when(pl.program_id(0) == 0)
    def _():
        wb_ref[...] = w_ref[...].astype(BF16)

    mod = mod_ref[...]
    a = _rms_mod(h_ref[...], g_ref[...], mod[0:1], mod[1:2]).astype(BF16)
    q_ref[...] = (_dot(a, wb_ref[:, 0:d]) * qscale).astype(BF16)
    k_ref[...] = _dot(a, wb_ref[:, d:2 * d]).astype(BF16)
    v_ref[...] = _dot(a, wb_ref[:, 2 * d:3 * d]).astype(BF16)


def _qkv(h, mod, g, layer, w, *, n_rows, seq_l, nb, qscale):
    t, d = h.shape
    const = lambda i: (0, 0)
    out = jax.ShapeDtypeStruct((n_rows, d), BF16)
    assert n_rows % TBD == 0 and seq_l % TBD == 0
    tpl = seq_l // TBD
    return pl.pallas_call(
        functools.partial(_qkv_kernel, d=d, qscale=qscale),
        grid=(n_rows // TBD,),
        in_specs=[pl.BlockSpec((TBD, d), lambda i: (i, 0)),
                  pl.BlockSpec((None, N_MOD, d), lambda i: (jnp.minimum(i // tpl, nb), 0, 0)),
                  pl.BlockSpec((1, d), const),
                  pl.BlockSpec((None,) + w.shape[1:], lambda i: (layer, 0, 0), pipeline_mode=pl.Buffered(1))],
        out_specs=[pl.BlockSpec((TBD, d), lambda i: (i, 0))] * 3,
        out_shape=[out, out, out],
        scratch_shapes=[pltpu.VMEM(w.shape[1:], BF16)],
        compiler_params=_params(1),
        name="qkv_proj",
    )(h, mod, g.reshape(1, d), w)


def _softmax_pv(s_parts, v_parts):
    m = s_parts[0].max(axis=-1, keepdims=True)
    for s in s_parts[1:]:
        m = jnp.maximum(m, s.max(axis=-1, keepdims=True))
    den = 0.0
    out = 0.0
    for s, v in zip(s_parts, v_parts):
        p = jnp.exp(s - m)
        den = den + p.sum(axis=-1, keepdims=True)
        out = out + _dot(p.astype(BF16), v)
    return out / den


def _natten_kernel(q_ref, k_ref, v_ref, kc_ref, vc_ref, qc_ref, bias_ref, o_ref, oc_ref,
                   s_scr, p_scr, den_scr, *, rows, na_rows, head_dim):
    w = GRID_W
    band = na_rows * w
    lane = lax.broadcasted_iota(I32, (1, LANE), 1)
    head0 = lane < head_dim
    zero = jnp.zeros((), BF16)
    kc = kc_ref[...]
    vc = vc_ref[...]

    def stack(q):
        return jnp.concatenate([jnp.where(head0, q, zero), jnp.where(head0, zero, q)], axis=0)

    def unstack(o, n):
        return jnp.where(head0, o[0:n], o[n:2 * n])

    def offsets(r):
        start = jnp.clip(r - na_rows // 2, 0, rows - na_rows)
        return pl.multiple_of(r * w, w), pl.multiple_of(start * w, w), start - r + (na_rows - 1)

    def scores(r, slot):
        q0, k0, d0 = offsets(r)
        qq = stack(q_ref[pl.ds(q0, w), :])
        bias = jnp.concatenate([bias_ref[d0 + 2 * p] for p in range(band // LANE)], axis=-1)
        s_scr[slot, :, 0:band] = _dot_nt(qq, k_ref[pl.ds(k0, band), :]) + bias
        s_scr[slot, :, band:] = _dot_nt(qq, kc)

    def softmax(slot):
        s = s_scr[slot]
        p = jnp.exp(s - s.max(axis=-1, keepdims=True))
        den_scr[slot] = p.sum(axis=-1, keepdims=True)
        p_scr[slot] = p.astype(BF16)

    def values(r, slot):
        q0, k0, _ = offsets(r)
        o = _dot(p_scr[slot, :, 0:band], v_ref[pl.ds(k0, band), :]) + _dot(p_scr[slot, :, band:], vc)
        o_ref[pl.ds(q0, w), :] = unstack(o / den_scr[slot], w).astype(o_ref.dtype)

    n_groups = rows // NA_GROUP
    assert rows % NA_GROUP == 0 and n_groups % 2 == 0 and n_groups >= 4

    def step(u, parity, do_scores, do_softmax, do_values):
        for g in range(NA_GROUP):
            if do_softmax:
                softmax((1 - parity) * NA_GROUP + g)
        for g in range(NA_GROUP):
            if do_scores:
                scores(u * NA_GROUP + g, parity * NA_GROUP + g)
            if do_values:
                values((u - 2) * NA_GROUP + g, parity * NA_GROUP + g)

    step(0, 0, True, False, False)
    step(1, 1, True, True, False)

    def group_pair_body(u2, carry):
        step(2 * u2, 0, True, True, True)
        step(2 * u2 + 1, 1, True, True, True)
        return carry

    lax.fori_loop(1, n_groups // 2, group_pair_body, 0)
    step(n_groups, 0, False, True, True)
    step(n_groups + 1, 1, False, False, True)

    qc = qc_ref[...]
    nc = qc.shape[0]
    oc = _softmax_pv([_dot_nt(stack(qc), kc)], [vc])
    oc_ref[...] = unstack(oc, nc).astype(oc_ref.dtype)


def _bias_table(rpb, na_cols):
    heads, nr, nc = rpb.shape
    w = GRID_W
    jcol = np.arange(w)
    cstart = np.clip(jcol - na_cols // 2, 0, w - na_cols)
    kcol = np.arange(w)
    inside = (kcol[None, :] >= cstart[:, None]) & (kcol[None, :] < cstart[:, None] + na_cols)
    dc = kcol[None, :] - jcol[:, None] + (na_cols - 1)
    pick = jnp.asarray((dc[None] == np.arange(nc)[:, None, None]) & inside[None], F32)
    t2 = jnp.einsum("hdm,mqk->hdqk", rpb, pick, precision=lax.Precision.HIGHEST)
    t2 = jnp.where(inside[None, None], t2, NEG)
    t3 = jnp.concatenate([t2[:, :-1], t2[:, 1:]], axis=-1)
    t3 = t3.reshape(heads // 2, 2, nr - 1, w, 2 * w).transpose(0, 2, 1, 3, 4)
    return t3.reshape(heads // 2, nr - 1, 2 * w, 2 * w).astype(F32)


def _natten(q, k, v, bias, *, nb, seq_l, seq_c, heads, na_rows):
    t, d = q.shape
    head_dim = d // heads
    assert 2 * head_dim == LANE and 2 * GRID_W == LANE and seq_l % GRID_W == 0
    rows = seq_l // GRID_W
    assert (nb * seq_l) % seq_c == 0
    cblk = nb * seq_l // seq_c
    lat = lambda b, hp: (b, hp)
    ctx = lambda b, hp: (cblk + b, hp)
    return pl.pallas_call(
        functools.partial(_natten_kernel, rows=rows, na_rows=na_rows, head_dim=head_dim),
        grid=(nb, heads // 2),
        in_specs=[pl.BlockSpec((seq_l, LANE), lat),
                  pl.BlockSpec((seq_l, LANE), lat),
                  pl.BlockSpec((seq_l, LANE), lat),
                  pl.BlockSpec((seq_c, LANE), ctx),
                  pl.BlockSpec((seq_c, LANE), ctx),
                  pl.BlockSpec((seq_c, LANE), ctx),
                  pl.BlockSpec((None,) + bias.shape[1:], lambda b, hp: (hp, 0, 0, 0))],
        out_specs=[pl.BlockSpec((seq_l, LANE), lat),
                   pl.BlockSpec((seq_c, LANE), lambda b, hp: (b, hp))],
        out_shape=[jax.ShapeDtypeStruct((nb * seq_l, d), BF16),
                   jax.ShapeDtypeStruct((nb * seq_c, d), BF16)],
        scratch_shapes=[pltpu.VMEM((2 * NA_GROUP, 2 * GRID_W, na_rows * GRID_W + seq_c), F32),
                        pltpu.VMEM((2 * NA_GROUP, 2 * GRID_W, na_rows * GRID_W + seq_c), BF16),
                        pltpu.VMEM((2 * NA_GROUP, 2 * GRID_W, 1), F32)],
        compiler_params=_params(2),
        name="neighbourhood_attention",
    )(q, k, v, k, v, q, bias)


def _split_bf16(x):
    hi = x.astype(BF16)
    return hi, (x - hi.astype(F32)).astype(BF16)


def _route_kernel(al_ref, ac_ref, h_ref, mod_ref, g_ref, wo_ref, wr_ref, h_out_ref, f_ref, info_ref, rt_ref,
                  cnt_ref, wob_ref, *, n_experts, nlat):
    i = pl.program_id(0)

    @pl.when(i == 0)
    def _():
        cnt_ref[...] = jnp.zeros_like(cnt_ref)
        wob_ref[...] = wo_ref[...].astype(BF16)

    mod = mod_ref[...]
    attn = jnp.where(i < nlat, al_ref[...], ac_ref[...])
    h = h_ref[...] + mod[2:3] * _dot(attn, wob_ref[...])
    h_out_ref[...] = h
    f = _rms_mod(h, g_ref[...], mod[3:4], mod[4:5])
    f_ref[...] = f

    f_hi, f_lo = _split_bf16(f)
    z_hi = _dot(f_hi, wr_ref[...])
    z_lo = _dot(f_lo, wr_ref[...])
    logits = z_hi[:, 0:LANE] + (z_hi[:, LANE:] + z_lo[:, 0:LANE])
    lane_i = lax.broadcasted_iota(I32, logits.shape, 1)
    lane = lane_i.astype(F32)
    logits = jnp.where(lane_i < n_experts, logits, -jnp.inf)
    v1 = logits.max(axis=-1, keepdims=True)
    i1 = jnp.where(logits == v1, lane, float(LANE)).min(axis=-1, keepdims=True)
    rest = jnp.where(lane == i1, -jnp.inf, logits)
    v2 = rest.max(axis=-1, keepdims=True)
    i2 = jnp.where(rest == v2, lane, float(LANE)).min(axis=-1, keepdims=True)
    e2 = jnp.exp(v2 - v1)
    g1 = 1.0 / (1.0 + e2)
    g2 = e2 * g1

    sel1 = lane == i1
    sel2 = lane == i2
    onehot = jnp.where(sel1 | sel2, 1.0, 0.0)
    tr = lax.broadcasted_iota(I32, (TB, TB), 0)
    tc = lax.broadcasted_iota(I32, (TB, TB), 1)
    before = _dot(jnp.where(tc < tr, 1.0, 0.0).astype(BF16), onehot.astype(BF16)) + cnt_ref[0:1, :]
    r1 = jnp.where(sel1, before, 0.0).sum(axis=-1, keepdims=True)
    r2 = jnp.where(sel2, before, 0.0).sum(axis=-1, keepdims=True)
    cnt_ref[...] = cnt_ref[...] + onehot.sum(axis=0, keepdims=True)

    info = jnp.where(lane_i == 0, i1, 0.0)
    info = jnp.where(lane_i == 1, i2, info)
    info = jnp.where(lane_i == 2, r1, info)
    info = jnp.where(lane_i == 3, r2, info)
    info = jnp.where(lane_i == 4, g1, info)
    info = jnp.where(lane_i == 5, g2, info)
    info_ref[...] = info
    rt_ref[...] = jnp.transpose(info)[0:SUBLANE, :].astype(I32)


def _route(a_lat, a_ctx, h, mod, g, layer, w_out, w_router, *, n_rows, seq_l, nb):
    t, d = h.shape
    n_experts = w_router.shape[-1]
    wr = jnp.concatenate(_split_bf16(jnp.zeros((d, LANE), F32).at[:, :n_experts].set(w_router)), axis=1)
    const = lambda i: (0, 0)
    row = lambda i: (i, 0)
    n = n_rows
    assert n_rows % TB == 0 and seq_l % TB == 0 and a_ctx.shape[0] % TB == 0
    n_tiles = n_rows // TB
    tpl = seq_l // TB
    nlat = nb * tpl
    return pl.pallas_call(
        functools.partial(_route_kernel, n_experts=n_experts, nlat=nlat),
        grid=(n_tiles,),
        in_specs=[pl.BlockSpec((TB, d), lambda i: (jnp.minimum(i, nlat - 1), 0)),
                  pl.BlockSpec((TB, d), lambda i: (jnp.maximum(i - nlat, 0), 0)),
                  pl.BlockSpec((TB, d), row),
                  pl.BlockSpec((None, N_MOD, d), lambda i: (jnp.minimum(i // tpl, nb), 0, 0)),
                  pl.BlockSpec((1, d), const),
                  pl.BlockSpec((None,) + w_out.shape[1:], lambda i: (layer, 0, 0), pipeline_mode=pl.Buffered(1)),
                  pl.BlockSpec(wr.shape, const)],
        out_specs=[pl.BlockSpec((TB, d), row),
                   pl.BlockSpec((TB, d), row),
                   pl.BlockSpec((TB, LANE), row),
                   pl.BlockSpec((None, SUBLANE, TB), lambda i: (i, 0, 0)),
                   pl.BlockSpec((SUBLANE, LANE), const)],
        out_shape=[jax.ShapeDtypeStruct((n, d), F32),
                   jax.ShapeDtypeStruct((n, d), F32),
                   jax.ShapeDtypeStruct((n, LANE), F32),
                   jax.ShapeDtypeStruct((n_tiles, SUBLANE, TB), I32),
                   jax.ShapeDtypeStruct((SUBLANE, LANE), F32)],
        scratch_shapes=[pltpu.VMEM(w_out.shape[1:], BF16)],
        compiler_params=_params(1),
        name="attn_out_router",
    )(a_lat, a_ctx, h, mod, g.reshape(1, d), w_out, wr)


def _row_copies(src_ref, dst_ref, idx_ref, sem, scatter, wait_here=True):
    def issue(jj, wait):
        for u in range(SUBLANE):
            for k in range(TOP_K):
                row = idx_ref[0, k * TBR + jj * SUBLANE + u]
                if scatter:
                    cp = pltpu.make_async_copy(src_ref.at[jj, pl.ds(u, 1)], dst_ref.at[pl.ds(row, 1)], sem)
                else:
                    cp = pltpu.make_async_copy(src_ref.at[pl.ds(row, 1)], dst_ref.at[k, jj, pl.ds(u, 1)], sem)
                if wait:
                    cp.wait()
                else:
                    cp.start(priority=k % 2)

    def start_body(jj, c):
        issue(jj, False)
        return c

    def wait_body(jj, c):
        issue(jj, True)
        return c

    lax.fori_loop(0, TBR // SUBLANE, start_body, 0)
    if wait_here:
        lax.fori_loop(0, TBR // SUBLANE, wait_body, 0)


def _dispatch_kernel(pad_ref, dest_ref, f_ref, xs_ref, zbuf, fbuf, sem, zsem, fsem, *, n_fills, n_tiles):
    i = pl.program_id(0)
    tile_groups = TBR // SUBLANE

    def stage(tile):
        s = lax.rem(tile, 3)
        return pltpu.make_async_copy(f_ref.at[pl.ds(tile * tile_groups, tile_groups)], fbuf.at[s], fsem.at[s])

    def wait_rows(tile):
        s = lax.rem(tile, 3)
        for _ in range(TOP_K):
            pltpu.make_async_copy(fbuf.at[s], fbuf.at[s], sem.at[s]).wait()

    @pl.when(i == 0)
    def _():
        stage(i).start()
        zbuf[...] = jnp.zeros_like(zbuf)

        def fill(e):
            return pltpu.make_async_copy(zbuf, xs_ref.at[pl.ds(pl.multiple_of(pad_ref[e] * TM, TM), TM)], zsem)

        for e in range(n_fills):
            @pl.when(pad_ref[e] >= 0)
            def _():
                fill(e).start()
        for e in range(n_fills):
            @pl.when(pad_ref[e] >= 0)
            def _():
                fill(e).wait()

    stage(i).wait()

    @pl.when(i + 1 < n_tiles)
    def _():
        stage(i + 1).start()

    slot = lax.rem(i, 3)
    _row_copies(fbuf.at[slot], xs_ref, dest_ref, sem.at[slot], scatter=True, wait_here=False)

    @pl.when(i > 0)
    def _():
        wait_rows(i - 1)

    @pl.when(i == n_tiles - 1)
    def _():
        wait_rows(i)


def _dispatch(f, dest, pad_at, *, n_tiles, n_rows):
    n_fills = pad_at.shape[0]
    n, d = f.shape
    grid_spec = pltpu.PrefetchScalarGridSpec(
        num_scalar_prefetch=1,
        grid=(n_tiles,),
        in_specs=[pl.BlockSpec((None, 1, TOP_K * TBR), lambda i, pad: (i, 0, 0), memory_space=pltpu.SMEM),
                  pl.BlockSpec(memory_space=pl.ANY)],
        out_specs=pl.BlockSpec(memory_space=pl.ANY),
        scratch_shapes=[pltpu.VMEM((TM, d), F32), pltpu.VMEM((3, TBR // SUBLANE, SUBLANE, d), F32),
                        pltpu.SemaphoreType.DMA((3,)), pltpu.SemaphoreType.DMA(()), pltpu.SemaphoreType.DMA((3,))],
    )
    return pl.pallas_call(
        functools.partial(_dispatch_kernel, n_fills=n_fills, n_tiles=n_tiles),
        grid_spec=grid_spec,
        out_shape=jax.ShapeDtypeStruct((n_rows, d), F32),
        compiler_params=_params(1),
        name="moe_dispatch",
    )(pad_at, dest, f.reshape(n // SUBLANE, SUBLANE, d))


def _experts_kernel(te_ref, tv_ref, nu_ref, x_ref, wg_ref, wu_ref, wd_ref, o_ref, xb_ref, wg_buf, wu_buf, wd_buf, wsem,
                    *, layer, n_chunks):
    i = pl.program_id(0)
    n_used = nu_ref[0]
    used = i < n_used
    valid = tv_ref[i]

    def copies(tile, j, slot):
        e = te_ref[tile]
        cols = pl.ds(pl.multiple_of(j * FC, FC), FC)
        return (pltpu.make_async_copy(wg_ref.at[layer, e, :, cols], wg_buf.at[slot], wsem.at[0, slot]),
                pltpu.make_async_copy(wu_ref.at[layer, e, :, cols], wu_buf.at[slot], wsem.at[1, slot]),
                pltpu.make_async_copy(wd_ref.at[layer, e, cols, :], wd_buf.at[slot], wsem.at[2, slot]))

    depth = W_BUFFERS - 1

    def start_ahead(c, ahead):
        t, j = lax.div(c + ahead, n_chunks), lax.rem(c + ahead, n_chunks)

        @pl.when(t < n_used)
        def _():
            for cp in copies(t, j, lax.rem(c + ahead, W_BUFFERS)):
                cp.start()

    @pl.when(i == 0)
    def _():
        for ahead in range(depth):
            start_ahead(0, ahead)

    o_ref[...] = jnp.zeros_like(o_ref)

    def compute(n_rows):
        xb_ref[0:n_rows, :] = x_ref[0:n_rows, :].astype(BF16)

        def chunk_body(j, carry):
            c = i * n_chunks + j
            slot = lax.rem(c, W_BUFFERS)
            for cp in copies(i, j, slot):
                cp.wait()
            xb = xb_ref[0:n_rows, :]
            act = _silu(_dot(xb, wg_buf[slot].astype(BF16))) * _dot(xb, wu_buf[slot].astype(BF16))
            o_ref[0:n_rows, :] += _dot(act.astype(BF16), wd_buf[slot].astype(BF16))
            start_ahead(c, depth)
            return carry

        lax.fori_loop(0, n_chunks, chunk_body, 0)

    for n_rows in range(SUB, TM + 1, SUB):
        @pl.when(used & (valid > n_rows - SUB) & (valid <= n_rows))
        def _():
            compute(n_rows)


def _experts(xs, tile_expert, tile_valid, n_used, layer, we_gate, we_up, we_down):
    p, d = xs.shape
    dfe = we_gate.shape[-1]
    assert dfe % FC == 0 and p % TM == 0 and TM % SUB == 0
    n_chunks = dfe // FC
    n_tiles = p // TM

    grid_spec = pltpu.PrefetchScalarGridSpec(
        num_scalar_prefetch=3,
        grid=(n_tiles,),
        in_specs=[pl.BlockSpec((TM, d), lambda i, te, tv, nu: (jnp.minimum(i, nu[0] - 1), 0)),
                  pl.BlockSpec(memory_space=pl.ANY),
                  pl.BlockSpec(memory_space=pl.ANY),
                  pl.BlockSpec(memory_space=pl.ANY)],
        out_specs=pl.BlockSpec((TM, d), lambda i, te, tv, nu: (i, 0)),
        scratch_shapes=[pltpu.VMEM((TM, d), BF16), pltpu.VMEM((W_BUFFERS, d, FC), F32),
                        pltpu.VMEM((W_BUFFERS, d, FC), F32), pltpu.VMEM((W_BUFFERS, FC, d), F32),
                        pltpu.SemaphoreType.DMA((3, W_BUFFERS))],
    )
    return pl.pallas_call(
        functools.partial(_experts_kernel, layer=layer, n_chunks=n_chunks),
        grid_spec=grid_spec,
        out_shape=jax.ShapeDtypeStruct((p, d), F32),
        compiler_params=_params(1),
        name="expert_swiglu",
    )(tile_expert, tile_valid, n_used, xs, we_gate, we_up, we_down)


def _combine_kernel(dest_ref, ys_ref, info_ref, h_ref, mod_ref, gf_ref, o_ref, ybuf, sem, *, final_norm, n_tiles):
    i = pl.program_id(0)
    slot = lax.rem(i, 2)

    @pl.when(i < n_tiles)
    def _():
        _row_copies(ys_ref, ybuf.at[slot], dest_ref, sem.at[slot], scatter=False, wait_here=False)

    @pl.when(i > 0)
    def _():
        done = ybuf.at[1 - slot]
        pltpu.make_async_copy(done, done, sem.at[1 - slot]).wait()
        info = info_ref[...]
        d = h_ref.shape[1]
        mix = info[:, 4:5] * done[0].reshape(TBR, d) + info[:, 5:6] * done[1].reshape(TBR, d)
        h = h_ref[...] + mod_ref[...][5:6] * mix
        if final_norm:
            h = (h * lax.rsqrt(jnp.mean(h * h, axis=-1, keepdims=True) + EPS)) * gf_ref[...]
        o_ref[...] = h


def _combine(ys, dest, info, h, mod, g_final, *, n_tiles, tpl, nb, final_norm):
    n, d = h.shape
    prev = lambda i: (jnp.maximum(i - 1, 0), 0)
    return pl.pallas_call(
        functools.partial(_combine_kernel, final_norm=final_norm, n_tiles=n_tiles),
        grid=(n_tiles + 1,),
        in_specs=[pl.BlockSpec((None, 1, TOP_K * TBR), lambda i: (jnp.minimum(i, n_tiles - 1), 0, 0),
                               memory_space=pltpu.SMEM),
                  pl.BlockSpec(memory_space=pl.ANY),
                  pl.BlockSpec((TBR, LANE), prev),
                  pl.BlockSpec((TBR, d), prev),
                  pl.BlockSpec((None, N_MOD, d), lambda i: (jnp.minimum(jnp.maximum(i - 1, 0) // tpl, nb), 0, 0)),
                  pl.BlockSpec((1, d), lambda i: (0, 0))],
        out_specs=pl.BlockSpec((TBR, d), prev),
        out_shape=jax.ShapeDtypeStruct((n_tiles * TBR, d), F32),
        scratch_shapes=[pltpu.VMEM((2, TOP_K, TBR // SUBLANE, SUBLANE, d), F32), pltpu.SemaphoreType.DMA((2,))],
        compiler_params=_params(1),
        name="moe_combine",
    )(dest, ys, info, h, mod, g_final.reshape(1, d))


def _moe(f, info, rt, counts, h, mod, g_final, layer, we_gate, we_up, we_down, *, seq_l, nb, final_norm):
    n = f.shape[0]
    assert n % TBR == 0 and seq_l % TBR == 0 and TBR % TB == 0
    n_tiles, tpl = n // TBR, seq_l // TBR
    n_experts = we_gate.shape[1]
    n_xtiles = -(-TOP_K * n // TM) + n_experts
    n_rows = n_xtiles * TM

    cnt = counts[0, :n_experts].astype(I32)
    tiles_e = (cnt + TM - 1) // TM
    tile_end = jnp.cumsum(tiles_e)
    starts = (tile_end - tiles_e) * TM
    n_used = tile_end[-1:]
    tile_ids = jnp.arange(n_xtiles, dtype=I32)
    tile_expert = jnp.minimum(jnp.sum(tile_ids[:, None] >= tile_end[None, :], axis=1), n_experts - 1).astype(I32)
    tile_valid = jnp.clip(cnt[tile_expert] - (tile_ids * TM - starts[tile_expert]), 0, TM).astype(I32)
    choice = rt[:, 0:TOP_K, :]
    rank = rt[:, TOP_K:2 * TOP_K, :]
    start_of = sum(jnp.where(choice == e, starts[e], 0) for e in range(n_experts))
    dest = (start_of + rank).reshape(n_tiles, TBR // TB, TOP_K, TB).transpose(0, 2, 1, 3)
    dest = dest.reshape(n_tiles, 1, TOP_K * TBR)
    tail = n_used + jnp.arange(n_experts, dtype=I32)
    pad_at = jnp.concatenate([jnp.where(tiles_e > 0, tile_end - 1, -1),
                              jnp.where(tail < n_xtiles, tail, -1)]).astype(I32)

    xs = _dispatch(f, dest, pad_at, n_tiles=n_tiles, n_rows=n_rows)
    ys = _experts(xs, tile_expert, tile_valid, n_used, layer, we_gate, we_up, we_down)
    return _combine(ys, dest, info, h, mod, g_final, n_tiles=n_tiles, tpl=tpl, nb=nb, final_norm=final_norm)


def kernel(x, c, ctx, c_ctx, w_mod, b_mod, g_mix, g_ffn, g_final, w_in_ab, conv_w, pool_w, pool_scale,
           w_out_ab, w_ff_gate, w_ff_up, w_ff_down, w_qkv, rpb, w_out_na, w_router, we_gate, we_up, we_down):
    nb, seq_l, d = x.shape
    seq_c = ctx.shape[1]
    depth = w_mod.shape[0]
    heads = rpb.shape[1]
    na_rows = (rpb.shape[2] + 1) // 2
    na_cols = (rpb.shape[3] + 1) // 2
    assert seq_l % TB == 0 and seq_c % TB == 0 and nb + 1 <= SUBLANE and depth % 2 == 0
    tpl = seq_l // TB
    nlat = nb * tpl
    nall = nlat + nb * seq_c // TB

    cond = jnp.zeros((SUBLANE, d), F32).at[:nb].set(c).at[nb].set(c_ctx)
    mods = _adaln(cond, w_mod, b_mod).reshape(depth, SUBLANE, N_MOD, d)

    bf = lambda a: a.astype(BF16)
    h = out = None
    for i in range(depth):
        j = i // 2
        last = i == depth - 1
        mod = mods[i]
        if i % 2 == 0:
            srcs = (x.reshape(nb * seq_l, d), ctx.reshape(nb * seq_c, d), 0) if i == 0 else (h, h, nlat)
            h = _mixer(*srcs, mod, g_mix[i], j, w_in_ab, conv_w, pool_w, pool_scale[j], w_out_ab,
                       n_tiles=nall, nlat=nlat, nb=nb, seq_l=seq_l, seq_c=seq_c)
            h = _ffn(h, mod, g_ffn[i], bf(w_ff_gate[j]), bf(w_ff_up[j]), bf(w_ff_down[j]),
                     n_rows=nall * TB, seq_l=seq_l, nb=nb)
        else:
            n_tiles = nlat if last else nall
            q, k, v = _qkv(h, mod, g_mix[i], j, w_qkv, n_rows=nall * TB, seq_l=seq_l, nb=nb,
                           qscale=(d // heads) ** -0.5)
            o_lat, o_ctx = _natten(q, k, v, _bias_table(rpb[j], na_cols), nb=nb, seq_l=seq_l, seq_c=seq_c,
                                   heads=heads, na_rows=na_rows)
            h, f, info, rt, counts = _route(o_lat, o_ctx, h, mod, g_ffn[i], j, w_out_na, w_router[j],
                                            n_rows=n_tiles * TB, seq_l=seq_l, nb=nb)
            h = _moe(f, info, rt, counts, h, mod, g_final, j, we_gate, we_up, we_down,
                     seq_l=seq_l, nb=nb, final_norm=last)
            if last:
                out = h
    return out.reshape(nb, seq_l, d)
```
